```python
import math
import jax, jax.numpy as jnp
from jax import lax
import numpy as np

D_MODEL = 2048
BATCH = 2
SEQ = 8192
DEPTH = 4

HEAD_DIM = 128
N_RET_HEADS = D_MODEL // (2 * HEAD_DIM)
N_NA_HEADS = D_MODEL // (2 * HEAD_DIM)
N_DIFF_HEADS = D_MODEL // (2 * HEAD_DIM)
N_X_HEADS = 4
X_HEAD_DIM = 128
N_MEM = 256
GRID_W = 64
NA_WIN_ROWS = 8
NA_WIN_COLS = 16
RET_CHUNK = 128
ATTN_BLOCK = 128
FFN_HIDDEN = -((-8 * D_MODEL) // (3 * 256)) * 256
RMS_EPS = 1e-6

kernel_name = "hybrid_retention_na_diffattn_encoder"


def rms_norm(x, g):
    xf = x.astype(jnp.float32)
    y = xf * lax.rsqrt(jnp.mean(xf * xf, axis=-1, keepdims=True) + RMS_EPS)
    return (y * g.astype(jnp.float32)).astype(x.dtype)


def alibi_slopes(n):
    return 2.0 ** (-8.0 * jnp.arange(1, n + 1, dtype=jnp.float32) / n)


def chunk_retention(q, k, v, log_gamma, include_diag):
    b, h, s, d = q.shape
    dv = v.shape[-1]
    n = s // RET_CHUNK

    def to_chunks(t):
        return t.reshape(b, h, n, RET_CHUNK, t.shape[-1]).transpose(2, 0, 1, 3, 4)

    idx = jnp.arange(RET_CHUNK, dtype=jnp.float32)
    diff = idx[:, None] - idx[None, :]
    mask = (diff >= 0) if include_diag else (diff > 0)
    d_intra = jnp.where(mask, jnp.exp(log_gamma[:, None, None] * jnp.maximum(diff, 0.0)), 0.0)
    q_decay = jnp.exp(log_gamma[:, None] * (idx + 1.0))[:, :, None]
    k_decay = jnp.exp(log_gamma[:, None] * (RET_CHUNK - 1.0 - idx))[:, :, None]
    c_decay = jnp.exp(log_gamma * RET_CHUNK)[:, None, None]

    def step(state, qkv):
        qc, kc, vc = qkv
        scores = jnp.einsum('bhid,bhjd->bhij', qc, kc) * d_intra
        inner = jnp.einsum('bhij,bhje->bhie', scores, vc)
        cross = jnp.einsum('bhid,bhde->bhie', qc * q_decay, state)
        state = state * c_decay + jnp.einsum('bhjd,bhje->bhde', kc * k_decay, vc)
        return state, inner + cross

    state0 = jnp.zeros((b, h, d, dv), jnp.float32)
    _, out = lax.scan(step, state0, (to_chunks(q), to_chunks(k), to_chunks(v)))
    return out.transpose(1, 2, 0, 3, 4).reshape(b, h, s, dv)


def neighborhood_attention(q, k, v, rpb):
    b, h, s, d = q.shape
    rows = s // GRID_W
    wr = min(NA_WIN_ROWS, rows)
    qg = q.reshape(b, h, rows, GRID_W, d)
    kg = k.reshape(b, h, rows, GRID_W, d)
    vg = v.reshape(b, h, rows, GRID_W, d)
    col = jnp.arange(GRID_W)
    cs = jnp.clip(col - NA_WIN_COLS // 2, 0, GRID_W - NA_WIN_COLS)
    colidx = cs[:, None] + jnp.arange(NA_WIN_COLS)[None, :]
    dc_idx = colidx - col[:, None] + NA_WIN_COLS - 1
    scale = HEAD_DIM ** -0.5

    def row_fn(r):
        rs = jnp.clip(r - wr // 2, 0, rows - wr)
        krow = lax.dynamic_slice_in_dim(kg, rs, wr, axis=2)
        vrow = lax.dynamic_slice_in_dim(vg, rs, wr, axis=2)
        kw = krow[:, :, :, colidx, :]
        vw = vrow[:, :, :, colidx, :]
        qr = lax.dynamic_index_in_dim(qg, r, axis=2, keepdims=False)
        logits = jnp.einsum('bhcd,bhrcjd->bhcrj', qr, kw).astype(jnp.float32) * scale
        dr_idx = rs + jnp.arange(wr) - r + NA_WIN_ROWS - 1
        bias = rpb[:, dr_idx][:, :, dc_idx].astype(jnp.float32)
        logits = logits + bias.transpose(0, 2, 1, 3)[None]
        p = jax.nn.softmax(logits.reshape(b, h, GRID_W, wr * NA_WIN_COLS), axis=-1)
        p = p.reshape(b, h, GRID_W, wr, NA_WIN_COLS)
        return jnp.einsum('bhcrj,bhrcjd->bhcd', p, vw.astype(jnp.float32)).astype(v.dtype)

    out = lax.map(row_fn, jnp.arange(rows))
    return out.transpose(1, 2, 0, 3, 4).reshape(b, h, s, d)


def retention_na_mixer(h, w_in, dec_f, dec_b, ret_g, na_q_g, na_k_g, rpb, w_out):
    b, s, _ = h.shape
    proj = h @ w_in
    rq, rk, rv, rg, nq, nk, nv = jnp.split(proj, 7, axis=-1)

    def heads(t):
        return t.reshape(b, s, -1, HEAD_DIM).transpose(0, 2, 1, 3)

    f32 = jnp.float32
    rq = heads(rq).astype(f32)
    rk = heads(rk).astype(f32) * (HEAD_DIM ** -0.5)
    rv = heads(rv).astype(f32)
    lg_f = jax.nn.log_sigmoid(dec_f.astype(f32))
    lg_b = jax.nn.log_sigmoid(dec_b.astype(f32))
    fwd = chunk_retention(rq, rk, rv, lg_f, True)
    bwd = jnp.flip(chunk_retention(jnp.flip(rq, 2), jnp.flip(rk, 2), jnp.flip(rv, 2), lg_b, False), 2)
    ret = rms_norm(fwd + bwd, ret_g)
    ret = ret.transpose(0, 2, 1, 3).reshape(b, s, -1).astype(h.dtype) * jax.nn.silu(rg)

    na = neighborhood_attention(rms_norm(heads(nq), na_q_g), rms_norm(heads(nk), na_k_g), heads(nv), rpb)
    na = na.transpose(0, 2, 1, 3).reshape(b, s, -1)
    return jnp.concatenate([ret, na], axis=-1) @ w_out


def diff_attention_mixer(h, w_in, q_g, k_g, lq1, lk1, lq2, lk2, out_g, w_out, layer_idx):
    b, s, _ = h.shape
    f32 = jnp.float32
    proj = h @ w_in
    q, k, v = jnp.split(proj, 3, axis=-1)
    q = rms_norm(q.reshape(b, s, N_DIFF_HEADS, 2, HEAD_DIM), q_g).transpose(3, 0, 2, 1, 4)
    k = rms_norm(k.reshape(b, s, N_DIFF_HEADS, 2, HEAD_DIM), k_g).transpose(3, 0, 2, 1, 4)
    vf = v.reshape(b, s, N_DIFF_HEADS, 2 * HEAD_DIM).transpose(0, 2, 1, 3).astype(f32)
    lam_init = 0.8 - 0.6 * math.exp(-0.3 * layer_idx)
    lam = (jnp.exp(jnp.sum(lq1.astype(f32) * lk1.astype(f32)))
           - jnp.exp(jnp.sum(lq2.astype(f32) * lk2.astype(f32))) + lam_init)
    slopes = alibi_slopes(N_DIFF_HEADS)
    scale = HEAD_DIM ** -0.5
    nb = s // ATTN_BLOCK
    qb = q.reshape(2, b, N_DIFF_HEADS, nb, ATTN_BLOCK, HEAD_DIM).transpose(3, 0, 1, 2, 4, 5)
    key_pos = jnp.arange(s, dtype=f32)

    def block_fn(args):
        qblk, bi = args
        qpos = (bi * ATTN_BLOCK + jnp.arange(ATTN_BLOCK)).astype(f32)
        alibi = -slopes[:, None, None] * jnp.abs(qpos[:, None] - key_pos[None, :])
        logits = jnp.einsum('cbhqd,cbhkd->cbhqk', qblk, k).astype(f32) * scale + alibi
        p = jax.nn.softmax(logits, axis=-1)
        a = p[0] - lam * p[1]
        return jnp.einsum('bhqk,bhke->bhqe', a, vf)

    out = lax.map(block_fn, (qb, jnp.arange(nb)))
    out = out.transpose(1, 2, 0, 3, 4).reshape(b, N_DIFF_HEADS, s, 2 * HEAD_DIM)
    out = rms_norm(out, out_g) * (1.0 - lam_init)
    out = out.transpose(0, 2, 1, 3).reshape(b, s, -1).astype(h.dtype)
    return out @ w_out


def memory_cross_attention(h, m, w_q, w_kv, w_o, q_g, k_g):
    b, s, _ = h.shape
    nm = m.shape[1]
    q = rms_norm((h @ w_q).reshape(b, s, N_X_HEADS, X_HEAD_DIM), q_g)
    kv = (m @ w_kv).reshape(b, nm, 2, N_X_HEADS, X_HEAD_DIM)
    k = rms_norm(kv[:, :, 0], k_g)
    v = kv[:, :, 1]
    logits = jnp.einsum('bshd,bmhd->bhsm', q, k).astype(jnp.float32) * (X_HEAD_DIM ** -0.5)
    p = jax.nn.softmax(logits, axis=-1)
    o = jnp.einsum('bhsm,bmhd->bshd', p, v.astype(jnp.float32)).astype(h.dtype)
    return o.reshape(b, s, -1) @ w_o


def swiglu_ffn(h, w_in, w_out):
    g, u = jnp.split(h @ w_in, 2, axis=-1)
    return (jax.nn.silu(g) * u) @ w_out


def setup_inputs(seed: int = 0) -> dict:
    key = jax.random.key(seed)
    ks = list(jax.random.split(key, 32))
    n_even = (DEPTH + 1) // 2
    n_odd = DEPTH // 2
    f32 = jnp.float32

    def w(k, shape, fan_in):
        return jax.random.normal(k, shape, f32) * (fan_in ** -0.5)

    def gain(k, shape):
        return 1.0 + 0.02 * jax.random.normal(k, shape, f32)

    def small(k, shape, sc):
        return sc * jax.random.normal(k, shape, f32)

    ab_width = (N_RET_HEADS + N_NA_HEADS) * HEAD_DIM
    ab_in_cols = 4 * N_RET_HEADS * HEAD_DIM + 3 * N_NA_HEADS * HEAD_DIM
    c_width = N_DIFF_HEADS * 2 * HEAD_DIM
    x_width = N_X_HEADS * X_HEAD_DIM
    base_logit = jnp.asarray(np.log(2.0 ** (5 + np.arange(N_RET_HEADS)) - 1.0), f32)

    return {
        "x": jax.random.normal(ks[0], (BATCH, SEQ, D_MODEL), f32),
        "mem": jax.random.normal(ks[1], (BATCH, N_MEM, D_MODEL), f32),
        "norm_mix_g": gain(ks[2], (DEPTH, D_MODEL)),
        "norm_xattn_g": gain(ks[3], (DEPTH, D_MODEL)),
        "norm_mem_g": gain(ks[4], (DEPTH, D_MODEL)),
        "norm_ffn_g": gain(ks[5], (DEPTH, D_MODEL)),
        "w_in_ab": w(ks[6], (n_even, D_MODEL, ab_in_cols), D_MODEL),
        "ret_decay_fwd": base_logit[None] + small(ks[7], (n_even, N_RET_HEADS), 0.05),
        "ret_decay_bwd": base_logit[None] + small(ks[8], (n_even, N_RET_HEADS), 0.05),
        "ret_out_g": gain(ks[9], (n_even, HEAD_DIM)),
        "na_q_g": gain(ks[10], (n_even, HEAD_DIM)),
        "na_k_g": gain(ks[11], (n_even, HEAD_DIM)),
        "na_rpb": small(ks[12], (n_even, N_NA_HEADS, 2 * NA_WIN_ROWS - 1, 2 * NA_WIN_COLS - 1), 0.1),
        "w_out_ab": w(ks[13], (n_even, ab_width, D_MODEL), ab_width),
        "w_in_c": w(ks[14], (n_odd, D_MODEL, 3 * c_width), D_MODEL),
        "diff_q_g": gain(ks[15], (n_odd, HEAD_DIM)),
        "diff_k_g": gain(ks[16], (n_odd, HEAD_DIM)),
        "lambda_q1": small(ks[17], (n_odd, HEAD_DIM), 0.1),
        "lambda_k1": small(ks[18], (n_odd, HEAD_DIM), 0.1),
        "lambda_q2": small(ks[19], (n_odd, HEAD_DIM), 0.1),
        "lambda_k2": small(ks[20], (n_odd, HEAD_DIM), 0.1),
        "diff_out_g": gain(ks[21], (n_odd, 2 * HEAD_DIM)),
        "w_out_c": w(ks[22], (n_odd, c_width, D_MODEL), c_width),
        "w_xq": w(ks[23], (DEPTH, D_MODEL, x_width), D_MODEL),
        "w_xkv": w(ks[24], (DEPTH, D_MODEL, 2 * x_width), D_MODEL),
        "w_xo": w(ks[25], (DEPTH, x_width, D_MODEL), x_width),
        "xq_g": gain(ks[26], (DEPTH, X_HEAD_DIM)),
        "xk_g": gain(ks[27], (DEPTH, X_HEAD_DIM)),
        "w_ffn_in": w(ks[28], (DEPTH, D_MODEL, 2 * FFN_HIDDEN), D_MODEL),
        "w_ffn_out": w(ks[29], (DEPTH, FFN_HIDDEN, D_MODEL), FFN_HIDDEN),
    }


def reference(x, mem, norm_mix_g, norm_xattn_g, norm_mem_g, norm_ffn_g,
              w_in_ab, ret_decay_fwd, ret_decay_bwd, ret_out_g, na_q_g, na_k_g, na_rpb, w_out_ab,
              w_in_c, diff_q_g, diff_k_g, lambda_q1, lambda_k1, lambda_q2, lambda_k2, diff_out_g, w_out_c,
              w_xq, w_xkv, w_xo, xq_g, xk_g, w_ffn_in, w_ffn_out):
    for i in range(DEPTH):
        j = i // 2
        h = rms_norm(x, norm_mix_g[i])
        if i % 2 == 0:
            x = x + retention_na_mixer(h, w_in_ab[j], ret_decay_fwd[j], ret_decay_bwd[j], ret_out_g[j],
                                       na_q_g[j], na_k_g[j], na_rpb[j], w_out_ab[j])
        else:
            x = x + diff_attention_mixer(h, w_in_c[j], diff_q_g[j], diff_k_g[j], lambda_q1[j], lambda_k1[j],
                                         lambda_q2[j], lambda_k2[j], diff_out_g[j], w_out_c[j], i)
        m = rms_norm(mem, norm_mem_g[i])
        x = x + memory_cross_attention(rms_norm(x, norm_xattn_g[i]), m, w_xq[i], w_xkv[i], w_xo[i], xq_g[i], xk_g[i])
        x = x + swiglu_ffn(rms_norm(x, norm_ffn_g[i]), w_ffn_in[i], w_ffn_out[i])
    return x
```

```python
import functools
import math

import jax
import jax.numpy as jnp
import numpy as np
from jax import lax
from jax.experimental import pallas as pl
from jax.experimental.pallas import tpu as pltpu

F32 = jnp.float32
BF16 = jnp.bfloat16

HEAD_DIM = 128
N_HEADS = 8
N_X_HEADS = 4
GRID_W = 64
NA_WIN_ROWS = 8
NA_WIN_COLS = 16
NA_QROWS = 4
RET_CHUNK = 256
RMS_EPS = 1e-6
NEG_BIG = -1e30

LANE = 128
VMEM_LIMIT = 56 * 1024 * 1024


def _cparams(sem):
    return pltpu.CompilerParams(dimension_semantics=sem, vmem_limit_bytes=VMEM_LIMIT)


def _dot_nt(a, b):
    return lax.dot_general(a, b, (((1,), (1,)), ((), ())), preferred_element_type=F32)


def _dot_tn(a, b):
    return lax.dot_general(a, b, (((0,), (0,)), ((), ())), preferred_element_type=F32)


def _rms(x):
    return x * lax.rsqrt(jnp.mean(x * x, axis=-1, keepdims=True) + RMS_EPS)


def _norm_rows_to_scratch(x_ref, g_ref, xn_ref, rows):
    n = x_ref.shape[0] // rows

    def body(r, c):
        sl = pl.ds(pl.multiple_of(r * rows, rows), rows)
        xn_ref[sl, :] = (_rms(x_ref[sl, :]) * g_ref[...]).astype(BF16)
        return c

    lax.fori_loop(0, n, body, 0)


def _norm_matmul_kernel(x_ref, g_ref, w_ref, gc_ref, o_ref, xn_ref, *, lo, hi):
    j = pl.program_id(1)

    @pl.when(j == 0)
    def _():
        _norm_rows_to_scratch(x_ref, g_ref, xn_ref, 128)

    acc = jnp.dot(xn_ref[...], w_ref[...], preferred_element_type=F32)
    if lo == hi:
        o_ref[...] = acc.astype(o_ref.dtype)
        return
    grouped = jnp.logical_and(j >= lo, j < hi)

    @pl.when(grouped)
    def _():
        for c in range(acc.shape[1] // HEAD_DIM):
            sl = slice(c * HEAD_DIM, (c + 1) * HEAD_DIM)
            o_ref[:, sl] = (_rms(acc[:, sl]) * gc_ref[:, sl]).astype(o_ref.dtype)

    @pl.when(jnp.logical_not(grouped))
    def _():
        o_ref[...] = acc.astype(o_ref.dtype)


def norm_matmul(x, g, w, gcol, lo, hi, *, tm, tn, out_dtype=BF16):
    t, d = x.shape
    n = w.shape[1]
    return pl.pallas_call(
        functools.partial(_norm_matmul_kernel, lo=lo, hi=hi),
        grid=(t // tm, n // tn),
        in_specs=[
            pl.BlockSpec((tm, d), lambda i, j: (i, 0)),
            pl.BlockSpec((1, d), lambda i, j: (0, 0)),
            pl.BlockSpec((d, tn), lambda i, j: (0, j)),
            pl.BlockSpec((1, tn), lambda i, j: (0, j)),
        ],
        out_specs=pl.BlockSpec((tm, tn), lambda i, j: (i, j)),
        out_shape=jax.ShapeDtypeStruct((t, n), out_dtype),
        scratch_shapes=[pltpu.VMEM((tm, d), BF16)],
        compiler_params=_cparams(("parallel", "arbitrary")),
        name="norm_matmul",
    )(x, g, w, gcol)


def _swiglu_in_kernel(x_ref, g_ref, wg_ref, wu_ref, o_ref, xn_ref):
    @pl.when(pl.program_id(1) == 0)
    def _():
        _norm_rows_to_scratch(x_ref, g_ref, xn_ref, 128)

    xn = xn_ref[...]
    gate = jnp.dot(xn, wg_ref[...], preferred_element_type=F32)
    up = jnp.dot(xn, wu_ref[...], preferred_element_type=F32)
    o_ref[...] = (gate * jax.nn.sigmoid(gate) * up).astype(o_ref.dtype)


def swiglu_in(x, g, w_in, *, tm, tn):
    t, d = x.shape
    hidden = w_in.shape[1] // 2
    nh = hidden // tn
    return pl.pallas_call(
        _swiglu_in_kernel,
        grid=(t // tm, nh),
        in_specs=[
            pl.BlockSpec((tm, d), lambda i, j: (i, 0)),
            pl.BlockSpec((1, d), lambda i, j: (0, 0)),
            pl.BlockSpec((d, tn), lambda i, j: (0, j)),
            pl.BlockSpec((d, tn), lambda i, j: (0, j + nh)),
        ],
        out_specs=pl.BlockSpec((tm, tn), lambda i, j: (i, j)),
        out_shape=jax.ShapeDtypeStruct((t, hidden), BF16),
        scratch_shapes=[pltpu.VMEM((tm, d), BF16)],
        compiler_params=_cparams(("parallel", "arbitrary")),
        name="swiglu_in",
    )(x, g, w_in, w_in)


def _matmul_res_kernel(a_ref, w_ref, r_ref, o_ref):
    o_ref[...] = r_ref[...] + jnp.dot(a_ref[...], w_ref[...], preferred_element_type=F32)


def matmul_res(a, w, res, *, tm, tn):
    t, k = a.shape
    n = w.shape[1]
    return pl.pallas_call(
        _matmul_res_kernel,
        grid=(t // tm, n // tn),
        in_specs=[
            pl.BlockSpec((tm, k), lambda i, j: (i, 0)),
            pl.BlockSpec((k, tn), lambda i, j: (0, j)),
            pl.BlockSpec((tm, tn), lambda i, j: (i, j)),
        ],
        out_specs=pl.BlockSpec((tm, tn), lambda i, j: (i, j)),
        out_shape=jax.ShapeDtypeStruct((t, n), F32),
        compiler_params=_cparams(("parallel", "arbitrary")),
        name="matmul_res",
    )(a, w, res)


def _xattn_kernel(x_ref, g_ref, wq_ref, gq_ref, kv_ref, wo_ref, o_ref):
    x = x_ref[...]
    xn = (_rms(x) * g_ref[...]).astype(BF16)
    q = jnp.dot(xn, wq_ref[...], preferred_element_type=F32)
    width = N_X_HEADS * HEAD_DIM
    outs = []
    for h in range(N_X_HEADS):
        sl = slice(h * HEAD_DIM, (h + 1) * HEAD_DIM)
        qh = (_rms(q[:, sl]) * gq_ref[:, sl]).astype(BF16)
        s = _dot_nt(qh, kv_ref[:, sl])
        p = jnp.exp(s - jnp.max(s, axis=-1, keepdims=True))
        l = jnp.sum(p, axis=-1, keepdims=True)
        vh = kv_ref[:, width + h * HEAD_DIM: width + (h + 1) * HEAD_DIM]
        oh = jnp.dot(p.astype(BF16), vh, preferred_element_type=F32) / l
        outs.append(oh.astype(BF16))
    o = jnp.concatenate(outs, axis=1)
    o_ref[...] = x + jnp.dot(o, wo_ref[...], preferred_element_type=F32)


def xattn(x, g, wq, gq, kv, wo, *, tm, seq):
    t, d = x.shape
    n_mem = kv.shape[0] // (t // seq)
    width = wq.shape[1]
    per_b = seq // tm
    return pl.pallas_call(
        _xattn_kernel,
        grid=(t // tm,),
        in_specs=[
            pl.BlockSpec((tm, d), lambda i: (i, 0)),
            pl.BlockSpec((1, d), lambda i: (0, 0)),
            pl.BlockSpec((d, width), lambda i: (0, 0)),
            pl.BlockSpec((1, width), lambda i: (0, 0)),
            pl.BlockSpec((n_mem, 2 * width), lambda i: (i // per_b, 0)),
            pl.BlockSpec((width, d), lambda i: (0, 0)),
        ],
        out_specs=pl.BlockSpec((tm, d), lambda i: (i, 0)),
        out_shape=jax.ShapeDtypeStruct((t, d), F32),
        compiler_params=_cparams(("parallel",)),
        name="xattn",
    )(x, g, wq, gq, kv, wo)


def _ret_state_kernel(k_ref, v_ref, lgb_ref, sb_ref, state_ref, kd_ref):
    c = pl.program_id(1)
    chunk = k_ref.shape[1]
    scale = HEAD_DIM ** -0.5

    @pl.when(c == 0)
    def _():
        state_ref[...] = jnp.zeros_like(state_ref)
        pos = lax.broadcasted_iota(jnp.int32, (chunk, HEAD_DIM), 0).astype(F32)
        for h in range(N_HEADS):
            kd_ref[h] = jnp.exp(lgb_ref[h:h + 1, :] * pos) * scale

    for h in range(N_HEADS):
        sl = slice(h * HEAD_DIM, (h + 1) * HEAD_DIM)
        st = state_ref[h]
        sb_ref[0, 0, h] = st.astype(BF16)
        kd = (k_ref[0, :, sl].astype(F32) * kd_ref[h]).astype(BF16)
        cdec = jnp.exp(lgb_ref[h:h + 1, :] * float(chunk))
        state_ref[h] = st * cdec + _dot_tn(kd, v_ref[0, :, sl])


def ret_states(proj, lgb, *, chunk):
    b, s, _ = proj.shape
    nc = s // chunk
    width = N_HEADS * HEAD_DIM
    return pl.pallas_call(
        _ret_state_kernel,
        grid=(b, nc),
        in_specs=[
            pl.BlockSpec((1, chunk, width), lambda bi, c: (bi, nc - 1 - c, 1)),
            pl.BlockSpec((1, chunk, width), lambda bi, c: (bi, nc - 1 - c, 2)),
            pl.BlockSpec((N_HEADS, LANE), lambda bi, c: (0, 0)),
        ],
        out_specs=pl.BlockSpec((1, 1, N_HEADS, HEAD_DIM, HEAD_DIM), lambda bi, c: (bi, nc - 1 - c, 0, 0, 0)),
        out_shape=jax.ShapeDtypeStruct((b, nc, N_HEADS, HEAD_DIM, HEAD_DIM), BF16),
        scratch_shapes=[pltpu.VMEM((N_HEADS, HEAD_DIM, HEAD_DIM), F32),
                        pltpu.VMEM((N_HEADS, chunk, HEAD_DIM), F32)],
        compiler_params=_cparams(("parallel", "arbitrary")),
        name="ret_states",
    )(proj, proj, lgb)


def _ret_out_kernel(q_ref, k_ref, v_ref, rg_ref, sb_ref, lgf_ref, lgb_ref, og_ref, o_ref,
                    state_ref, dmat_ref, qdf_ref, qdb_ref, kdf_ref):
    c = pl.program_id(1)
    chunk = q_ref.shape[1]
    scale = HEAD_DIM ** -0.5

    @pl.when(c == 0)
    def _():
        state_ref[...] = jnp.zeros_like(state_ref)
        pos = lax.broadcasted_iota(jnp.int32, (chunk, HEAD_DIM), 0).astype(F32)
        ri = lax.broadcasted_iota(jnp.int32, (chunk, chunk), 0)
        ci = lax.broadcasted_iota(jnp.int32, (chunk, chunk), 1)
        diff = (ri - ci).astype(F32)
        for h in range(N_HEADS):
            lf = lgf_ref[h:h + 1, :]
            lb = lgb_ref[h:h + 1, :]
            dfwd = jnp.exp(lf[:, :1] * jnp.maximum(diff, 0.0))
            dbwd = jnp.exp(lb[:, :1] * jnp.maximum(-diff, 0.0))
            dmat_ref[h] = jnp.where(diff >= 0.0, dfwd, dbwd) * scale
            qdf_ref[h] = jnp.exp(lf * (pos + 1.0))
            qdb_ref[h] = jnp.exp(lb * (float(chunk) - pos))
            kdf_ref[h] = jnp.exp(lf * (float(chunk) - 1.0 - pos)) * scale

    for h in range(N_HEADS):
        sl = slice(h * HEAD_DIM, (h + 1) * HEAD_DIM)
        qh = q_ref[0, :, sl]
        kh = k_ref[0, :, sl]
        vh = v_ref[0, :, sl]
        qf32 = qh.astype(F32)
        sd = (_dot_nt(qh, kh) * dmat_ref[h]).astype(BF16)
        st = state_ref[h]
        out = jnp.dot(sd, vh, preferred_element_type=F32)
        out += jnp.dot((qf32 * qdf_ref[h]).astype(BF16), st.astype(BF16), preferred_element_type=F32)
        out += jnp.dot((qf32 * qdb_ref[h]).astype(BF16), sb_ref[0, 0, h], preferred_element_type=F32)
        gate = rg_ref[0, :, sl].astype(F32)
        y = _rms(out) * og_ref[...]
        o_ref[0, :, sl] = (y * (gate * jax.nn.sigmoid(gate))).astype(o_ref.dtype)
        kd = (kh.astype(F32) * kdf_ref[h]).astype(BF16)
        cdec = jnp.exp(lgf_ref[h:h + 1, :] * float(chunk))
        state_ref[h] = st * cdec + _dot_tn(kd, vh)


def ret_out(proj, sb, lgf, lgb, og, *, chunk):
    b, s, _ = proj.shape
    nc = s // chunk
    width = N_HEADS * HEAD_DIM
    col = lambda k: pl.BlockSpec((1, chunk, width), lambda bi, c, k=k: (bi, c, k))
    return pl.pallas_call(
        _ret_out_kernel,
        grid=(b, nc),
        in_specs=[
            col(0), col(1), col(2), col(3),
            pl.BlockSpec((1, 1, N_HEADS, HEAD_DIM, HEAD_DIM), lambda bi, c: (bi, c, 0, 0, 0)),
            pl.BlockSpec((N_HEADS, LANE), lambda bi, c: (0, 0)),
            pl.BlockSpec((N_HEADS, LANE), lambda bi, c: (0, 0)),
            pl.BlockSpec((1, HEAD_DIM), lambda bi, c: (0, 0)),
        ],
        out_specs=pl.BlockSpec((1, chunk, width), lambda bi, c: (bi, c, 0)),
        out_shape=jax.ShapeDtypeStruct((b, s, width), BF16),
        scratch_shapes=[pltpu.VMEM((N_HEADS, HEAD_DIM, HEAD_DIM), F32),
                        pltpu.VMEM((N_HEADS, chunk, chunk), F32),
                        pltpu.VMEM((N_HEADS, chunk, HEAD_DIM), F32),
                        pltpu.VMEM((N_HEADS, chunk, HEAD_DIM), F32),
                        pltpu.VMEM((N_HEADS, chunk, HEAD_DIM), F32)],
        compiler_params=_cparams(("parallel", "arbitrary")),
        name="ret_out",
    )(proj, proj, proj, proj, sb, lgf, lgb, og)


def _na_slab_indices(rows):
    assert NA_QROWS >= NA_WIN_ROWS // 2 and rows >= 3 * NA_QROWS and rows >= NA_WIN_ROWS
    nq = NA_QROWS * GRID_W
    qj, qc = np.divmod(np.arange(nq), GRID_W)
    ki, kc = np.divmod(np.arange(3 * nq), GRID_W)
    cs = np.clip(qc - NA_WIN_COLS // 2, 0, GRID_W - NA_WIN_COLS)
    col_ok = (kc[None, :] >= cs[:, None]) & (kc[None, :] < cs[:, None] + NA_WIN_COLS)
    dc = kc[None, :] - qc[:, None]
    dr = ki[None, :] - NA_QROWS - qj[:, None]
    row_ok = []
    for qb in (0, 1, rows // NA_QROWS - 1):
        r = NA_QROWS * qb + qj[:, None]
        kr = NA_QROWS * (qb - 1) + ki[None, :]
        rs = np.clip(r - NA_WIN_ROWS // 2, 0, rows - NA_WIN_ROWS)
        row_ok.append((kr >= rs) & (kr < rs + NA_WIN_ROWS))
    ok = np.stack(row_ok) & col_ok[None]
    dr_idx = np.clip(dr + NA_WIN_ROWS - 1, 0, 2 * NA_WIN_ROWS - 2)
    dc_idx = np.clip(dc + NA_WIN_COLS - 1, 0, 2 * NA_WIN_COLS - 2)
    return ok, dr_idx, dc_idx


def na_bias_slab(rpb, rows):
    ok, dr_idx, dc_idx = _na_slab_indices(rows)
    bias = rpb.astype(F32)[:, dr_idx, dc_idx]
    return jnp.where(ok[None], bias[:, None], NEG_BIG)


def _na_kernel(q_ref, kp_ref, kc_ref, kn_ref, vp_ref, vc_ref, vn_ref, slab_ref, o_ref):
    k = jnp.concatenate([kp_ref[0], kc_ref[0], kn_ref[0]], axis=0)
    v = jnp.concatenate([vp_ref[0], vc_ref[0], vn_ref[0]], axis=0)
    s = _dot_nt(q_ref[0], k) + slab_ref[0, 0]
    p = jnp.exp(s - jnp.max(s, axis=-1, keepdims=True))
    l = jnp.sum(p, axis=-1, keepdims=True)
    o = jnp.dot(p.astype(BF16), v, preferred_element_type=F32) / l
    o_ref[0] = o.astype(o_ref.dtype)


def neighborhood_attention(proj, slab):
    b, s, _ = proj.shape
    nq = NA_QROWS * GRID_W
    nb = s // nq
    qoff, koff, voff = 4 * N_HEADS, 5 * N_HEADS, 6 * N_HEADS

    def blk(off, shift):
        return pl.BlockSpec((1, nq, HEAD_DIM),
                            lambda bi, h, qb: (bi, jnp.clip(qb + shift, 0, nb - 1), off + h))

    def kind(qb):
        return jnp.where(qb == 0, 0, jnp.where(qb == nb - 1, 2, 1))

    return pl.pallas_call(
        _na_kernel,
        grid=(b, N_HEADS, nb),
        in_specs=[
            blk(qoff, 0),
            blk(koff, -1), blk(koff, 0), blk(koff, 1),
            blk(voff, -1), blk(voff, 0), blk(voff, 1),
            pl.BlockSpec((1, 1, nq, 3 * nq), lambda bi, h, qb: (h, kind(qb), 0, 0)),
        ],
        out_specs=pl.BlockSpec((1, nq, HEAD_DIM), lambda bi, h, qb: (bi, qb, h)),
        out_shape=jax.ShapeDtypeStruct((b, s, N_HEADS * HEAD_DIM), BF16),
        compiler_params=_cparams(("parallel", "parallel", "arbitrary")),
        name="neighborhood_attention",
    )(proj, proj, proj, proj, proj, proj, proj, slab)


def _diff_attn_kernel(slopes_ref, q_ref, k_ref, v_ref, lam_ref, og_ref, o_ref,
                      e_ref, acc_ref, m_ref, l_ref, *, lam_init):
    h = pl.program_id(1)
    qi = pl.program_id(2)
    t = q_ref.shape[1]
    nk = k_ref.shape[1] // t
    slope = slopes_ref[h]

    ri = lax.broadcasted_iota(jnp.int32, (t, t), 0)
    ci = lax.broadcasted_iota(jnp.int32, (t, t), 1)
    e_ref[...] = (ci - ri).astype(F32) * slope
    acc_ref[...] = jnp.zeros_like(acc_ref)
    m_ref[...] = jnp.full_like(m_ref, NEG_BIG)
    l_ref[...] = jnp.zeros_like(l_ref)

    def block(kj, bias_fn):
        ks = pl.multiple_of(kj * t, t)
        kc = k_ref[0, pl.ds(ks, t), :]
        vc = v_ref[0, pl.ds(ks, t), :]
        cst = -slope * jnp.abs((qi - kj) * t).astype(F32)
        for c in range(2):
            sl = slice(c * HEAD_DIM, (c + 1) * HEAD_DIM)
            s = bias_fn(_dot_nt(q_ref[0, :, sl], kc[:, sl]))
            m_old = m_ref[c]
            m_new = jnp.maximum(m_old, jnp.max(s, axis=-1, keepdims=True) + cst)
            p = jnp.exp(s - (m_new - cst))
            alpha = jnp.exp(m_old - m_new)
            l_ref[c] = alpha * l_ref[c] + jnp.sum(p, axis=-1, keepdims=True)
            acc_ref[c] = alpha * acc_ref[c] + jnp.dot(p.astype(BF16), vc, preferred_element_type=F32)
            m_ref[c] = m_new

    def left(kj, carry):
        block(kj, lambda s: s + e_ref[...])
        return carry

    def right(kj, carry):
        block(kj, lambda s: s - e_ref[...])
        return carry

    lax.fori_loop(0, qi, left, 0)
    block(qi, lambda s: s - jnp.abs(e_ref[...]))
    lax.fori_loop(qi + 1, nk, right, 0)

    lam = (jnp.exp(jnp.sum(lam_ref[0:1, :] * lam_ref[1:2, :], axis=-1, keepdims=True))
           - jnp.exp(jnp.sum(lam_ref[2:3, :] * lam_ref[3:4, :], axis=-1, keepdims=True)) + lam_init)
    o = acc_ref[0] / l_ref[0] - lam * (acc_ref[1] / l_ref[1])
    o_ref[0] = (_rms(o) * og_ref[...] * (1.0 - lam_init)).astype(o_ref.dtype)


def diff_attention(proj, lam_params, og, *, lam_init, t):
    b, s, _ = proj.shape
    vw = 2 * HEAD_DIM
    slopes = jnp.asarray(2.0 ** (-8.0 * np.arange(1, N_HEADS + 1) / N_HEADS), F32)
    grid_spec = pltpu.PrefetchScalarGridSpec(
        num_scalar_prefetch=1,
        grid=(b, N_HEADS, s // t),
        in_specs=[
            pl.BlockSpec((1, t, vw), lambda bi, h, qi, sl: (bi, qi, h)),
            pl.BlockSpec((1, s, vw), lambda bi, h, qi, sl: (bi, 0, N_HEADS + h)),
            pl.BlockSpec((1, s, vw), lambda bi, h, qi, sl: (bi, 0, 2 * N_HEADS + h)),
            pl.BlockSpec((4, HEAD_DIM), lambda bi, h, qi, sl: (0, 0)),
            pl.BlockSpec((1, vw), lambda bi, h, qi, sl: (0, 0)),
        ],
        out_specs=pl.BlockSpec((1, t, vw), lambda bi, h, qi, sl: (bi, qi, h)),
        scratch_shapes=[pltpu.VMEM((t, t), F32),
                        pltpu.VMEM((2, t, vw), F32),
                        pltpu.VMEM((2, t, 1), F32),
                        pltpu.VMEM((2, t, 1), F32)],
    )
    return pl.pallas_call(
        functools.partial(_diff_attn_kernel, lam_init=lam_init),
        grid_spec=grid_spec,
        out_shape=jax.ShapeDtypeStruct((b, s, N_HEADS * vw), BF16),
        compiler_params=_cparams(("parallel", "parallel", "arbitrary")),
        name="diff_attention",
    )(slopes, proj, proj, proj, lam_params, og)


def _row(v):
    return v.astype(F32).reshape(1, -1)


def _lane_bcast(v):
    return jnp.broadcast_to(v.astype(F32)[:, None], (v.shape[0], LANE))


def kernel(x, mem, norm_mix_g, norm_xattn_g, norm_mem_g, norm_ffn_g, w_in_ab, ret_decay_fwd, ret_decay_bwd, ret_out_g, na_q_g, na_k_g, na_rpb, w_out_ab, w_in_c, diff_q_g, diff_k_g, lambda_q1, lambda_k1, lambda_q2, lambda_k2, diff_out_g, w_out_c, w_xq, w_xkv, w_xo, xq_g, xk_g, w_ffn_in, w_ffn_out):
    b, s, d = x.shape
    n_mem = mem.shape[1]
    depth = norm_mix_g.shape[0]
    t = b * s
    scale = HEAD_DIM ** -0.5
    width = N_HEADS * HEAD_DIM
    xw = N_X_HEADS * HEAD_DIM
    tm, tn = 1024, 512

    xf = x.reshape(t, d)
    memf = mem.reshape(b * n_mem, d)
    ones_cols = lambda n: jnp.ones((1, n), F32)

    for i in range(depth):
        j = i // 2
        if i % 2 == 0:
            w_in = w_in_ab[j].astype(BF16)
            gcol = jnp.concatenate([
                ones_cols(4 * width),
                jnp.tile(_row(na_q_g[j]) * scale, (1, N_HEADS)),
                jnp.tile(_row(na_k_g[j]), (1, N_HEADS)),
                ones_cols(width)], axis=1)
            proj = norm_matmul(xf, _row(norm_mix_g[i]), w_in, gcol,
                               4 * width // tn, 6 * width // tn, tm=tm, tn=tn)
            proj = proj.reshape(b, s, 7 * width)
            lgf = _lane_bcast(jax.nn.log_sigmoid(ret_decay_fwd[j].astype(F32)))
            lgb = _lane_bcast(jax.nn.log_sigmoid(ret_decay_bwd[j].astype(F32)))
            sb = ret_states(proj, lgb, chunk=RET_CHUNK)
            ret = ret_out(proj, sb, lgf, lgb, _row(ret_out_g[j]), chunk=RET_CHUNK)
            na = neighborhood_attention(proj, na_bias_slab(na_rpb[j], s // GRID_W))
            mixed = jnp.concatenate([ret, na], axis=-1).reshape(t, 2 * width)
            xf = matmul_res(mixed, w_out_ab[j].astype(BF16), xf, tm=tm, tn=tn)
        else:
            w_in = w_in_c[j].astype(BF16)
            cw = 2 * width
            gcol = jnp.concatenate([
                jnp.tile(_row(diff_q_g[j]) * scale, (1, 2 * N_HEADS)),
                jnp.tile(_row(diff_k_g[j]), (1, 2 * N_HEADS)),
                ones_cols(cw)], axis=1)
            proj = norm_matmul(xf, _row(norm_mix_g[i]), w_in, gcol, 0, 2 * cw // tn, tm=tm, tn=tn)
            proj = proj.reshape(b, s, 3 * cw)
            lam_params = jnp.stack([lambda_q1[j], lambda_k1[j], lambda_q2[j], lambda_k2[j]]).astype(F32)
            lam_init = 0.8 - 0.6 * math.exp(-0.3 * i)
            att = diff_attention(proj, lam_params, _row(diff_out_g[j]), lam_init=lam_init, t=512)
            xf = matmul_res(att.reshape(t, cw), w_out_c[j].astype(BF16), xf, tm=tm, tn=tn)

        kv_gcol = jnp.concatenate([jnp.tile(_row(xk_g[i]), (1, N_X_HEADS)), ones_cols(xw)], axis=1)
        kv = norm_matmul(memf, _row(norm_mem_g[i]), w_xkv[i].astype(BF16), kv_gcol, 0, 1,
                         tm=b * n_mem, tn=xw)
        gq = jnp.tile(_row(xq_g[i]) * scale, (1, N_X_HEADS))
        xf = xattn(xf, _row(norm_xattn_g[i]), w_xq[i].astype(BF16), gq, kv, w_xo[i].astype(BF16),
                   tm=512, seq=s)

        hmid = swiglu_in(xf, _row(norm_ffn_g[i]), w_ffn_in[i].astype(BF16), tm=tm, tn=tn)
        xf = matmul_res(hmid, w_ffn_out[i].astype(BF16), xf, tm=tm, tn=tn)

    return xf.reshape(b, s, d)
```

```python
import functools
import math

import jax
import jax.numpy as jnp
import numpy as np
from jax import lax
from jax.experimental import pallas as pl
from jax.experimental.pallas import tpu as pltpu

F32 = jnp.float32
BF16 = jnp.bfloat16

HEAD_DIM = 128
N_HEADS = 8
N_X_HEADS = 4
GRID_W = 64
NA_WIN_ROWS = 8
NA_WIN_COLS = 16
NA_QROWS = 4
RET_CHUNK = 256
RMS_EPS = 1e-6
NEG_BIG = -1e30

LANE = 128
VMEM_LIMIT = 56 * 1024 * 1024


def _cparams(sem):
    return pltpu.CompilerParams(dimension_semantics=sem, vmem_limit_bytes=VMEM_LIMIT)


def _dot_nt(a, b):
    return lax.dot_general(a, b, (((1,), (1,)), ((), ())), preferred_element_type=F32)


def _dot_tn(a, b):
    return lax.dot_general(a, b, (((0,), (0,)), ((), ())), preferred_element_type=F32)


def _rms(x):
    return x * lax.rsqrt(jnp.mean(x * x, axis=-1, keepdims=True) + RMS_EPS)


def _norm_rows_to_scratch(x_ref, g_ref, xn_ref, rows):
    n = x_ref.shape[0] // rows

    def body(r, c):
        sl = pl.ds(pl.multiple_of(r * rows, rows), rows)
        xn_ref[sl, :] = (_rms(x_ref[sl, :]) * g_ref[...]).astype(BF16)
        return c

    lax.fori_loop(0, n, body, 0)


def _norm_matmul_kernel(x_ref, g_ref, w_ref, gc_ref, o_ref, xn_ref, *, lo, hi):
    j = pl.program_id(1)

    @pl.when(j == 0)
    def _():
        _norm_rows_to_scratch(x_ref, g_ref, xn_ref, 128)

    acc = jnp.dot(xn_ref[...], w_ref[...], preferred_element_type=F32)
    if lo == hi:
        o_ref[...] = acc.astype(o_ref.dtype)
        return
    grouped = jnp.logical_and(j >= lo, j < hi)

    @pl.when(grouped)
    def _():
        for c in range(acc.shape[1] // HEAD_DIM):
            sl = slice(c * HEAD_DIM, (c + 1) * HEAD_DIM)
            o_ref[:, sl] = (_rms(acc[:, sl]) * gc_ref[:, sl]).astype(o_ref.dtype)

    @pl.when(jnp.logical_not(grouped))
    def _():
        o_ref[...] = acc.astype(o_ref.dtype)


def norm_matmul(x, g, w, gcol, lo, hi, *, tm, tn, out_dtype=BF16):
    t, d = x.shape
    n = w.shape[1]
    return pl.pallas_call(
        functools.partial(_norm_matmul_kernel, lo=lo, hi=hi),
        grid=(t // tm, n // tn),
        in_specs=[
            pl.BlockSpec((tm, d), lambda i, j: (i, 0)),
            pl.BlockSpec((1, d), lambda i, j: (0, 0)),
            pl.BlockSpec((d, tn), lambda i, j: (0, j)),
            pl.BlockSpec((1, tn), lambda i, j: (0, j)),
        ],
        out_specs=pl.BlockSpec((tm, tn), lambda i, j: (i, j)),
        out_shape=jax.ShapeDtypeStruct((t, n), out_dtype),
        scratch_shapes=[pltpu.VMEM((tm, d), BF16)],
        compiler_params=_cparams(("parallel", "arbitrary")),
        name="norm_matmul",
    )(x, g, w, gcol)


def _swiglu_in_kernel(x_ref, g_ref, wg_ref, wu_ref, o_ref, xn_ref):
    @pl.when(pl.program_id(1) == 0)
    def _():
        _norm_rows_to_scratch(x_ref, g_ref, xn_ref, 128)

    xn = xn_ref[...]
    gate = jnp.dot(xn, wg_ref[...], preferred_element_type=F32)
    up = jnp.dot(xn, wu_ref[...], preferred_element_type=F32)
    o_ref[...] = (gate * jax.nn.sigmoid(gate) * up).astype(o_ref.dtype)


def swiglu_in(x, g, w_in, *, tm, tn):
    t, d = x.shape
    hidden = w_in.shape[1] // 2
    nh = hidden // tn
    return pl.pallas_call(
        _swiglu_in_kernel,
        grid=(t // tm, nh),
        in_specs=[
            pl.BlockSpec((tm, d), lambda i, j: (i, 0)),
            pl.BlockSpec((1, d), lambda i, j: (0, 0)),
            pl.BlockSpec((d, tn), lambda i, j: (0, j)),
            pl.BlockSpec((d, tn), lambda i, j: (0, j + nh)),
        ],
        out_specs=pl.BlockSpec((tm, tn), lambda i, j: (i, j)),
        out_shape=jax.ShapeDtypeStruct((t, hidden), BF16),
        scratch_shapes=[pltpu.VMEM((tm, d), BF16)],
        compiler_params=_cparams(("parallel", "arbitrary")),
        name="swiglu_in",
    )(x, g, w_in, w_in)


def _matmul_res_kernel(a_ref, w_ref, r_ref, o_ref):
    o_ref[...] = r_ref[...] + jnp.dot(a_ref[...], w_ref[...], preferred_element_type=F32)


def matmul_res(a, w, res, *, tm, tn):
    t, k = a.shape
    n = w.shape[1]
    return pl.pallas_call(
        _matmul_res_kernel,
        grid=(t // tm, n // tn),
        in_specs=[
            pl.BlockSpec((tm, k), lambda i, j: (i, 0)),
            pl.BlockSpec((k, tn), lambda i, j: (0, j)),
            pl.BlockSpec((tm, tn), lambda i, j: (i, j)),
        ],
        out_specs=pl.BlockSpec((tm, tn), lambda i, j: (i, j)),
        out_shape=jax.ShapeDtypeStruct((t, n), F32),
        compiler_params=_cparams(("parallel", "arbitrary")),
        name="matmul_res",
    )(a, w, res)


def _xattn_kernel(x_ref, g_ref, wq_ref, gq_ref, kv_ref, wo_ref, o_ref):
    x = x_ref[...]
    xn = (_rms(x) * g_ref[...]).astype(BF16)
    q = jnp.dot(xn, wq_ref[...], preferred_element_type=F32)
    width = N_X_HEADS * HEAD_DIM
    outs = []
    for h in range(N_X_HEADS):
        sl = slice(h * HEAD_DIM, (h + 1) * HEAD_DIM)
        qh = (_rms(q[:, sl]) * gq_ref[:, sl]).astype(BF16)
        s = _dot_nt(qh, kv_ref[:, sl])
        p = jnp.exp(s - jnp.max(s, axis=-1, keepdims=True))
        l = jnp.sum(p, axis=-1, keepdims=True)
        vh = kv_ref[:, width + h * HEAD_DIM: width + (h + 1) * HEAD_DIM]
        oh = jnp.dot(p.astype(BF16), vh, preferred_element_type=F32) / l
        outs.append(oh.astype(BF16))
    o = jnp.concatenate(outs, axis=1)
    o_ref[...] = x + jnp.dot(o, wo_ref[...], preferred_element_type=F32)


def xattn(x, g, wq, gq, kv, wo, *, tm, seq):
    t, d = x.shape
    n_mem = kv.shape[0] // (t // seq)
    width = wq.shape[1]
    per_b = seq // tm
    return pl.pallas_call(
        _xattn_kernel,
        grid=(t // tm,),
        in_specs=[
            pl.BlockSpec((tm, d), lambda i: (i, 0)),
            pl.BlockSpec((1, d), lambda i: (0, 0)),
            pl.BlockSpec((d, width), lambda i: (0, 0)),
            pl.BlockSpec((1, width), lambda i: (0, 0)),
            pl.BlockSpec((n_mem, 2 * width), lambda i: (i // per_b, 0)),
            pl.BlockSpec((width, d), lambda i: (0, 0)),
        ],
        out_specs=pl.BlockSpec((tm, d), lambda i: (i, 0)),
        out_shape=jax.ShapeDtypeStruct((t, d), F32),
        compiler_params=_cparams(("parallel",)),
        name="xattn",
    )(x, g, wq, gq, kv, wo)


def _ret_state_kernel(k_ref, v_ref, lgb_ref, sb_ref, state_ref, kd_ref):
    c = pl.program_id(1)
    chunk = k_ref.shape[1]
    scale = HEAD_DIM ** -0.5

    @pl.when(c == 0)
    def _():
        state_ref[...] = jnp.zeros_like(state_ref)
        pos = lax.broadcasted_iota(jnp.int32, (chunk, HEAD_DIM), 0).astype(F32)
        for h in range(N_HEADS):
            kd_ref[h] = jnp.exp(lgb_ref[h:h + 1, :] * pos) * scale

    for h in range(N_HEADS):
        sl = slice(h * HEAD_DIM, (h + 1) * HEAD_DIM)
        st = state_ref[h]
        sb_ref[0, 0, h] = st.astype(BF16)
        kd = (k_ref[0, :, sl].astype(F32) * kd_ref[h]).astype(BF16)
        cdec = jnp.exp(lgb_ref[h:h + 1, :] * float(chunk))
        state_ref[h] = st * cdec + _dot_tn(kd, v_ref[0, :, sl])


def ret_states(proj, lgb, *, chunk):
    b, s, _ = proj.shape
    nc = s // chunk
    width = N_HEADS * HEAD_DIM
    return pl.pallas_call(
        _ret_state_kernel,
        grid=(b, nc),
        in_specs=[
            pl.BlockSpec((1, chunk, width), lambda bi, c: (bi, nc - 1 - c, 1)),
            pl.BlockSpec((1, chunk, width), lambda bi, c: (bi, nc - 1 - c, 2)),
            pl.BlockSpec((N_HEADS, LANE), lambda bi, c: (0, 0)),
        ],
        out_specs=pl.BlockSpec((1, 1, N_HEADS, HEAD_DIM, HEAD_DIM), lambda bi, c: (bi, nc - 1 - c, 0, 0, 0)),
        out_shape=jax.ShapeDtypeStruct((b, nc, N_HEADS, HEAD_DIM, HEAD_DIM), BF16),
        scratch_shapes=[pltpu.VMEM((N_HEADS, HEAD_DIM, HEAD_DIM), F32),
                        pltpu.VMEM((N_HEADS, chunk, HEAD_DIM), F32)],
        compiler_params=_cparams(("parallel", "arbitrary")),
        name="ret_states",
    )(proj, proj, lgb)


def _ret_out_kernel(q_ref, k_ref, v_ref, rg_ref, sb_ref, lgf_ref, lgb_ref, og_ref, o_ref,
                    state_ref, dmat_ref, qdf_ref, qdb_ref, kdf_ref):
    c = pl.program_id(1)
    chunk = q_ref.shape[1]
    scale = HEAD_DIM ** -0.5

    @pl.when(c == 0)
    def _():
        state_ref[...] = jnp.zeros_like(state_ref)
        pos = lax.broadcasted_iota(jnp.int32, (chunk, HEAD_DIM), 0).astype(F32)
        ri = lax.broadcasted_iota(jnp.int32, (chunk, chunk), 0)
        ci = lax.broadcasted_iota(jnp.int32, (chunk, chunk), 1)
        diff = (ri - ci).astype(F32)
        for h in range(N_HEADS):
            lf = lgf_ref[h:h + 1, :]
            lb = lgb_ref[h:h + 1, :]
            dfwd = jnp.exp(lf[:, :1] * jnp.maximum(diff, 0.0))
            dbwd = jnp.exp(lb[:, :1] * jnp.maximum(-diff, 0.0))
            dmat_ref[h] = jnp.where(diff >= 0.0, dfwd, dbwd) * scale
            qdf_ref[h] = jnp.exp(lf * (pos + 1.0))
            qdb_ref[h] = jnp.exp(lb * (float(chunk) - pos))
            kdf_ref[h] = jnp.exp(lf * (float(chunk) - 1.0 - pos)) * scale

    for h in range(N_HEADS):
        sl = slice(h * HEAD_DIM, (h + 1) * HEAD_DIM)
        qh = q_ref[0, :, sl]
        kh = k_ref[0, :, sl]
        vh = v_ref[0, :, sl]
        qf32 = qh.astype(F32)
        sd = (_dot_nt(qh, kh) * dmat_ref[h]).astype(BF16)
        st = state_ref[h]
        out = jnp.dot(sd, vh, preferred_element_type=F32)
        out += jnp.dot((qf32 * qdf_ref[h]).astype(BF16), st.astype(BF16), preferred_element_type=F32)
        out += jnp.dot((qf32 * qdb_ref[h]).astype(BF16), sb_ref[0, 0, h], preferred_element_type=F32)
        gate = rg_ref[0, :, sl].astype(F32)
        y = _rms(out) * og_ref[...]
        o_ref[0, :, sl] = (y * (gate * jax.nn.sigmoid(gate))).astype(o_ref.dtype)
        kd = (kh.astype(F32) * kdf_ref[h]).astype(BF16)
        cdec = jnp.exp(lgf_ref[h:h + 1, :] * float(chunk))
        state_ref[h] = st * cdec + _dot_tn(kd, vh)


def ret_out(proj, sb, lgf, lgb, og, *, chunk):
    b, s, _ = proj.shape
    nc = s // chunk
    width = N_HEADS * HEAD_DIM
    col = lambda k: pl.BlockSpec((1, chunk, width), lambda bi, c, k=k: (bi, c, k))
    return pl.pallas_call(
        _ret_out_kernel,
        grid=(b, nc),
        in_specs=[
            col(0), col(1), col(2), col(3),
            pl.BlockSpec((1, 1, N_HEADS, HEAD_DIM, HEAD_DIM), lambda bi, c: (bi, c, 0, 0, 0)),
            pl.BlockSpec((N_HEADS, LANE), lambda bi, c: (0, 0)),
            pl.BlockSpec((N_HEADS, LANE), lambda bi, c: (0, 0)),
            pl.BlockSpec((1, HEAD_DIM), lambda bi, c: (0, 0)),
        ],
        out_specs=pl.BlockSpec((1, chunk, width), lambda bi, c: (bi, c, 0)),
        out_shape=jax.ShapeDtypeStruct((b, s, width), BF16),
        scratch_shapes=[pltpu.VMEM((N_HEADS, HEAD_DIM, HEAD_DIM), F32),
                        pltpu.VMEM((N_HEADS, chunk, chunk), F32),
                        pltpu.VMEM((N_HEADS, chunk, HEAD_DIM), F32),
                        pltpu.VMEM((N_HEADS, chunk, HEAD_DIM), F32),
                        pltpu.VMEM((N_HEADS, chunk, HEAD_DIM), F32)],
        compiler_params=_cparams(("parallel", "arbitrary")),
        name="ret_out",
    )(proj, proj, proj, proj, sb, lgf, lgb, og)


N_DR = 2 * NA_WIN_ROWS - 1
N_DC = 2 * NA_WIN_COLS - 1


def _na_kernel(rpb_ref, q_ref, kp_ref, kc_ref, kn_ref, vp_ref, vc_ref, vn_ref, o_ref,
               bcol_ref, slab_ref, *, rows):
    qb = pl.program_id(1)
    nb = rows // NA_QROWS
    neg_tile = jnp.full((GRID_W, GRID_W), NEG_BIG, F32)

    @pl.when(qb == 0)
    def _():
        qc = lax.broadcasted_iota(jnp.int32, (GRID_W, GRID_W), 0)
        kc = lax.broadcasted_iota(jnp.int32, (GRID_W, GRID_W), 1)
        cs = jnp.clip(qc - NA_WIN_COLS // 2, 0, GRID_W - NA_WIN_COLS)
        col_ok = jnp.logical_and(kc >= cs, kc < cs + NA_WIN_COLS)
        dc = kc - qc + (NA_WIN_COLS - 1)
        for h in range(N_HEADS):
            for dr in range(N_DR):
                tile = neg_tile
                for d in range(N_DC):
                    tile = jnp.where(dc == d, rpb_ref[(h * N_DR + dr) * N_DC + d], tile)
                bcol_ref[h, dr] = jnp.where(col_ok, tile, NEG_BIG)

    def assemble(qb_static):
        for h in range(N_HEADS):
            for j in range(NA_QROWS):
                r = NA_QROWS * qb_static + j
                rs = min(max(r - NA_WIN_ROWS // 2, 0), rows - NA_WIN_ROWS)
                for i2 in range(0, 3 * NA_QROWS, 2):
                    halves = []
                    for i in (i2, i2 + 1):
                        kr = NA_QROWS * (qb_static - 1) + i
                        ok = rs <= kr < rs + NA_WIN_ROWS
                        halves.append(bcol_ref[h, kr - r + NA_WIN_ROWS - 1] if ok else neg_tile)
                    slab_ref[h, j * GRID_W:(j + 1) * GRID_W, i2 * GRID_W:(i2 + 2) * GRID_W] = (
                        jnp.concatenate(halves, axis=1))

    for qb_static in (0, 1, nb - 1):
        pl.when(qb == qb_static)(functools.partial(assemble, qb_static))

    for h in range(N_HEADS):
        sl = slice(h * HEAD_DIM, (h + 1) * HEAD_DIM)
        k = jnp.concatenate([kp_ref[0, :, sl], kc_ref[0, :, sl], kn_ref[0, :, sl]], axis=0)
        v = jnp.concatenate([vp_ref[0, :, sl], vc_ref[0, :, sl], vn_ref[0, :, sl]], axis=0)
        s = _dot_nt(q_ref[0, :, sl], k) + slab_ref[h]
        p = jnp.exp(s - jnp.max(s, axis=-1, keepdims=True))
        l = jnp.sum(p, axis=-1, keepdims=True)
        o = jnp.dot(p.astype(BF16), v, preferred_element_type=F32) / l
        o_ref[0, :, sl] = o.astype(o_ref.dtype)


def neighborhood_attention(proj, rpb):
    b, s, _ = proj.shape
    rows = s // GRID_W
    assert NA_QROWS >= NA_WIN_ROWS // 2 and NA_QROWS % 2 == 0 and rows % NA_QROWS == 0
    assert rows >= 3 * NA_QROWS and rows >= NA_WIN_ROWS
    nq = NA_QROWS * GRID_W
    nb = rows // NA_QROWS
    width = N_HEADS * HEAD_DIM

    def blk(group, shift):
        return pl.BlockSpec((1, nq, width),
                            lambda bi, qb, rp: (bi, jnp.clip(qb + shift, 0, nb - 1), group))

    grid_spec = pltpu.PrefetchScalarGridSpec(
        num_scalar_prefetch=1,
        grid=(b, nb),
        in_specs=[blk(4, 0), blk(5, -1), blk(5, 0), blk(5, 1), blk(6, -1), blk(6, 0), blk(6, 1)],
        out_specs=pl.BlockSpec((1, nq, width), lambda bi, qb, rp: (bi, qb, 0)),
        scratch_shapes=[pltpu.VMEM((N_HEADS, N_DR, GRID_W, GRID_W), F32),
                        pltpu.VMEM((N_HEADS, nq, 3 * nq), F32)],
    )
    return pl.pallas_call(
        functools.partial(_na_kernel, rows=rows),
        grid_spec=grid_spec,
        out_shape=jax.ShapeDtypeStruct((b, s, width), BF16),
        compiler_params=_cparams(("arbitrary", "arbitrary")),
        name="neighborhood_attention",
    )(rpb.astype(F32).reshape(-1), proj, proj, proj, proj, proj, proj, proj)


def _bf16_pieces(x, n):
    out, rest = [], float(x)
    for _ in range(n):
        piece = float(np.float32(rest).astype(BF16).astype(np.float32))
        out.append(piece)
        rest -= piece
    return out


LOG2E = math.log2(math.e)
LOG2E_PIECES = _bf16_pieces(LOG2E, 3)
N_ALIBI_COLS = 12
POS_LO_BITS = 7


def _alibi_cols(pos, slope, lane, unit_cols, sign):
    hi = (pos >> POS_LO_BITS).astype(F32) * (slope * float(1 << POS_LO_BITS))
    lo = (pos & ((1 << POS_LO_BITS) - 1)).astype(F32) * slope
    if sign > 0:
        return jnp.where(lane < 3, hi, jnp.where(lane < 6, lo, unit_cols))
    return jnp.where(lane < 6, unit_cols, jnp.where(lane < 9, hi, jnp.where(lane < N_ALIBI_COLS, lo, 0.0)))


def _diff_attn_kernel(slopes_ref, q_ref, k_ref, v_ref, cq_ref, ck_ref, lam_ref, og_ref, o_ref,
                      kaug_ref, qaug_ref, s_ref, bias_ref, mrun_ref, acc_ref, *, lam_init, nsub):
    h = pl.program_id(1)
    qi = pl.program_id(2)
    t = q_ref.shape[1]
    nk = k_ref.shape[1] // t
    nslab = t // LANE
    slope = slopes_ref[h]
    lane = lax.broadcasted_iota(jnp.int32, (t, LANE), 1)
    row = lax.broadcasted_iota(jnp.int32, (t, LANE), 0)

    @pl.when(qi == 0)
    def _():
        def body(r, carry):
            rs = pl.multiple_of(r * t, t)
            kx = _alibi_cols(row + rs, slope, lane, ck_ref[...], 1).astype(BF16)
            for c in range(2):
                kaug_ref[c, pl.ds(rs, t), 0:HEAD_DIM] = k_ref[0, pl.ds(rs, t), c * HEAD_DIM:(c + 1) * HEAD_DIM]
                kaug_ref[c, pl.ds(rs, t), HEAD_DIM:2 * HEAD_DIM] = kx
            return carry

        lax.fori_loop(0, nk, body, 0)
        ri = lax.broadcasted_iota(jnp.int32, (t, t), 0)
        ci = lax.broadcasted_iota(jnp.int32, (t, t), 1)
        bias_ref[0] = jnp.zeros((t, t), F32)
        bias_ref[1] = jnp.abs(ri - ci).astype(F32) * (-LOG2E * slope)

    qx = _alibi_cols(row + qi * t, slope, lane, cq_ref[...], -1)
    for c in range(2):
        qc = q_ref[0, :, c * HEAD_DIM:(c + 1) * HEAD_DIM]
        for side, ext in ((0, qx), (1, -qx), (2, jnp.zeros_like(qx))):
            qaug_ref[c, side, :, 0:HEAD_DIM] = qc
            qaug_ref[c, side, :, HEAD_DIM:2 * HEAD_DIM] = ext.astype(BF16)

    outs = []
    for c in range(2):
        mrun_ref[...] = jnp.full_like(mrun_ref, NEG_BIG)

        def scores(g, carry):
            m = mrun_ref[...]
            for i in range(nsub):
                kb = g * nsub + i
                side = jnp.where(kb < qi, 0, jnp.where(kb > qi, 1, 2))
                ks = pl.multiple_of(kb * t, t)
                s = _dot_nt(qaug_ref[c, side], kaug_ref[c, pl.ds(ks, t), :])
                s = s + bias_ref[(kb == qi).astype(jnp.int32)]
                s_ref[kb] = s
                for u in range(nslab):
                    m = jnp.maximum(m, s[:, u * LANE:(u + 1) * LANE])
            mrun_ref[...] = m
            return carry

        lax.fori_loop(0, nk // nsub, scores, 0)

        mb = jnp.broadcast_to(jnp.max(mrun_ref[...], axis=-1, keepdims=True), (t, LANE))
        acc_ref[...] = jnp.zeros_like(acc_ref)

        def pv(bb, lrun):
            ps = []
            for i in range(nsub):
                for u in range(nslab):
                    p = jnp.exp2(s_ref[bb * nsub + i, :, u * LANE:(u + 1) * LANE] - mb)
                    lrun = lrun + p
                    ps.append(p.astype(BF16))
            width = nsub * t
            vs = v_ref[0, pl.ds(pl.multiple_of(bb * width, width), width), :]
            acc_ref[...] += jnp.dot(jnp.concatenate(ps, axis=1), vs, preferred_element_type=F32)
            return lrun

        lrun = lax.fori_loop(0, nk // nsub, pv, jnp.zeros((t, LANE), F32))
        outs.append(acc_ref[...] / jnp.sum(lrun, axis=-1, keepdims=True))

    lam = (jnp.exp(jnp.sum(lam_ref[0:1, :] * lam_ref[1:2, :], axis=-1, keepdims=True))
           - jnp.exp(jnp.sum(lam_ref[2:3, :] * lam_ref[3:4, :], axis=-1, keepdims=True)) + lam_init)
    o = outs[0] - lam * outs[1]
    o_ref[0] = (_rms(o) * og_ref[...] * (1.0 - lam_init)).astype(o_ref.dtype)


def diff_attention(proj, lam_params, og, *, lam_init, t):
    b, s, _ = proj.shape
    assert s % t == 0 and s <= (1 << (2 * POS_LO_BITS + 1)) and t % LANE == 0
    nk = s // t
    nsub = max(n for n in (4, 3, 2, 1) if nk % n == 0)
    vw = 2 * HEAD_DIM
    slopes = jnp.asarray(2.0 ** (-8.0 * np.arange(1, N_HEADS + 1) / N_HEADS), F32)
    pieces = np.zeros((1, LANE), np.float32)
    pieces[0, :6] = LOG2E_PIECES * 2
    cq = jnp.asarray(pieces)
    ck = jnp.asarray(-np.roll(pieces, 6, axis=1))
    const = lambda shape: pl.BlockSpec(shape, lambda bi, h, qi, sl: (0, 0))
    grid_spec = pltpu.PrefetchScalarGridSpec(
        num_scalar_prefetch=1,
        grid=(b, N_HEADS, nk),
        in_specs=[
            pl.BlockSpec((1, t, vw), lambda bi, h, qi, sl: (bi, qi, h)),
            pl.BlockSpec((1, s, vw), lambda bi, h, qi, sl: (bi, 0, N_HEADS + h), pipeline_mode=pl.Buffered(1)),
            pl.BlockSpec((1, s, vw), lambda bi, h, qi, sl: (bi, 0, 2 * N_HEADS + h), pipeline_mode=pl.Buffered(1)),
            const((1, LANE)), const((1, LANE)), const((4, HEAD_DIM)), const((1, vw)),
        ],
        out_specs=pl.BlockSpec((1, t, vw), lambda bi, h, qi, sl: (bi, qi, h)),
        scratch_shapes=[pltpu.VMEM((2, s, vw), BF16),
                        pltpu.VMEM((2, 3, t, vw), BF16),
                        pltpu.VMEM((nk, t, t), F32),
                        pltpu.VMEM((2, t, t), F32),
                        pltpu.VMEM((t, LANE), F32),
                        pltpu.VMEM((t, vw), F32)],
    )
    return pl.pallas_call(
        functools.partial(_diff_attn_kernel, lam_init=lam_init, nsub=nsub),
        grid_spec=grid_spec,
        out_shape=jax.ShapeDtypeStruct((b, s, N_HEADS * vw), BF16),
        compiler_params=_cparams(("arbitrary", "arbitrary", "arbitrary")),
        name="diff_attention",
    )(slopes, proj, proj, proj, cq, ck, lam_params, og)


def _row(v):
    return v.astype(F32).reshape(1, -1)


def _lane_bcast(v):
    return jnp.broadcast_to(v.astype(F32)[:, None], (v.shape[0], LANE))


def kernel(x, mem, norm_mix_g, norm_xattn_g, norm_mem_g, norm_ffn_g, w_in_ab, ret_decay_fwd, ret_decay_bwd, ret_out_g, na_q_g, na_k_g, na_rpb, w_out_ab, w_in_c, diff_q_g, diff_k_g, lambda_q1, lambda_k1, lambda_q2, lambda_k2, diff_out_g, w_out_c, w_xq, w_xkv, w_xo, xq_g, xk_g, w_ffn_in, w_ffn_out):
    b, s, d = x.shape
    n_mem = mem.shape[1]
    depth = norm_mix_g.shape[0]
    t = b * s
    scale = HEAD_DIM ** -0.5
    width = N_HEADS * HEAD_DIM
    xw = N_X_HEADS * HEAD_DIM
    tm, tn = 1024, 512

    xf = x.reshape(t, d)
    memf = mem.reshape(b * n_mem, d)
    ones_cols = lambda n: jnp.ones((1, n), F32)

    for i in range(depth):
        j = i // 2
        if i % 2 == 0:
            w_in = w_in_ab[j].astype(BF16)
            gcol = jnp.concatenate([
                ones_cols(4 * width),
                jnp.tile(_row(na_q_g[j]) * scale, (1, N_HEADS)),
                jnp.tile(_row(na_k_g[j]), (1, N_HEADS)),
                ones_cols(width)], axis=1)
            proj = norm_matmul(xf, _row(norm_mix_g[i]), w_in, gcol,
                               4 * width // tn, 6 * width // tn, tm=tm, tn=tn)
            proj = proj.reshape(b, s, 7 * width)
            lgf = _lane_bcast(jax.nn.log_sigmoid(ret_decay_fwd[j].astype(F32)))
            lgb = _lane_bcast(jax.nn.log_sigmoid(ret_decay_bwd[j].astype(F32)))
            sb = ret_states(proj, lgb, chunk=RET_CHUNK)
            ret = ret_out(proj, sb, lgf, lgb, _row(ret_out_g[j]), chunk=RET_CHUNK)
            na = neighborhood_attention(proj, na_rpb[j])
            mixed = jnp.concatenate([ret, na], axis=-1).reshape(t, 2 * width)
            xf = matmul_res(mixed, w_out_ab[j].astype(BF16), xf, tm=tm, tn=tn)
        else:
            w_in = w_in_c[j].astype(BF16)
            cw = 2 * width
            gcol = jnp.concatenate([
                jnp.tile(_row(diff_q_g[j]) * (scale * LOG2E), (1, 2 * N_HEADS)),
                jnp.tile(_row(diff_k_g[j]), (1, 2 * N_HEADS)),
                ones_cols(cw)], axis=1)
            proj = norm_matmul(xf, _row(norm_mix_g[i]), w_in, gcol, 0, 2 * cw // tn, tm=tm, tn=tn)
            proj = proj.reshape(b, s, 3 * cw)
            lam_params = jnp.stack([lambda_q1[j], lambda_k1[j], lambda_q2[j], lambda_k2[j]]).astype(F32)
            lam_init = 0.8 - 0.6 * math.exp(-0.3 * i)
            att = diff_attention(proj, lam_params, _row(diff_out_g[j]), lam_init=lam_init, t=512)
            xf = matmul_res(att.reshape(t, cw), w_out_c[j].astype(BF16), xf, tm=tm, tn=tn)

        kv_gcol = jnp.concatenate([jnp.tile(_row(xk_g[i]), (1, N_X_HEADS)), ones_cols(xw)], axis=1)
        kv = norm_matmul(memf, _row(norm_mem_g[i]), w_xkv[i].astype(BF16), kv_gcol, 0, 1,
                         tm=b * n_mem, tn=xw)
        gq = jnp.tile(_row(xq_g[i]) * scale, (1, N_X_HEADS))
        xf = xattn(xf, _row(norm_xattn_g[i]), w_xq[i].astype(BF16), gq, kv, w_xo[i].astype(BF16),
                   tm=512, seq=s)

        hmid = swiglu_in(xf, _row(norm_ffn_g[i]), w_ffn_in[i].astype(BF16), tm=tm, tn=tn)
        xf = matmul_res(hmid, w_ffn_out[i].astype(BF16), xf, tm=tm, tn=tn)

    return xf.reshape(b, s, d)
```

```python
import functools
import math

import jax
import jax.numpy as jnp
import numpy as np
from jax import lax
from jax.experimental import pallas as pl
from jax.experimental.pallas import tpu as pltpu

F32 = jnp.float32
BF16 = jnp.bfloat16

HEAD_DIM = 128
N_HEADS = 8
N_X_HEADS = 4
GRID_W = 64
NA_WIN_ROWS = 8
NA_WIN_COLS = 16
NA_QROWS = 4
RET_CHUNK = 256
RMS_EPS = 1e-6
NEG_BIG = -1e30

LANE = 128
VMEM_LIMIT = 56 * 1024 * 1024


def _cparams(sem):
    return pltpu.CompilerParams(dimension_semantics=sem, vmem_limit_bytes=VMEM_LIMIT)


def _dot_nt(a, b):
    return lax.dot_general(a, b, (((1,), (1,)), ((), ())), preferred_element_type=F32)


def _dot_tn(a, b):
    return lax.dot_general(a, b, (((0,), (0,)), ((), ())), preferred_element_type=F32)


def _rms(x):
    return x * lax.rsqrt(jnp.mean(x * x, axis=-1, keepdims=True) + RMS_EPS)


def _norm_rows_to_scratch(x_ref, g_ref, xn_ref, rows):
    n = x_ref.shape[0] // rows

    def body(r, c):
        sl = pl.ds(pl.multiple_of(r * rows, rows), rows)
        xn_ref[sl, :] = (_rms(x_ref[sl, :]) * g_ref[...]).astype(BF16)
        return c

    lax.fori_loop(0, n, body, 0)


def _norm_matmul_kernel(x_ref, g_ref, w_ref, gc_ref, o_ref, xn_ref, *, lo, hi):
    j = pl.program_id(1)

    @pl.when(j == 0)
    def _():
        _norm_rows_to_scratch(x_ref, g_ref, xn_ref, 128)

    acc = jnp.dot(xn_ref[...], w_ref[...], preferred_element_type=F32)
    grouped = jnp.logical_and(j >= lo, j < hi)
    for c in range(acc.shape[1] // HEAD_DIM):
        sl = slice(c * HEAD_DIM, (c + 1) * HEAD_DIM)
        a = acc[:, sl]
        inv = lax.rsqrt(jnp.mean(a * a, axis=-1, keepdims=True) + RMS_EPS)
        o_ref[:, sl] = ((a * jnp.where(grouped, inv, 1.0)) * gc_ref[:, sl]).astype(o_ref.dtype)


def norm_matmul(x, g, w, gcol, lo, hi, *, tm, tn, out_dtype=BF16):
    t, d = x.shape
    n = w.shape[1]
    return pl.pallas_call(
        functools.partial(_norm_matmul_kernel, lo=lo, hi=hi),
        grid=(t // tm, n // tn),
        in_specs=[
            pl.BlockSpec((tm, d), lambda i, j: (i, 0)),
            pl.BlockSpec((1, d), lambda i, j: (0, 0)),
            pl.BlockSpec((d, tn), lambda i, j: (0, j)),
            pl.BlockSpec((1, tn), lambda i, j: (0, j)),
        ],
        out_specs=pl.BlockSpec((tm, tn), lambda i, j: (i, j)),
        out_shape=jax.ShapeDtypeStruct((t, n), out_dtype),
        scratch_shapes=[pltpu.VMEM((tm, d), BF16)],
        compiler_params=_cparams(("parallel", "arbitrary")),
        name="norm_matmul",
    )(x, g, w, gcol)


def _swiglu_in_kernel(x_ref, g_ref, wg_ref, wu_ref, o_ref, xn_ref):
    @pl.when(pl.program_id(1) == 0)
    def _():
        _norm_rows_to_scratch(x_ref, g_ref, xn_ref, 128)

    xn = xn_ref[...]
    gate = jnp.dot(xn, wg_ref[...], preferred_element_type=F32)
    up = jnp.dot(xn, wu_ref[...], preferred_element_type=F32)
    o_ref[...] = (gate * jax.nn.sigmoid(gate) * up).astype(o_ref.dtype)


def swiglu_in(x, g, w_in, *, tm, tn):
    t, d = x.shape
    hidden = w_in.shape[1] // 2
    nh = hidden // tn
    return pl.pallas_call(
        _swiglu_in_kernel,
        grid=(t // tm, nh),
        in_specs=[
            pl.BlockSpec((tm, d), lambda i, j: (i, 0)),
            pl.BlockSpec((1, d), lambda i, j: (0, 0)),
            pl.BlockSpec((d, tn), lambda i, j: (0, j)),
            pl.BlockSpec((d, tn), lambda i, j: (0, j + nh)),
        ],
        out_specs=pl.BlockSpec((tm, tn), lambda i, j: (i, j)),
        out_shape=jax.ShapeDtypeStruct((t, hidden), BF16),
        scratch_shapes=[pltpu.VMEM((tm, d), BF16)],
        compiler_params=_cparams(("parallel", "arbitrary")),
        name="swiglu_in",
    )(x, g, w_in, w_in)


def _matmul_res_kernel(*refs):
    *a_refs, w_ref, r_ref, o_ref = refs
    acc = r_ref[...]
    k0 = 0
    for a_ref in a_refs:
        k1 = k0 + a_ref.shape[1]
        acc += jnp.dot(a_ref[...], w_ref[k0:k1, :], preferred_element_type=F32)
        k0 = k1
    o_ref[...] = acc


def matmul_res(a_parts, w, res, *, tm, tn):
    t = res.shape[0]
    k, n = w.shape
    assert sum(a.shape[1] for a in a_parts) == k
    return pl.pallas_call(
        _matmul_res_kernel,
        grid=(t // tm, n // tn),
        in_specs=[pl.BlockSpec((tm, a.shape[1]), lambda i, j: (i, 0)) for a in a_parts] + [
            pl.BlockSpec((k, tn), lambda i, j: (0, j)),
            pl.BlockSpec((tm, tn), lambda i, j: (i, j)),
        ],
        out_specs=pl.BlockSpec((tm, tn), lambda i, j: (i, j)),
        out_shape=jax.ShapeDtypeStruct((t, n), F32),
        compiler_params=_cparams(("parallel", "arbitrary")),
        name="matmul_res",
    )(*a_parts, w, res)


def _xattn_kernel(x_ref, g_ref, wq_ref, gq_ref, kv_ref, wo_ref, o_ref):
    x = x_ref[...]
    xn = (_rms(x) * g_ref[...]).astype(BF16)
    q = jnp.dot(xn, wq_ref[...], preferred_element_type=F32)
    width = N_X_HEADS * HEAD_DIM
    outs = []
    for h in range(N_X_HEADS):
        sl = slice(h * HEAD_DIM, (h + 1) * HEAD_DIM)
        qh = (_rms(q[:, sl]) * gq_ref[:, sl]).astype(BF16)
        s = _dot_nt(qh, kv_ref[:, sl])
        p = jnp.exp(s - jnp.max(s, axis=-1, keepdims=True))
        l = jnp.sum(p, axis=-1, keepdims=True)
        vh = kv_ref[:, width + h * HEAD_DIM: width + (h + 1) * HEAD_DIM]
        oh = jnp.dot(p.astype(BF16), vh, preferred_element_type=F32) / l
        outs.append(oh.astype(BF16))
    o = jnp.concatenate(outs, axis=1)
    o_ref[...] = x + jnp.dot(o, wo_ref[...], preferred_element_type=F32)


def xattn(x, g, wq, gq, kv, wo, *, tm, seq):
    t, d = x.shape
    n_mem = kv.shape[0] // (t // seq)
    width = wq.shape[1]
    per_b = seq // tm
    return pl.pallas_call(
        _xattn_kernel,
        grid=(t // tm,),
        in_specs=[
            pl.BlockSpec((tm, d), lambda i: (i, 0)),
            pl.BlockSpec((1, d), lambda i: (0, 0)),
            pl.BlockSpec((d, width), lambda i: (0, 0)),
            pl.BlockSpec((1, width), lambda i: (0, 0)),
            pl.BlockSpec((n_mem, 2 * width), lambda i: (i // per_b, 0)),
            pl.BlockSpec((width, d), lambda i: (0, 0)),
        ],
        out_specs=pl.BlockSpec((tm, d), lambda i: (i, 0)),
        out_shape=jax.ShapeDtypeStruct((t, d), F32),
        compiler_params=_cparams(("parallel",)),
        name="xattn",
    )(x, g, wq, gq, kv, wo)


def _ret_state_kernel(k_ref, v_ref, lgb_ref, sb_ref, state_ref, kd_ref):
    c = pl.program_id(1)
    chunk = k_ref.shape[1]
    scale = HEAD_DIM ** -0.5

    @pl.when(c == 0)
    def _():
        state_ref[...] = jnp.zeros_like(state_ref)
        pos = lax.broadcasted_iota(jnp.int32, (chunk, HEAD_DIM), 0).astype(F32)
        for h in range(N_HEADS):
            kd_ref[h] = jnp.exp(lgb_ref[h:h + 1, :] * pos) * scale

    for h in range(N_HEADS):
        sl = slice(h * HEAD_DIM, (h + 1) * HEAD_DIM)
        st = state_ref[h]
        sb_ref[0, 0, h] = st.astype(BF16)
        kd = (k_ref[0, :, sl].astype(F32) * kd_ref[h]).astype(BF16)
        cdec = jnp.exp(lgb_ref[h:h + 1, :] * float(chunk))
        state_ref[h] = st * cdec + _dot_tn(kd, v_ref[0, :, sl])


def ret_states(proj, lgb, *, chunk):
    b, s, _ = proj.shape
    nc = s // chunk
    width = N_HEADS * HEAD_DIM
    return pl.pallas_call(
        _ret_state_kernel,
        grid=(b, nc),
        in_specs=[
            pl.BlockSpec((1, chunk, width), lambda bi, c: (bi, nc - 1 - c, 1)),
            pl.BlockSpec((1, chunk, width), lambda bi, c: (bi, nc - 1 - c, 2)),
            pl.BlockSpec((N_HEADS, LANE), lambda bi, c: (0, 0)),
        ],
        out_specs=pl.BlockSpec((1, 1, N_HEADS, HEAD_DIM, HEAD_DIM), lambda bi, c: (bi, nc - 1 - c, 0, 0, 0)),
        out_shape=jax.ShapeDtypeStruct((b, nc, N_HEADS, HEAD_DIM, HEAD_DIM), BF16),
        scratch_shapes=[pltpu.VMEM((N_HEADS, HEAD_DIM, HEAD_DIM), F32),
                        pltpu.VMEM((N_HEADS, chunk, HEAD_DIM), F32)],
        compiler_params=_cparams(("parallel", "arbitrary")),
        name="ret_states",
    )(proj, proj, lgb)


def _ret_out_kernel(q_ref, k_ref, v_ref, rg_ref, sb_ref, lgf_ref, lgb_ref, og_ref, o_ref,
                    state_ref, dmat_ref, qdf_ref, qdb_ref, kdf_ref):
    c = pl.program_id(1)
    chunk = q_ref.shape[1]
    scale = HEAD_DIM ** -0.5

    @pl.when(c == 0)
    def _():
        state_ref[...] = jnp.zeros_like(state_ref)
        pos = lax.broadcasted_iota(jnp.int32, (chunk, HEAD_DIM), 0).astype(F32)
        ri = lax.broadcasted_iota(jnp.int32, (chunk, chunk), 0)
        ci = lax.broadcasted_iota(jnp.int32, (chunk, chunk), 1)
        diff = (ri - ci).astype(F32)
        for h in range(N_HEADS):
            lf = lgf_ref[h:h + 1, :]
            lb = lgb_ref[h:h + 1, :]
            dfwd = jnp.exp(lf[:, :1] * jnp.maximum(diff, 0.0))
            dbwd = jnp.exp(lb[:, :1] * jnp.maximum(-diff, 0.0))
            dmat_ref[h] = jnp.where(diff >= 0.0, dfwd, dbwd) * scale
            qdf_ref[h] = jnp.exp(lf * (pos + 1.0))
            qdb_ref[h] = jnp.exp(lb * (float(chunk) - pos))
            kdf_ref[h] = jnp.exp(lf * (float(chunk) - 1.0 - pos)) * scale

    for h in range(N_HEADS):
        sl = slice(h * HEAD_DIM, (h + 1) * HEAD_DIM)
        qh = q_ref[0, :, sl]
        kh = k_ref[0, :, sl]
        vh = v_ref[0, :, sl]
        qf32 = qh.astype(F32)
        sd = (_dot_nt(qh, kh) * dmat_ref[h]).astype(BF16)
        st = state_ref[h]
        out = jnp.dot(sd, vh, preferred_element_type=F32)
        out += jnp.dot((qf32 * qdf_ref[h]).astype(BF16), st.astype(BF16), preferred_element_type=F32)
        out += jnp.dot((qf32 * qdb_ref[h]).astype(BF16), sb_ref[0, 0, h], preferred_element_type=F32)
        gate = rg_ref[0, :, sl].astype(F32)
        y = _rms(out) * og_ref[...]
        o_ref[0, :, sl] = (y * (gate * jax.nn.sigmoid(gate))).astype(o_ref.dtype)
        kd = (kh.astype(F32) * kdf_ref[h]).astype(BF16)
        cdec = jnp.exp(lgf_ref[h:h + 1, :] * float(chunk))
        state_ref[h] = st * cdec + _dot_tn(kd, vh)


def ret_out(proj, sb, lgf, lgb, og, *, chunk):
    b, s, _ = proj.shape
    nc = s // chunk
    width = N_HEADS * HEAD_DIM
    col = lambda k: pl.BlockSpec((1, chunk, width), lambda bi, c, k=k: (bi, c, k))
    return pl.pallas_call(
        _ret_out_kernel,
        grid=(b, nc),
        in_specs=[
            col(0), col(1), col(2), col(3),
            pl.BlockSpec((1, 1, N_HEADS, HEAD_DIM, HEAD_DIM), lambda bi, c: (bi, c, 0, 0, 0)),
            pl.BlockSpec((N_HEADS, LANE), lambda bi, c: (0, 0)),
            pl.BlockSpec((N_HEADS, LANE), lambda bi, c: (0, 0)),
            pl.BlockSpec((1, HEAD_DIM), lambda bi, c: (0, 0)),
        ],
        out_specs=pl.BlockSpec((1, chunk, width), lambda bi, c: (bi, c, 0)),
        out_shape=jax.ShapeDtypeStruct((b, s, width), BF16),
        scratch_shapes=[pltpu.VMEM((N_HEADS, HEAD_DIM, HEAD_DIM), F32),
                        pltpu.VMEM((N_HEADS, chunk, chunk), F32),
                        pltpu.VMEM((N_HEADS, chunk, HEAD_DIM), F32),
                        pltpu.VMEM((N_HEADS, chunk, HEAD_DIM), F32),
                        pltpu.VMEM((N_HEADS, chunk, HEAD_DIM), F32)],
        compiler_params=_cparams(("parallel", "arbitrary")),
        name="ret_out",
    )(proj, proj, proj, proj, sb, lgf, lgb, og)


N_DR = 2 * NA_WIN_ROWS - 1
N_DC = 2 * NA_WIN_COLS - 1


def _na_kernel(rpb_ref, q_ref, kp_ref, kc_ref, kn_ref, vp_ref, vc_ref, vn_ref, o_ref,
               bcol_ref, slab_ref, *, rows):
    qb = pl.program_id(1)
    nb = rows // NA_QROWS
    neg_tile = jnp.full((GRID_W, GRID_W), NEG_BIG, F32)

    @pl.when(qb == 0)
    def _():
        qc = lax.broadcasted_iota(jnp.int32, (GRID_W, GRID_W), 0)
        kc = lax.broadcasted_iota(jnp.int32, (GRID_W, GRID_W), 1)
        cs = jnp.clip(qc - NA_WIN_COLS // 2, 0, GRID_W - NA_WIN_COLS)
        col_ok = jnp.logical_and(kc >= cs, kc < cs + NA_WIN_COLS)
        dc = kc - qc + (NA_WIN_COLS - 1)
        for h in range(N_HEADS):
            for dr in range(N_DR):
                tile = neg_tile
                for d in range(N_DC):
                    tile = jnp.where(dc == d, rpb_ref[(h * N_DR + dr) * N_DC + d], tile)
                bcol_ref[h, dr] = jnp.where(col_ok, tile, NEG_BIG)

    def assemble(qb_static):
        for h in range(N_HEADS):
            for j in range(NA_QROWS):
                r = NA_QROWS * qb_static + j
                rs = min(max(r - NA_WIN_ROWS // 2, 0), rows - NA_WIN_ROWS)
                for i2 in range(0, 3 * NA_QROWS, 2):
                    halves = []
                    for i in (i2, i2 + 1):
                        kr = NA_QROWS * (qb_static - 1) + i
                        ok = rs <= kr < rs + NA_WIN_ROWS
                        halves.append(bcol_ref[h, kr - r + NA_WIN_ROWS - 1] if ok else neg_tile)
                    slab_ref[h, j * GRID_W:(j + 1) * GRID_W, i2 * GRID_W:(i2 + 2) * GRID_W] = (
                        jnp.concatenate(halves, axis=1))

    for qb_static in (0, 1, nb - 1):
        pl.when(qb == qb_static)(functools.partial(assemble, qb_static))

    for h in range(N_HEADS):
        sl = slice(h * HEAD_DIM, (h + 1) * HEAD_DIM)
        k = jnp.concatenate([kp_ref[0, :, sl], kc_ref[0, :, sl], kn_ref[0, :, sl]], axis=0)
        v = jnp.concatenate([vp_ref[0, :, sl], vc_ref[0, :, sl], vn_ref[0, :, sl]], axis=0)
        s = _dot_nt(q_ref[0, :, sl], k) + slab_ref[h]
        p = jnp.exp(s - jnp.max(s, axis=-1, keepdims=True))
        l = jnp.sum(p, axis=-1, keepdims=True)
        o = jnp.dot(p.astype(BF16), v, preferred_element_type=F32) / l
        o_ref[0, :, sl] = o.astype(o_ref.dtype)


def neighborhood_attention(proj, rpb):
    b, s, _ = proj.shape
    rows = s // GRID_W
    assert NA_QROWS >= NA_WIN_ROWS // 2 and NA_QROWS % 2 == 0 and rows % NA_QROWS == 0
    assert rows >= 3 * NA_QROWS and rows >= NA_WIN_ROWS
    nq = NA_QROWS * GRID_W
    nb = rows // NA_QROWS
    width = N_HEADS * HEAD_DIM

    def blk(group, shift):
        return pl.BlockSpec((1, nq, width),
                            lambda bi, qb, rp: (bi, jnp.clip(qb + shift, 0, nb - 1), group))

    grid_spec = pltpu.PrefetchScalarGridSpec(
        num_scalar_prefetch=1,
        grid=(b, nb),
        in_specs=[blk(4, 0), blk(5, -1), blk(5, 0), blk(5, 1), blk(6, -1), blk(6, 0), blk(6, 1)],
        out_specs=pl.BlockSpec((1, nq, width), lambda bi, qb, rp: (bi, qb, 0)),
        scratch_shapes=[pltpu.VMEM((N_HEADS, N_DR, GRID_W, GRID_W), F32),
                        pltpu.VMEM((N_HEADS, nq, 3 * nq), F32)],
    )
    return pl.pallas_call(
        functools.partial(_na_kernel, rows=rows),
        grid_spec=grid_spec,
        out_shape=jax.ShapeDtypeStruct((b, s, width), BF16),
        compiler_params=_cparams(("arbitrary", "arbitrary")),
        name="neighborhood_attention",
    )(rpb.astype(F32).reshape(-1), proj, proj, proj, proj, proj, proj, proj)


def _bf16_pieces(x, n):
    out, rest = [], float(x)
    for _ in range(n):
        piece = float(np.float32(rest).astype(BF16).astype(np.float32))
        out.append(piece)
        rest -= piece
    return out


LOG2E = math.log2(math.e)
LOG2E_PIECES = _bf16_pieces(LOG2E, 3)
N_ALIBI_COLS = 12
POS_LO_BITS = 7


def _alibi_cols(pos, slope, lane, unit_cols, sign):
    hi = (pos >> POS_LO_BITS).astype(F32) * (slope * float(1 << POS_LO_BITS))
    lo = (pos & ((1 << POS_LO_BITS) - 1)).astype(F32) * slope
    if sign > 0:
        return jnp.where(lane < 3, hi, jnp.where(lane < 6, lo, unit_cols))
    return jnp.where(lane < 6, unit_cols, jnp.where(lane < 9, hi, jnp.where(lane < N_ALIBI_COLS, lo, 0.0)))


def _diff_attn_kernel(slopes_ref, q_ref, k_ref, v_ref, cq_ref, ck_ref, lam_ref, og_ref, o_ref,
                      kaug_ref, qaug_ref, s_ref, bias_ref, acc_ref, *, lam_init, nsub, npv):
    h = pl.program_id(1)
    qi = pl.program_id(2)
    t = q_ref.shape[1]
    nk = k_ref.shape[1] // t
    nslab = t // LANE
    slope = slopes_ref[h]
    lane = lax.broadcasted_iota(jnp.int32, (t, LANE), 1)
    row = lax.broadcasted_iota(jnp.int32, (t, LANE), 0)

    @pl.when(qi == 0)
    def _():
        def body(r, carry):
            rs = pl.multiple_of(r * t, t)
            kx = _alibi_cols(row + rs, slope, lane, ck_ref[...], 1).astype(BF16)
            for c in range(2):
                kaug_ref[c, pl.ds(rs, t), 0:HEAD_DIM] = k_ref[0, pl.ds(rs, t), c * HEAD_DIM:(c + 1) * HEAD_DIM]
                kaug_ref[c, pl.ds(rs, t), HEAD_DIM:2 * HEAD_DIM] = kx
            return carry

        lax.fori_loop(0, nk, body, 0)
        ri = lax.broadcasted_iota(jnp.int32, (t, t), 0)
        ci = lax.broadcasted_iota(jnp.int32, (t, t), 1)
        bias_ref[0] = jnp.zeros((t, t), F32)
        bias_ref[1] = jnp.abs(ri - ci).astype(F32) * (-LOG2E * slope)

    qx = _alibi_cols(row + qi * t, slope, lane, cq_ref[...], -1)
    for c in range(2):
        qc = q_ref[0, :, c * HEAD_DIM:(c + 1) * HEAD_DIM]
        for side, ext in ((0, qx), (1, -qx), (2, jnp.zeros_like(qx))):
            qaug_ref[c, side, :, 0:HEAD_DIM] = qc
            qaug_ref[c, side, :, HEAD_DIM:2 * HEAD_DIM] = ext.astype(BF16)

    width = npv * t
    ngroups = nk // npv
    outs = []
    for c in range(2):
        def scores(g, m):
            for i in range(nsub):
                kb = g * nsub + i
                side = jnp.where(kb < qi, 0, jnp.where(kb > qi, 1, 2))
                ks = pl.multiple_of(kb * t, t)
                s = _dot_nt(qaug_ref[c, side], kaug_ref[c, pl.ds(ks, t), :])
                s = s + bias_ref[(kb == qi).astype(jnp.int32)]
                s_ref[kb] = s
                for u in range(nslab):
                    m = jnp.maximum(m, s[:, u * LANE:(u + 1) * LANE])
            return m

        m = lax.fori_loop(0, nk // nsub, scores, jnp.full((t, LANE), NEG_BIG, F32))
        mb = jnp.broadcast_to(jnp.max(m, axis=-1, keepdims=True), (t, LANE))
        acc_ref[...] = jnp.zeros_like(acc_ref)

        def pv(g, lrun):
            ps = []
            for i in range(npv):
                for u in range(nslab):
                    p = jnp.exp2(s_ref[g * npv + i, :, u * LANE:(u + 1) * LANE] - mb)
                    lrun = lrun + p
                    ps.append(p.astype(BF16))
            vs = v_ref[0, pl.ds(pl.multiple_of(g * width, width), width), :]
            acc_ref[...] += jnp.dot(jnp.concatenate(ps, axis=1), vs, preferred_element_type=F32)
            return lrun

        lrun = lax.fori_loop(0, ngroups, pv, jnp.zeros((t, LANE), F32))
        outs.append(acc_ref[...] / jnp.sum(lrun, axis=-1, keepdims=True))

    lam = (jnp.exp(jnp.sum(lam_ref[0:1, :] * lam_ref[1:2, :], axis=-1, keepdims=True))
           - jnp.exp(jnp.sum(lam_ref[2:3, :] * lam_ref[3:4, :], axis=-1, keepdims=True)) + lam_init)
    o = outs[0] - lam * outs[1]
    o_ref[0] = (_rms(o) * og_ref[...] * (1.0 - lam_init)).astype(o_ref.dtype)


def diff_attention(proj, lam_params, og, *, lam_init, t):
    b, s, _ = proj.shape
    assert s % t == 0 and s <= (1 << (2 * POS_LO_BITS + 1)) and t % LANE == 0
    nk = s // t
    nsub = max(n for n in (8, 4, 3, 2, 1) if nk % n == 0)
    npv = max(n for n in (4, 3, 2, 1) if nk % n == 0)
    vw = 2 * HEAD_DIM
    slopes = jnp.asarray(2.0 ** (-8.0 * np.arange(1, N_HEADS + 1) / N_HEADS), F32)
    pieces = np.zeros((1, LANE), np.float32)
    pieces[0, :6] = LOG2E_PIECES * 2
    cq = jnp.asarray(pieces)
    ck = jnp.asarray(-np.roll(pieces, 6, axis=1))
    const = lambda shape: pl.BlockSpec(shape, lambda bi, h, qi, sl: (0, 0))
    grid_spec = pltpu.PrefetchScalarGridSpec(
        num_scalar_prefetch=1,
        grid=(b, N_HEADS, nk),
        in_specs=[
            pl.BlockSpec((1, t, vw), lambda bi, h, qi, sl: (bi, qi, h)),
            pl.BlockSpec((1, s, vw), lambda bi, h, qi, sl: (bi, 0, N_HEADS + h), pipeline_mode=pl.Buffered(1)),
            pl.BlockSpec((1, s, vw), lambda bi, h, qi, sl: (bi, 0, 2 * N_HEADS + h), pipeline_mode=pl.Buffered(1)),
            const((1, LANE)), const((1, LANE)), const((4, HEAD_DIM)), const((1, vw)),
        ],
        out_specs=pl.BlockSpec((1, t, vw), lambda bi, h, qi, sl: (bi, qi, h)),
        scratch_shapes=[pltpu.VMEM((2, s, vw), BF16),
                        pltpu.VMEM((2, 3, t, vw), BF16),
                        pltpu.VMEM((nk, t, t), F32),
                        pltpu.VMEM((2, t, t), F32),
                        pltpu.VMEM((t, vw), F32)],
    )
    return pl.pallas_call(
        functools.partial(_diff_attn_kernel, lam_init=lam_init, nsub=nsub, npv=npv),
        grid_spec=grid_spec,
        out_shape=jax.ShapeDtypeStruct((b, s, N_HEADS * vw), BF16),
        compiler_params=_cparams(("arbitrary", "arbitrary", "arbitrary")),
        name="diff_attention",
    )(slopes, proj, proj, proj, cq, ck, lam_params, og)


def _row(v):
    return v.astype(F32).reshape(1, -1)


def _lane_bcast(v):
    return jnp.broadcast_to(v.astype(F32)[:, None], (v.shape[0], LANE))


def kernel(x, mem, norm_mix_g, norm_xattn_g, norm_mem_g, norm_ffn_g, w_in_ab, ret_decay_fwd, ret_decay_bwd, ret_out_g, na_q_g, na_k_g, na_rpb, w_out_ab, w_in_c, diff_q_g, diff_k_g, lambda_q1, lambda_k1, lambda_q2, lambda_k2, diff_out_g, w_out_c, w_xq, w_xkv, w_xo, xq_g, xk_g, w_ffn_in, w_ffn_out):
    b, s, d = x.shape
    n_mem = mem.shape[1]
    depth = norm_mix_g.shape[0]
    t = b * s
    scale = HEAD_DIM ** -0.5
    width = N_HEADS * HEAD_DIM
    xw = N_X_HEADS * HEAD_DIM
    tm, tn = 1024, 512
    tn_in = 1024

    xf = x.reshape(t, d)
    memf = mem.reshape(b * n_mem, d)
    ones_cols = lambda n: jnp.ones((1, n), F32)

    for i in range(depth):
        j = i // 2
        if i % 2 == 0:
            w_in = w_in_ab[j].astype(BF16)
            gcol = jnp.concatenate([
                ones_cols(4 * width),
                jnp.tile(_row(na_q_g[j]) * scale, (1, N_HEADS)),
                jnp.tile(_row(na_k_g[j]), (1, N_HEADS)),
                ones_cols(width)], axis=1)
            proj = norm_matmul(xf, _row(norm_mix_g[i]), w_in, gcol,
                               4 * width // tn_in, 6 * width // tn_in, tm=tm, tn=tn_in)
            proj = proj.reshape(b, s, 7 * width)
            lgf = _lane_bcast(jax.nn.log_sigmoid(ret_decay_fwd[j].astype(F32)))
            lgb = _lane_bcast(jax.nn.log_sigmoid(ret_decay_bwd[j].astype(F32)))
            sb = ret_states(proj, lgb, chunk=RET_CHUNK)
            ret = ret_out(proj, sb, lgf, lgb, _row(ret_out_g[j]), chunk=RET_CHUNK)
            na = neighborhood_attention(proj, na_rpb[j])
            mixed = [ret.reshape(t, width), na.reshape(t, width)]
            xf = matmul_res(mixed, w_out_ab[j].astype(BF16), xf, tm=tm, tn=tn_in)
        else:
            w_in = w_in_c[j].astype(BF16)
            cw = 2 * width
            gcol = jnp.concatenate([
                jnp.tile(_row(diff_q_g[j]) * (scale * LOG2E), (1, 2 * N_HEADS)),
                jnp.tile(_row(diff_k_g[j]), (1, 2 * N_HEADS)),
                ones_cols(cw)], axis=1)
            proj = norm_matmul(xf, _row(norm_mix_g[i]), w_in, gcol, 0, 2 * cw // tn_in, tm=tm, tn=tn_in)
            proj = proj.reshape(b, s, 3 * cw)
            lam_params = jnp.stack([lambda_q1[j], lambda_k1[j], lambda_q2[j], lambda_k2[j]]).astype(F32)
            lam_init = 0.8 - 0.6 * math.exp(-0.3 * i)
            att = diff_attention(proj, lam_params, _row(diff_out_g[j]), lam_init=lam_init, t=512)
            xf = matmul_res([att.reshape(t, cw)], w_out_c[j].astype(BF16), xf, tm=tm, tn=tn_in)

        kv_gcol = jnp.concatenate([jnp.tile(_row(xk_g[i]), (1, N_X_HEADS)), ones_cols(xw)], axis=1)
        kv = norm_matmul(memf, _row(norm_mem_g[i]), w_xkv[i].astype(BF16), kv_gcol, 0, 1,
                         tm=b * n_mem, tn=xw)
        gq = jnp.tile(_row(xq_g[i]) * scale, (1, N_X_HEADS))
        xf = xattn(xf, _row(norm_xattn_g[i]), w_xq[i].astype(BF16), gq, kv, w_xo[i].astype(BF16),
                   tm=512, seq=s)

        hmid = swiglu_in(xf, _row(norm_ffn_g[i]), w_ffn_in[i].astype(BF16), tm=tm, tn=tn)
        xf = matmul_res([hmid], w_ffn_out[i].astype(BF16), xf, tm=tm, tn=tn)

    return xf.reshape(b, s, d)
```

```python
import functools
import math

import jax
import jax.numpy as jnp
import numpy as np
from jax import lax
from jax.experimental import pallas as pl
from jax.experimental.pallas import tpu as pltpu

F32 = jnp.float32
BF16 = jnp.bfloat16

HEAD_DIM = 128
N_HEADS = 8
N_X_HEADS = 4
GRID_W = 64
NA_WIN_ROWS = 8
NA_WIN_COLS = 16
NA_QROWS = 4
RET_CHUNK = 256
RMS_EPS = 1e-6
NEG_BIG = -1e30

LANE = 128
VMEM_LIMIT = 56 * 1024 * 1024


def _cparams(sem):
    return pltpu.CompilerParams(dimension_semantics=sem, vmem_limit_bytes=VMEM_LIMIT)


def _dot_nt(a, b):
    return lax.dot_general(a, b, (((1,), (1,)), ((), ())), preferred_element_type=F32)


def _dot_nn(a, b):
    return lax.dot_general(a, b, (((1,), (0,)), ((), ())), preferred_element_type=F32)


def _dot_tn(a, b):
    return lax.dot_general(a, b, (((0,), (0,)), ((), ())), preferred_element_type=F32)


def _layer_spec(w, layer, block, index_map):
    if w.ndim == 2:
        return pl.BlockSpec(block, index_map)
    return pl.BlockSpec((None,) + tuple(block), lambda *idx: (layer,) + tuple(index_map(*idx)))


def _rms(x):
    return x * lax.rsqrt(jnp.mean(x * x, axis=-1, keepdims=True) + RMS_EPS)


def _norm_rows_to_scratch(x_ref, g_ref, xn_ref, rows):
    n = x_ref.shape[0] // rows

    def body(r, c):
        sl = pl.ds(pl.multiple_of(r * rows, rows), rows)
        xn_ref[sl, :] = (_rms(x_ref[sl, :]) * g_ref[...]).astype(BF16)
        return c

    lax.fori_loop(0, n, body, 0)


def _norm_matmul_kernel(x_ref, g_ref, w_ref, gc_ref, o_ref, xn_ref, *, lo, hi):
    j = pl.program_id(1)

    @pl.when(j == 0)
    def _():
        _norm_rows_to_scratch(x_ref, g_ref, xn_ref, 128)

    acc = _dot_nn(xn_ref[...], w_ref[...])
    grouped = jnp.logical_and(j >= lo, j < hi)
    for c in range(acc.shape[1] // HEAD_DIM):
        sl = slice(c * HEAD_DIM, (c + 1) * HEAD_DIM)
        a = acc[:, sl]
        inv = lax.rsqrt(jnp.mean(a * a, axis=-1, keepdims=True) + RMS_EPS)
        o_ref[:, sl] = ((a * jnp.where(grouped, inv, 1.0)) * gc_ref[:, sl]).astype(o_ref.dtype)


def norm_matmul(x, g, w, layer, gcol, lo, hi, *, tm, tn, out_dtype=BF16):
    t, d = x.shape
    n = w.shape[-1]
    return pl.pallas_call(
        functools.partial(_norm_matmul_kernel, lo=lo, hi=hi),
        grid=(t // tm, n // tn),
        in_specs=[
            pl.BlockSpec((tm, d), lambda i, j: (i, 0)),
            pl.BlockSpec((1, d), lambda i, j: (0, 0)),
            _layer_spec(w, layer, (d, tn), lambda i, j: (0, j)),
            pl.BlockSpec((1, tn), lambda i, j: (0, j)),
        ],
        out_specs=pl.BlockSpec((tm, tn), lambda i, j: (i, j)),
        out_shape=jax.ShapeDtypeStruct((t, n), out_dtype),
        scratch_shapes=[pltpu.VMEM((tm, d), BF16)],
        compiler_params=_cparams(("parallel", "arbitrary")),
        name="norm_matmul",
    )(x, g, w, gcol)


def _swiglu_in_kernel(x_ref, g_ref, wg_ref, wu_ref, o_ref, xn_ref):
    @pl.when(pl.program_id(1) == 0)
    def _():
        _norm_rows_to_scratch(x_ref, g_ref, xn_ref, 128)

    xn = xn_ref[...]
    gate = _dot_nn(xn, wg_ref[...])
    up = _dot_nn(xn, wu_ref[...])
    o_ref[...] = (gate * jax.nn.sigmoid(gate) * up).astype(o_ref.dtype)


def swiglu_in(x, g, w_in, layer, *, tm, tn):
    t, d = x.shape
    hidden = w_in.shape[-1] // 2
    nh = hidden // tn
    return pl.pallas_call(
        _swiglu_in_kernel,
        grid=(t // tm, nh),
        in_specs=[
            pl.BlockSpec((tm, d), lambda i, j: (i, 0)),
            pl.BlockSpec((1, d), lambda i, j: (0, 0)),
            _layer_spec(w_in, layer, (d, tn), lambda i, j: (0, j)),
            _layer_spec(w_in, layer, (d, tn), lambda i, j: (0, j + nh)),
        ],
        out_specs=pl.BlockSpec((tm, tn), lambda i, j: (i, j)),
        out_shape=jax.ShapeDtypeStruct((t, hidden), BF16),
        scratch_shapes=[pltpu.VMEM((tm, d), BF16)],
        compiler_params=_cparams(("parallel", "arbitrary")),
        name="swiglu_in",
    )(x, g, w_in, w_in)


def _matmul_res_kernel(*refs):
    *a_refs, w_ref, r_ref, o_ref = refs
    acc = r_ref[...]
    k0 = 0
    for a_ref in a_refs:
        k1 = k0 + a_ref.shape[1]
        acc += _dot_nn(a_ref[...], w_ref[k0:k1, :])
        k0 = k1
    o_ref[...] = acc


def matmul_res(a_parts, w, layer, res, *, tm, tn):
    t = res.shape[0]
    k, n = w.shape[-2:]
    assert sum(a.shape[1] for a in a_parts) == k
    return pl.pallas_call(
        _matmul_res_kernel,
        grid=(t // tm, n // tn),
        in_specs=[pl.BlockSpec((tm, a.shape[1]), lambda i, j: (i, 0)) for a in a_parts] + [
            _layer_spec(w, layer, (k, tn), lambda i, j: (0, j)),
            pl.BlockSpec((tm, tn), lambda i, j: (i, j)),
        ],
        out_specs=pl.BlockSpec((tm, tn), lambda i, j: (i, j)),
        out_shape=jax.ShapeDtypeStruct((t, n), F32),
        compiler_params=_cparams(("parallel", "arbitrary")),
        name="matmul_res",
    )(*a_parts, w, res)


def _xattn_kernel(x_ref, g_ref, wq_ref, gq_ref, kv_ref, wo_ref, o_ref):
    x = x_ref[...]
    xn = (_rms(x) * g_ref[...]).astype(BF16)
    q = _dot_nn(xn, wq_ref[...])
    width = N_X_HEADS * HEAD_DIM
    outs = []
    for h in range(N_X_HEADS):
        sl = slice(h * HEAD_DIM, (h + 1) * HEAD_DIM)
        qh = (_rms(q[:, sl]) * gq_ref[:, sl]).astype(BF16)
        s = _dot_nt(qh, kv_ref[:, sl])
        p = jnp.exp(s - jnp.max(s, axis=-1, keepdims=True))
        l = jnp.sum(p, axis=-1, keepdims=True)
        vh = kv_ref[:, width + h * HEAD_DIM: width + (h + 1) * HEAD_DIM]
        oh = jnp.dot(p.astype(BF16), vh, preferred_element_type=F32) / l
        outs.append(oh.astype(BF16))
    o = jnp.concatenate(outs, axis=1)
    o_ref[...] = x + _dot_nn(o, wo_ref[...])


def xattn(x, g, wq, gq, kv, wo, layer, *, tm, seq):
    t, d = x.shape
    n_mem = kv.shape[0] // (t // seq)
    width = wq.shape[-1]
    per_b = seq // tm
    return pl.pallas_call(
        _xattn_kernel,
        grid=(t // tm,),
        in_specs=[
            pl.BlockSpec((tm, d), lambda i: (i, 0)),
            pl.BlockSpec((1, d), lambda i: (0, 0)),
            _layer_spec(wq, layer, (d, width), lambda i: (0, 0)),
            pl.BlockSpec((1, width), lambda i: (0, 0)),
            pl.BlockSpec((n_mem, 2 * width), lambda i: (i // per_b, 0)),
            _layer_spec(wo, layer, (width, d), lambda i: (0, 0)),
        ],
        out_specs=pl.BlockSpec((tm, d), lambda i: (i, 0)),
        out_shape=jax.ShapeDtypeStruct((t, d), F32),
        compiler_params=_cparams(("parallel",)),
        name="xattn",
    )(x, g, wq, gq, kv, wo)


def _ret_state_kernel(k_ref, v_ref, lgb_ref, sb_ref, state_ref, kd_ref):
    c = pl.program_id(1)
    chunk = k_ref.shape[1]
    scale = HEAD_DIM ** -0.5

    @pl.when(c == 0)
    def _():
        state_ref[...] = jnp.zeros_like(state_ref)
        pos = lax.broadcasted_iota(jnp.int32, (chunk, HEAD_DIM), 0).astype(F32)
        for h in range(N_HEADS):
            kd_ref[h] = jnp.exp(lgb_ref[h:h + 1, :] * pos) * scale

    for h in range(N_HEADS):
        sl = slice(h * HEAD_DIM, (h + 1) * HEAD_DIM)
        st = state_ref[h]
        sb_ref[0, 0, h] = st.astype(BF16)
        kd = (k_ref[0, :, sl].astype(F32) * kd_ref[h]).astype(BF16)
        cdec = jnp.exp(lgb_ref[h:h + 1, :] * float(chunk))
        state_ref[h] = st * cdec + _dot_tn(kd, v_ref[0, :, sl])


def ret_states(proj, lgb, *, chunk):
    b, s, _ = proj.shape
    nc = s // chunk
    width = N_HEADS * HEAD_DIM
    return pl.pallas_call(
        _ret_state_kernel,
        grid=(b, nc),
        in_specs=[
            pl.BlockSpec((1, chunk, width), lambda bi, c: (bi, nc - 1 - c, 1)),
            pl.BlockSpec((1, chunk, width), lambda bi, c: (bi, nc - 1 - c, 2)),
            pl.BlockSpec((N_HEADS, LANE), lambda bi, c: (0, 0)),
        ],
        out_specs=pl.BlockSpec((1, 1, N_HEADS, HEAD_DIM, HEAD_DIM), lambda bi, c: (bi, nc - 1 - c, 0, 0, 0)),
        out_shape=jax.ShapeDtypeStruct((b, nc, N_HEADS, HEAD_DIM, HEAD_DIM), BF16),
        scratch_shapes=[pltpu.VMEM((N_HEADS, HEAD_DIM, HEAD_DIM), F32),
                        pltpu.VMEM((N_HEADS, chunk, HEAD_DIM), F32)],
        compiler_params=_cparams(("parallel", "arbitrary")),
        name="ret_states",
    )(proj, proj, lgb)


def _ret_out_kernel(q_ref, k_ref, v_ref, rg_ref, sb_ref, lgf_ref, lgb_ref, og_ref, o_ref,
                    state_ref, dmat_ref, qdf_ref, qdb_ref, kdf_ref):
    c = pl.program_id(1)
    chunk = q_ref.shape[1]
    scale = HEAD_DIM ** -0.5

    @pl.when(c == 0)
    def _():
        state_ref[...] = jnp.zeros_like(state_ref)
        pos = lax.broadcasted_iota(jnp.int32, (chunk, HEAD_DIM), 0).astype(F32)
        ri = lax.broadcasted_iota(jnp.int32, (chunk, chunk), 0)
        ci = lax.broadcasted_iota(jnp.int32, (chunk, chunk), 1)
        diff = (ri - ci).astype(F32)
        for h in range(N_HEADS):
            lf = lgf_ref[h:h + 1, :]
            lb = lgb_ref[h:h + 1, :]
            dfwd = jnp.exp(lf[:, :1] * jnp.maximum(diff, 0.0))
            dbwd = jnp.exp(lb[:, :1] * jnp.maximum(-diff, 0.0))
            dmat_ref[h] = jnp.where(diff >= 0.0, dfwd, dbwd) * scale
            qdf_ref[h] = jnp.exp(lf * (pos + 1.0))
            qdb_ref[h] = jnp.exp(lb * (float(chunk) - pos))
            kdf_ref[h] = jnp.exp(lf * (float(chunk) - 1.0 - pos)) * scale

    for h in range(N_HEADS):
        sl = slice(h * HEAD_DIM, (h + 1) * HEAD_DIM)
        qh = q_ref[0, :, sl]
        kh = k_ref[0, :, sl]
        vh = v_ref[0, :, sl]
        qf32 = qh.astype(F32)
        sd = (_dot_nt(qh, kh) * dmat_ref[h]).astype(BF16)
        st = state_ref[h]
        out = jnp.dot(sd, vh, preferred_element_type=F32)
        out += jnp.dot((qf32 * qdf_ref[h]).astype(BF16), st.astype(BF16), preferred_element_type=F32)
        out += jnp.dot((qf32 * qdb_ref[h]).astype(BF16), sb_ref[0, 0, h], preferred_element_type=F32)
        gate = rg_ref[0, :, sl].astype(F32)
        y = _rms(out) * og_ref[...]
        o_ref[0, :, sl] = (y * (gate * jax.nn.sigmoid(gate))).astype(o_ref.dtype)
        kd = (kh.astype(F32) * kdf_ref[h]).astype(BF16)
        cdec = jnp.exp(lgf_ref[h:h + 1, :] * float(chunk))
        state_ref[h] = st * cdec + _dot_tn(kd, vh)


def ret_out(proj, sb, lgf, lgb, og, *, chunk):
    b, s, _ = proj.shape
    nc = s // chunk
    width = N_HEADS * HEAD_DIM
    col = lambda k: pl.BlockSpec((1, chunk, width), lambda bi, c, k=k: (bi, c, k))
    return pl.pallas_call(
        _ret_out_kernel,
        grid=(b, nc),
        in_specs=[
            col(0), col(1), col(2), col(3),
            pl.BlockSpec((1, 1, N_HEADS, HEAD_DIM, HEAD_DIM), lambda bi, c: (bi, c, 0, 0, 0)),
            pl.BlockSpec((N_HEADS, LANE), lambda bi, c: (0, 0)),
            pl.BlockSpec((N_HEADS, LANE), lambda bi, c: (0, 0)),
            pl.BlockSpec((1, HEAD_DIM), lambda bi, c: (0, 0)),
        ],
        out_specs=pl.BlockSpec((1, chunk, width), lambda bi, c: (bi, c, 0)),
        out_shape=jax.ShapeDtypeStruct((b, s, width), BF16),
        scratch_shapes=[pltpu.VMEM((N_HEADS, HEAD_DIM, HEAD_DIM), F32),
                        pltpu.VMEM((N_HEADS, chunk, chunk), F32),
                        pltpu.VMEM((N_HEADS, chunk, HEAD_DIM), F32),
                        pltpu.VMEM((N_HEADS, chunk, HEAD_DIM), F32),
                        pltpu.VMEM((N_HEADS, chunk, HEAD_DIM), F32)],
        compiler_params=_cparams(("parallel", "arbitrary")),
        name="ret_out",
    )(proj, proj, proj, proj, sb, lgf, lgb, og)


N_DR = 2 * NA_WIN_ROWS - 1
N_DC = 2 * NA_WIN_COLS - 1


def _na_kernel(rpb_ref, q_ref, kp_ref, kc_ref, kn_ref, vp_ref, vc_ref, vn_ref, o_ref,
               bcol_ref, slab_ref, *, rows):
    qb = pl.program_id(1)
    nb = rows // NA_QROWS
    neg_tile = jnp.full((GRID_W, GRID_W), NEG_BIG, F32)

    @pl.when(qb == 0)
    def _():
        qc = lax.broadcasted_iota(jnp.int32, (GRID_W, GRID_W), 0)
        kc = lax.broadcasted_iota(jnp.int32, (GRID_W, GRID_W), 1)
        cs = jnp.clip(qc - NA_WIN_COLS // 2, 0, GRID_W - NA_WIN_COLS)
        col_ok = jnp.logical_and(kc >= cs, kc < cs + NA_WIN_COLS)
        dc = kc - qc + (NA_WIN_COLS - 1)
        for h in range(N_HEADS):
            for dr in range(N_DR):
                tile = neg_tile
                for d in range(N_DC):
                    tile = jnp.where(dc == d, rpb_ref[(h * N_DR + dr) * N_DC + d], tile)
                bcol_ref[h, dr] = jnp.where(col_ok, tile, NEG_BIG)

    def assemble(qb_static):
        for h in range(N_HEADS):
            for j in range(NA_QROWS):
                r = NA_QROWS * qb_static + j
                rs = min(max(r - NA_WIN_ROWS // 2, 0), rows - NA_WIN_ROWS)
                for i2 in range(0, 3 * NA_QROWS, 2):
                    halves = []
                    for i in (i2, i2 + 1):
                        kr = NA_QROWS * (qb_static - 1) + i
                        ok = rs <= kr < rs + NA_WIN_ROWS
                        halves.append(bcol_ref[h, kr - r + NA_WIN_ROWS - 1] if ok else neg_tile)
                    slab_ref[h, j * GRID_W:(j + 1) * GRID_W, i2 * GRID_W:(i2 + 2) * GRID_W] = (
                        jnp.concatenate(halves, axis=1))

    for qb_static in (0, 1, nb - 1):
        pl.when(qb == qb_static)(functools.partial(assemble, qb_static))

    for h in range(N_HEADS):
        sl = slice(h * HEAD_DIM, (h + 1) * HEAD_DIM)
        k = jnp.concatenate([kp_ref[0, :, sl], kc_ref[0, :, sl], kn_ref[0, :, sl]], axis=0)
        v = jnp.concatenate([vp_ref[0, :, sl], vc_ref[0, :, sl], vn_ref[0, :, sl]], axis=0)
        s = _dot_nt(q_ref[0, :, sl], k) + slab_ref[h]
        p = jnp.exp(s - jnp.max(s, axis=-1, keepdims=True))
        l = jnp.sum(p, axis=-1, keepdims=True)
        o = jnp.dot(p.astype(BF16), v, preferred_element_type=F32) / l
        o_ref[0, :, sl] = o.astype(o_ref.dtype)


def neighborhood_attention(proj, rpb):
    b, s, _ = proj.shape
    rows = s // GRID_W
    assert NA_QROWS >= NA_WIN_ROWS // 2 and NA_QROWS % 2 == 0 and rows % NA_QROWS == 0
    assert rows >= 3 * NA_QROWS and rows >= NA_WIN_ROWS
    nq = NA_QROWS * GRID_W
    nb = rows // NA_QROWS
    width = N_HEADS * HEAD_DIM

    def blk(group, shift):
        return pl.BlockSpec((1, nq, width),
                            lambda bi, qb, rp: (bi, jnp.clip(qb + shift, 0, nb - 1), group))

    grid_spec = pltpu.PrefetchScalarGridSpec(
        num_scalar_prefetch=1,
        grid=(b, nb),
        in_specs=[blk(4, 0), blk(5, -1), blk(5, 0), blk(5, 1), blk(6, -1), blk(6, 0), blk(6, 1)],
        out_specs=pl.BlockSpec((1, nq, width), lambda bi, qb, rp: (bi, qb, 0)),
        scratch_shapes=[pltpu.VMEM((N_HEADS, N_DR, GRID_W, GRID_W), F32),
                        pltpu.VMEM((N_HEADS, nq, 3 * nq), F32)],
    )
    return pl.pallas_call(
        functools.partial(_na_kernel, rows=rows),
        grid_spec=grid_spec,
        out_shape=jax.ShapeDtypeStruct((b, s, width), BF16),
        compiler_params=_cparams(("arbitrary", "arbitrary")),
        name="neighborhood_attention",
    )(rpb.astype(F32).reshape(-1), proj, proj, proj, proj, proj, proj, proj)


def _bf16_pieces(x, n):
    out, rest = [], float(x)
    for _ in range(n):
        piece = float(np.float32(rest).astype(BF16).astype(np.float32))
        out.append(piece)
        rest -= piece
    return out


LOG2E = math.log2(math.e)
LOG2E_PIECES = _bf16_pieces(LOG2E, 3)
N_ALIBI_COLS = 12
POS_LO_BITS = 7


def _alibi_cols(pos, slope, lane, unit_cols, sign):
    hi = (pos >> POS_LO_BITS).astype(F32) * (slope * float(1 << POS_LO_BITS))
    lo = (pos & ((1 << POS_LO_BITS) - 1)).astype(F32) * slope
    if sign > 0:
        return jnp.where(lane < 3, hi, jnp.where(lane < 6, lo, unit_cols))
    return jnp.where(lane < 6, unit_cols, jnp.where(lane < 9, hi, jnp.where(lane < N_ALIBI_COLS, lo, 0.0)))


def _diff_attn_kernel(slopes_ref, q_ref, k_ref, v_ref, cq_ref, ck_ref, lam_ref, og_ref, o_ref,
                      kaug_ref, qaug_ref, s_ref, bias_ref, acc_ref, *, lam_init, nsub, npv):
    h = pl.program_id(1)
    qi = pl.program_id(2)
    t = q_ref.shape[1]
    nk = k_ref.shape[1] // t
    nslab = t // LANE
    slope = slopes_ref[h]
    lane = lax.broadcasted_iota(jnp.int32, (t, LANE), 1)
    row = lax.broadcasted_iota(jnp.int32, (t, LANE), 0)

    @pl.when(qi == 0)
    def _():
        def body(r, carry):
            rs = pl.multiple_of(r * t, t)
            kx = _alibi_cols(row + rs, slope, lane, ck_ref[...], 1).astype(BF16)
            for c in range(2):
                kaug_ref[c, pl.ds(rs, t), 0:HEAD_DIM] = k_ref[0, pl.ds(rs, t), c * HEAD_DIM:(c + 1) * HEAD_DIM]
                kaug_ref[c, pl.ds(rs, t), HEAD_DIM:2 * HEAD_DIM] = kx
            return carry

        lax.fori_loop(0, nk, body, 0)
        ri = lax.broadcasted_iota(jnp.int32, (t, t), 0)
        ci = lax.broadcasted_iota(jnp.int32, (t, t), 1)
        bias_ref[0] = jnp.zeros((t, t), F32)
        bias_ref[1] = jnp.abs(ri - ci).astype(F32) * (-LOG2E * slope)

    qx = _alibi_cols(row + qi * t, slope, lane, cq_ref[...], -1)
    for c in range(2):
        qc = q_ref[0, :, c * HEAD_DIM:(c + 1) * HEAD_DIM]
        for side, ext in ((0, qx), (1, -qx), (2, jnp.zeros_like(qx))):
            qaug_ref[c, side, :, 0:HEAD_DIM] = qc
            qaug_ref[c, side, :, HEAD_DIM:2 * HEAD_DIM] = ext.astype(BF16)

    width = npv * t
    ngroups = nk // npv
    outs = []
    for c in range(2):
        def scores(g, m):
            for i in range(nsub):
                kb = g * nsub + i
                side = jnp.where(kb < qi, 0, jnp.where(kb > qi, 1, 2))
                ks = pl.multiple_of(kb * t, t)
                s = _dot_nt(qaug_ref[c, side], kaug_ref[c, pl.ds(ks, t), :])
                s = s + bias_ref[(kb == qi).astype(jnp.int32)]
                s_ref[kb] = s
                for u in range(nslab):
                    m = jnp.maximum(m, s[:, u * LANE:(u + 1) * LANE])
            return m

        m = lax.fori_loop(0, nk // nsub, scores, jnp.full((t, LANE), NEG_BIG, F32))
        mb = jnp.broadcast_to(jnp.max(m, axis=-1, keepdims=True), (t, LANE))
        acc_ref[...] = jnp.zeros_like(acc_ref)

        def pv(g, lrun):
            ps = []
            for i in range(npv):
                for u in range(nslab):
                    p = jnp.exp2(s_ref[g * npv + i, :, u * LANE:(u + 1) * LANE] - mb)
                    lrun = lrun + p
                    ps.append(p.astype(BF16))
            vs = v_ref[0, pl.ds(pl.multiple_of(g * width, width), width), :]
            acc_ref[...] += jnp.dot(jnp.concatenate(ps, axis=1), vs, preferred_element_type=F32)
            return lrun

        lrun = lax.fori_loop(0, ngroups, pv, jnp.zeros((t, LANE), F32))
        outs.append(acc_ref[...] / jnp.sum(lrun, axis=-1, keepdims=True))

    lam = (jnp.exp(jnp.sum(lam_ref[0:1, :] * lam_ref[1:2, :], axis=-1, keepdims=True))
           - jnp.exp(jnp.sum(lam_ref[2:3, :] * lam_ref[3:4, :], axis=-1, keepdims=True)) + lam_init)
    o = outs[0] - lam * outs[1]
    o_ref[0] = (_rms(o) * og_ref[...] * (1.0 - lam_init)).astype(o_ref.dtype)


def diff_attention(proj, lam_params, og, *, lam_init, t):
    b, s, _ = proj.shape
    assert s % t == 0 and s <= (1 << (2 * POS_LO_BITS + 1)) and t % LANE == 0
    nk = s // t
    nsub = max(n for n in (8, 4, 3, 2, 1) if nk % n == 0)
    npv = max(n for n in (8, 4, 3, 2, 1) if nk % n == 0)
    vw = 2 * HEAD_DIM
    slopes = jnp.asarray(2.0 ** (-8.0 * np.arange(1, N_HEADS + 1) / N_HEADS), F32)
    pieces = np.zeros((1, LANE), np.float32)
    pieces[0, :6] = LOG2E_PIECES * 2
    cq = jnp.asarray(pieces)
    ck = jnp.asarray(-np.roll(pieces, 6, axis=1))
    const = lambda shape: pl.BlockSpec(shape, lambda bi, h, qi, sl: (0, 0))
    grid_spec = pltpu.PrefetchScalarGridSpec(
        num_scalar_prefetch=1,
        grid=(b, N_HEADS, nk),
        in_specs=[
            pl.BlockSpec((1, t, vw), lambda bi, h, qi, sl: (bi, qi, h)),
            pl.BlockSpec((1, s, vw), lambda bi, h, qi, sl: (bi, 0, N_HEADS + h)),
            pl.BlockSpec((1, s, vw), lambda bi, h, qi, sl: (bi, 0, 2 * N_HEADS + h)),
            const((1, LANE)), const((1, LANE)), const((4, HEAD_DIM)), const((1, vw)),
        ],
        out_specs=pl.BlockSpec((1, t, vw), lambda bi, h, qi, sl: (bi, qi, h)),
        scratch_shapes=[pltpu.VMEM((2, s, vw), BF16),
                        pltpu.VMEM((2, 3, t, vw), BF16),
                        pltpu.VMEM((nk, t, t), F32),
                        pltpu.VMEM((2, t, t), F32),
                        pltpu.VMEM((t, vw), F32)],
    )
    return pl.pallas_call(
        functools.partial(_diff_attn_kernel, lam_init=lam_init, nsub=nsub, npv=npv),
        grid_spec=grid_spec,
        out_shape=jax.ShapeDtypeStruct((b, s, N_HEADS * vw), BF16),
        compiler_params=_cparams(("arbitrary", "arbitrary", "arbitrary")),
        name="diff_attention",
    )(slopes, proj, proj, proj, cq, ck, lam_params, og)


def _row(v):
    return v.astype(F32).reshape(1, -1)


def _lane_bcast(v):
    return jnp.broadcast_to(v.astype(F32)[:, None], (v.shape[0], LANE))


def kernel(x, mem, norm_mix_g, norm_xattn_g, norm_mem_g, norm_ffn_g, w_in_ab, ret_decay_fwd, ret_decay_bwd, ret_out_g, na_q_g, na_k_g, na_rpb, w_out_ab, w_in_c, diff_q_g, diff_k_g, lambda_q1, lambda_k1, lambda_q2, lambda_k2, diff_out_g, w_out_c, w_xq, w_xkv, w_xo, xq_g, xk_g, w_ffn_in, w_ffn_out):
    b, s, d = x.shape
    n_mem = mem.shape[1]
    depth = norm_mix_g.shape[0]
    t = b * s
    scale = HEAD_DIM ** -0.5
    width = N_HEADS * HEAD_DIM
    xw = N_X_HEADS * HEAD_DIM
    tm, tn = 1024, 512
    tn_in = 1024

    xf = x.reshape(t, d)
    memf = mem.reshape(b * n_mem, d)
    ones_cols = lambda n: jnp.ones((1, n), F32)
    w_ffn_out_bf16 = w_ffn_out.astype(BF16)

    for i in range(depth):
        j = i // 2
        if i % 2 == 0:
            gcol = jnp.concatenate([
                ones_cols(4 * width),
                jnp.tile(_row(na_q_g[j]) * scale, (1, N_HEADS)),
                jnp.tile(_row(na_k_g[j]), (1, N_HEADS)),
                ones_cols(width)], axis=1)
            proj = norm_matmul(xf, _row(norm_mix_g[i]), w_in_ab, j, gcol,
                               4 * width // tn_in, 6 * width // tn_in, tm=tm, tn=tn_in)
            proj = proj.reshape(b, s, 7 * width)
            lgf = _lane_bcast(jax.nn.log_sigmoid(ret_decay_fwd[j].astype(F32)))
            lgb = _lane_bcast(jax.nn.log_sigmoid(ret_decay_bwd[j].astype(F32)))
            sb = ret_states(proj, lgb, chunk=RET_CHUNK)
            ret = ret_out(proj, sb, lgf, lgb, _row(ret_out_g[j]), chunk=RET_CHUNK)
            na = neighborhood_attention(proj, na_rpb[j])
            mixed = [ret.reshape(t, width), na.reshape(t, width)]
            xf = matmul_res(mixed, w_out_ab, j, xf, tm=tm, tn=tn_in)
        else:
            cw = 2 * width
            gcol = jnp.concatenate([
                jnp.tile(_row(diff_q_g[j]) * (scale * LOG2E), (1, 2 * N_HEADS)),
                jnp.tile(_row(diff_k_g[j]), (1, 2 * N_HEADS)),
                ones_cols(cw)], axis=1)
            proj = norm_matmul(xf, _row(norm_mix_g[i]), w_in_c, j, gcol, 0, 2 * cw // tn_in, tm=tm, tn=tn_in)
            proj = proj.reshape(b, s, 3 * cw)
            lam_params = jnp.stack([lambda_q1[j], lambda_k1[j], lambda_q2[j], lambda_k2[j]]).astype(F32)
            lam_init = 0.8 - 0.6 * math.exp(-0.3 * i)
            att = diff_attention(proj, lam_params, _row(diff_out_g[j]), lam_init=lam_init, t=512)
            xf = matmul_res([att.reshape(t, cw)], w_out_c, j, xf, tm=tm, tn=tn_in)

        kv_gcol = jnp.concatenate([jnp.tile(_row(xk_g[i]), (1, N_X_HEADS)), ones_cols(xw)], axis=1)
        kv = norm_matmul(memf, _row(norm_mem_g[i]), w_xkv, i, kv_gcol, 0, 1, tm=b * n_mem, tn=xw)
        gq = jnp.tile(_row(xq_g[i]) * scale, (1, N_X_HEADS))
        xf = xattn(xf, _row(norm_xattn_g[i]), w_xq, gq, kv, w_xo, i, tm=512, seq=s)

        hmid = swiglu_in(xf, _row(norm_ffn_g[i]), w_ffn_in, i, tm=tm, tn=tn)
        xf = matmul_res([hmid], w_ffn_out_bf16, i, xf, tm=tm, tn=tn)

    return xf.reshape(b, s, d)
```

```python
import functools
import math

import jax
import jax.numpy as jnp
import numpy as np
from jax import lax
from jax.experimental import pallas as pl
from jax.experimental.pallas import tpu as pltpu

F32 = jnp.float32
BF16 = jnp.bfloat16

HEAD_DIM = 128
N_HEADS = 8
N_X_HEADS = 4
GRID_W = 64
NA_WIN_ROWS = 8
NA_WIN_COLS = 16
NA_QROWS = 4
RET_CHUNK = 256
RMS_EPS = 1e-6
NEG_BIG = -1e30

LANE = 128
VMEM_LIMIT = 56 * 1024 * 1024


def _cparams(sem):
    return pltpu.CompilerParams(dimension_semantics=sem, vmem_limit_bytes=VMEM_LIMIT)


def _dot_nt(a, b):
    return lax.dot_general(a, b, (((1,), (1,)), ((), ())), preferred_element_type=F32)


def _dot_nn(a, b):
    return lax.dot_general(a, b, (((1,), (0,)), ((), ())), preferred_element_type=F32)


def _dot_tn(a, b):
    return lax.dot_general(a, b, (((0,), (0,)), ((), ())), preferred_element_type=F32)


def _layer_spec(w, layer, block, index_map):
    if w.ndim == 2:
        return pl.BlockSpec(block, index_map)
    return pl.BlockSpec((None,) + tuple(block), lambda *idx: (layer,) + tuple(index_map(*idx)))


def _rms(x):
    return x * lax.rsqrt(jnp.mean(x * x, axis=-1, keepdims=True) + RMS_EPS)


def _norm_rows_to_scratch(x_ref, g_ref, xn_ref, rows):
    n = x_ref.shape[0] // rows

    def body(r, c):
        sl = pl.ds(pl.multiple_of(r * rows, rows), rows)
        xn_ref[sl, :] = (_rms(x_ref[sl, :]) * g_ref[...]).astype(BF16)
        return c

    lax.fori_loop(0, n, body, 0)


def _norm_matmul_kernel(x_ref, g_ref, w_ref, gc_ref, o_ref, xn_ref, *, lo, hi):
    j = pl.program_id(1)

    @pl.when(j == 0)
    def _():
        _norm_rows_to_scratch(x_ref, g_ref, xn_ref, 128)

    acc = _dot_nn(xn_ref[...], w_ref[...])
    grouped = jnp.logical_and(j >= lo, j < hi)
    for c in range(acc.shape[1] // HEAD_DIM):
        sl = slice(c * HEAD_DIM, (c + 1) * HEAD_DIM)
        a = acc[:, sl]
        inv = lax.rsqrt(jnp.mean(a * a, axis=-1, keepdims=True) + RMS_EPS)
        o_ref[:, sl] = ((a * jnp.where(grouped, inv, 1.0)) * gc_ref[:, sl]).astype(o_ref.dtype)


def norm_matmul(x, g, w, layer, gcol, lo, hi, *, tm, tn, out_dtype=BF16):
    t, d = x.shape
    n = w.shape[-1]
    return pl.pallas_call(
        functools.partial(_norm_matmul_kernel, lo=lo, hi=hi),
        grid=(t // tm, n // tn),
        in_specs=[
            pl.BlockSpec((tm, d), lambda i, j: (i, 0)),
            pl.BlockSpec((1, d), lambda i, j: (0, 0)),
            _layer_spec(w, layer, (d, tn), lambda i, j: (0, j)),
            pl.BlockSpec((1, tn), lambda i, j: (0, j)),
        ],
        out_specs=pl.BlockSpec((tm, tn), lambda i, j: (i, j)),
        out_shape=jax.ShapeDtypeStruct((t, n), out_dtype),
        scratch_shapes=[pltpu.VMEM((tm, d), BF16)],
        compiler_params=_cparams(("parallel", "arbitrary")),
        name="norm_matmul",
    )(x, g, w, gcol)


def _swiglu_in_kernel(x_ref, g_ref, wg_ref, wu_ref, o_ref, xn_ref):
    @pl.when(pl.program_id(1) == 0)
    def _():
        _norm_rows_to_scratch(x_ref, g_ref, xn_ref, 128)

    xn = xn_ref[...]
    gate = _dot_nn(xn, wg_ref[...])
    up = _dot_nn(xn, wu_ref[...])
    o_ref[...] = (gate * jax.nn.sigmoid(gate) * up).astype(o_ref.dtype)


def swiglu_in(x, g, w_in, layer, *, tm, tn):
    t, d = x.shape
    hidden = w_in.shape[-1] // 2
    nh = hidden // tn
    return pl.pallas_call(
        _swiglu_in_kernel,
        grid=(t // tm, nh),
        in_specs=[
            pl.BlockSpec((tm, d), lambda i, j: (i, 0)),
            pl.BlockSpec((1, d), lambda i, j: (0, 0)),
            _layer_spec(w_in, layer, (d, tn), lambda i, j: (0, j)),
            _layer_spec(w_in, layer, (d, tn), lambda i, j: (0, j + nh)),
        ],
        out_specs=pl.BlockSpec((tm, tn), lambda i, j: (i, j)),
        out_shape=jax.ShapeDtypeStruct((t, hidden), BF16),
        scratch_shapes=[pltpu.VMEM((tm, d), BF16)],
        compiler_params=_cparams(("parallel", "arbitrary")),
        name="swiglu_in",
    )(x, g, w_in, w_in)


def _matmul_res_kernel(*refs):
    *a_refs, w_ref, r_ref, o_ref = refs
    acc = r_ref[...]
    k0 = 0
    for a_ref in a_refs:
        k1 = k0 + a_ref.shape[1]
        acc += _dot_nn(a_ref[...], w_ref[k0:k1, :])
        k0 = k1
    o_ref[...] = acc


def matmul_res(a_parts, w, layer, res, *, tm, tn, weight_outer=False):
    t = res.shape[0]
    k, n = w.shape[-2:]
    assert sum(a.shape[1] for a in a_parts) == k
    if weight_outer:
        grid, ij = (n // tn, t // tm), (lambda j, i: (i, j))
    else:
        grid, ij = (t // tm, n // tn), (lambda i, j: (i, j))
    row = lambda *g: (ij(*g)[0], 0)
    col = lambda *g: (0, ij(*g)[1])
    return pl.pallas_call(
        _matmul_res_kernel,
        grid=grid,
        in_specs=[pl.BlockSpec((tm, a.shape[1]), row) for a in a_parts] + [
            _layer_spec(w, layer, (k, tn), col),
            pl.BlockSpec((tm, tn), ij),
        ],
        out_specs=pl.BlockSpec((tm, tn), ij),
        out_shape=jax.ShapeDtypeStruct((t, n), F32),
        compiler_params=_cparams(("arbitrary", "arbitrary")),
        name="matmul_res",
    )(*a_parts, w, res)


def _xattn_kernel(x_ref, g_ref, wq_ref, gq_ref, kv_ref, wo_ref, o_ref):
    x = x_ref[...]
    xn = (_rms(x) * g_ref[...]).astype(BF16)
    q = _dot_nn(xn, wq_ref[...])
    width = N_X_HEADS * HEAD_DIM
    outs = []
    for h in range(N_X_HEADS):
        sl = slice(h * HEAD_DIM, (h + 1) * HEAD_DIM)
        qh = (_rms(q[:, sl]) * gq_ref[:, sl]).astype(BF16)
        s = _dot_nt(qh, kv_ref[:, sl])
        p = jnp.exp(s - jnp.max(s, axis=-1, keepdims=True))
        l = jnp.sum(p, axis=-1, keepdims=True)
        vh = kv_ref[:, width + h * HEAD_DIM: width + (h + 1) * HEAD_DIM]
        oh = jnp.dot(p.astype(BF16), vh, preferred_element_type=F32) / l
        outs.append(oh.astype(BF16))
    o = jnp.concatenate(outs, axis=1)
    o_ref[...] = x + _dot_nn(o, wo_ref[...])


def xattn(x, g, wq, gq, kv, wo, layer, *, tm, seq):
    t, d = x.shape
    n_mem = kv.shape[0] // (t // seq)
    width = wq.shape[-1]
    per_b = seq // tm
    return pl.pallas_call(
        _xattn_kernel,
        grid=(t // tm,),
        in_specs=[
            pl.BlockSpec((tm, d), lambda i: (i, 0)),
            pl.BlockSpec((1, d), lambda i: (0, 0)),
            _layer_spec(wq, layer, (d, width), lambda i: (0, 0)),
            pl.BlockSpec((1, width), lambda i: (0, 0)),
            pl.BlockSpec((n_mem, 2 * width), lambda i: (i // per_b, 0)),
            _layer_spec(wo, layer, (width, d), lambda i: (0, 0)),
        ],
        out_specs=pl.BlockSpec((tm, d), lambda i: (i, 0)),
        out_shape=jax.ShapeDtypeStruct((t, d), F32),
        compiler_params=_cparams(("parallel",)),
        name="xattn",
    )(x, g, wq, gq, kv, wo)


def _ret_state_kernel(k_ref, v_ref, lgb_ref, sb_ref, state_ref, kd_ref):
    c = pl.program_id(1)
    chunk = k_ref.shape[1]
    scale = HEAD_DIM ** -0.5

    @pl.when(c == 0)
    def _():
        state_ref[...] = jnp.zeros_like(state_ref)
        pos = lax.broadcasted_iota(jnp.int32, (chunk, HEAD_DIM), 0).astype(F32)
        for h in range(N_HEADS):
            kd_ref[h] = jnp.exp(lgb_ref[h:h + 1, :] * pos) * scale

    for h in range(N_HEADS):
        sl = slice(h * HEAD_DIM, (h + 1) * HEAD_DIM)
        st = state_ref[h]
        sb_ref[0, 0, h] = st.astype(BF16)
        kd = (k_ref[0, :, sl].astype(F32) * kd_ref[h]).astype(BF16)
        cdec = jnp.exp(lgb_ref[h:h + 1, :] * float(chunk))
        state_ref[h] = st * cdec + _dot_tn(kd, v_ref[0, :, sl])


def ret_states(proj, lgb, *, chunk):
    b, s, _ = proj.shape
    nc = s // chunk
    width = N_HEADS * HEAD_DIM
    return pl.pallas_call(
        _ret_state_kernel,
        grid=(b, nc),
        in_specs=[
            pl.BlockSpec((1, chunk, width), lambda bi, c: (bi, nc - 1 - c, 1)),
            pl.BlockSpec((1, chunk, width), lambda bi, c: (bi, nc - 1 - c, 2)),
            pl.BlockSpec((N_HEADS, LANE), lambda bi, c: (0, 0)),
        ],
        out_specs=pl.BlockSpec((1, 1, N_HEADS, HEAD_DIM, HEAD_DIM), lambda bi, c: (bi, nc - 1 - c, 0, 0, 0)),
        out_shape=jax.ShapeDtypeStruct((b, nc, N_HEADS, HEAD_DIM, HEAD_DIM), BF16),
        scratch_shapes=[pltpu.VMEM((N_HEADS, HEAD_DIM, HEAD_DIM), F32),
                        pltpu.VMEM((N_HEADS, chunk, HEAD_DIM), F32)],
        compiler_params=_cparams(("parallel", "arbitrary")),
        name="ret_states",
    )(proj, proj, lgb)


def _ret_out_kernel(q_ref, k_ref, v_ref, rg_ref, sb_ref, lgf_ref, lgb_ref, og_ref, o_ref,
                    state_ref, dmat_ref, qdf_ref, qdb_ref, kdf_ref):
    c = pl.program_id(1)
    chunk = q_ref.shape[1]
    scale = HEAD_DIM ** -0.5

    @pl.when(c == 0)
    def _():
        state_ref[...] = jnp.zeros_like(state_ref)
        pos = lax.broadcasted_iota(jnp.int32, (chunk, HEAD_DIM), 0).astype(F32)
        ri = lax.broadcasted_iota(jnp.int32, (chunk, chunk), 0)
        ci = lax.broadcasted_iota(jnp.int32, (chunk, chunk), 1)
        diff = (ri - ci).astype(F32)
        for h in range(N_HEADS):
            lf = lgf_ref[h:h + 1, :]
            lb = lgb_ref[h:h + 1, :]
            dfwd = jnp.exp(lf[:, :1] * jnp.maximum(diff, 0.0))
            dbwd = jnp.exp(lb[:, :1] * jnp.maximum(-diff, 0.0))
            dmat_ref[h] = jnp.where(diff >= 0.0, dfwd, dbwd) * scale
            qdf_ref[h] = jnp.exp(lf * (pos + 1.0))
            qdb_ref[h] = jnp.exp(lb * (float(chunk) - pos))
            kdf_ref[h] = jnp.exp(lf * (float(chunk) - 1.0 - pos)) * scale

    for h in range(N_HEADS):
        sl = slice(h * HEAD_DIM, (h + 1) * HEAD_DIM)
        qh = q_ref[0, :, sl]
        kh = k_ref[0, :, sl]
        vh = v_ref[0, :, sl]
        qf32 = qh.astype(F32)
        sd = (_dot_nt(qh, kh) * dmat_ref[h]).astype(BF16)
        st = state_ref[h]
        out = jnp.dot(sd, vh, preferred_element_type=F32)
        out += jnp.dot((qf32 * qdf_ref[h]).astype(BF16), st.astype(BF16), preferred_element_type=F32)
        out += jnp.dot((qf32 * qdb_ref[h]).astype(BF16), sb_ref[0, 0, h], preferred_element_type=F32)
        gate = rg_ref[0, :, sl].astype(F32)
        y = _rms(out) * og_ref[...]
        o_ref[0, :, sl] = (y * (gate * jax.nn.sigmoid(gate))).astype(o_ref.dtype)
        kd = (kh.astype(F32) * kdf_ref[h]).astype(BF16)
        cdec = jnp.exp(lgf_ref[h:h + 1, :] * float(chunk))
        state_ref[h] = st * cdec + _dot_tn(kd, vh)


def ret_out(proj, sb, lgf, lgb, og, *, chunk):
    b, s, _ = proj.shape
    nc = s // chunk
    width = N_HEADS * HEAD_DIM
    col = lambda k: pl.BlockSpec((1, chunk, width), lambda bi, c, k=k: (bi, c, k))
    return pl.pallas_call(
        _ret_out_kernel,
        grid=(b, nc),
        in_specs=[
            col(0), col(1), col(2), col(3),
            pl.BlockSpec((1, 1, N_HEADS, HEAD_DIM, HEAD_DIM), lambda bi, c: (bi, c, 0, 0, 0)),
            pl.BlockSpec((N_HEADS, LANE), lambda bi, c: (0, 0)),
            pl.BlockSpec((N_HEADS, LANE), lambda bi, c: (0, 0)),
            pl.BlockSpec((1, HEAD_DIM), lambda bi, c: (0, 0)),
        ],
        out_specs=pl.BlockSpec((1, chunk, width), lambda bi, c: (bi, c, 0)),
        out_shape=jax.ShapeDtypeStruct((b, s, width), BF16),
        scratch_shapes=[pltpu.VMEM((N_HEADS, HEAD_DIM, HEAD_DIM), F32),
                        pltpu.VMEM((N_HEADS, chunk, chunk), F32),
                        pltpu.VMEM((N_HEADS, chunk, HEAD_DIM), F32),
                        pltpu.VMEM((N_HEADS, chunk, HEAD_DIM), F32),
                        pltpu.VMEM((N_HEADS, chunk, HEAD_DIM), F32)],
        compiler_params=_cparams(("parallel", "arbitrary")),
        name="ret_out",
    )(proj, proj, proj, proj, sb, lgf, lgb, og)


N_DR = 2 * NA_WIN_ROWS - 1
N_DC = 2 * NA_WIN_COLS - 1


def _na_kernel(rpb_ref, q_ref, kp_ref, kc_ref, kn_ref, vp_ref, vc_ref, vn_ref, o_ref,
               bcol_ref, slab_ref, *, rows):
    qb = pl.program_id(1)
    nb = rows // NA_QROWS
    neg_tile = jnp.full((GRID_W, GRID_W), NEG_BIG, F32)

    @pl.when(qb == 0)
    def _():
        qc = lax.broadcasted_iota(jnp.int32, (GRID_W, GRID_W), 0)
        kc = lax.broadcasted_iota(jnp.int32, (GRID_W, GRID_W), 1)
        cs = jnp.clip(qc - NA_WIN_COLS // 2, 0, GRID_W - NA_WIN_COLS)
        col_ok = jnp.logical_and(kc >= cs, kc < cs + NA_WIN_COLS)
        dc = kc - qc + (NA_WIN_COLS - 1)
        for h in range(N_HEADS):
            for dr in range(N_DR):
                tile = neg_tile
                for d in range(N_DC):
                    tile = jnp.where(dc == d, rpb_ref[(h * N_DR + dr) * N_DC + d], tile)
                bcol_ref[h, dr] = jnp.where(col_ok, tile, NEG_BIG)

    def assemble(qb_static):
        for h in range(N_HEADS):
            for j in range(NA_QROWS):
                r = NA_QROWS * qb_static + j
                rs = min(max(r - NA_WIN_ROWS // 2, 0), rows - NA_WIN_ROWS)
                for i2 in range(0, 3 * NA_QROWS, 2):
                    halves = []
                    for i in (i2, i2 + 1):
                        kr = NA_QROWS * (qb_static - 1) + i
                        ok = rs <= kr < rs + NA_WIN_ROWS
                        halves.append(bcol_ref[h, kr - r + NA_WIN_ROWS - 1] if ok else neg_tile)
                    slab_ref[h, j * GRID_W:(j + 1) * GRID_W, i2 * GRID_W:(i2 + 2) * GRID_W] = (
                        jnp.concatenate(halves, axis=1))

    for qb_static in (0, 1, nb - 1):
        pl.when(qb == qb_static)(functools.partial(assemble, qb_static))

    for h in range(N_HEADS):
        sl = slice(h * HEAD_DIM, (h + 1) * HEAD_DIM)
        k = jnp.concatenate([kp_ref[0, :, sl], kc_ref[0, :, sl], kn_ref[0, :, sl]], axis=0)
        v = jnp.concatenate([vp_ref[0, :, sl], vc_ref[0, :, sl], vn_ref[0, :, sl]], axis=0)
        s = _dot_nt(q_ref[0, :, sl], k) + slab_ref[h]
        p = jnp.exp(s - jnp.max(s, axis=-1, keepdims=True))
        l = jnp.sum(p, axis=-1, keepdims=True)
        o = jnp.dot(p.astype(BF16), v, preferred_element_type=F32) / l
        o_ref[0, :, sl] = o.astype(o_ref.dtype)


def neighborhood_attention(proj, rpb):
    b, s, _ = proj.shape
    rows = s // GRID_W
    assert NA_QROWS >= NA_WIN_ROWS // 2 and NA_QROWS % 2 == 0 and rows % NA_QROWS == 0
    assert rows >= 3 * NA_QROWS and rows >= NA_WIN_ROWS
    nq = NA_QROWS * GRID_W
    nb = rows // NA_QROWS
    width = N_HEADS * HEAD_DIM

    def blk(group, shift):
        return pl.BlockSpec((1, nq, width),
                            lambda bi, qb, rp: (bi, jnp.clip(qb + shift, 0, nb - 1), group))

    grid_spec = pltpu.PrefetchScalarGridSpec(
        num_scalar_prefetch=1,
        grid=(b, nb),
        in_specs=[blk(4, 0), blk(5, -1), blk(5, 0), blk(5, 1), blk(6, -1), blk(6, 0), blk(6, 1)],
        out_specs=pl.BlockSpec((1, nq, width), lambda bi, qb, rp: (bi, qb, 0)),
        scratch_shapes=[pltpu.VMEM((N_HEADS, N_DR, GRID_W, GRID_W), F32),
                        pltpu.VMEM((N_HEADS, nq, 3 * nq), F32)],
    )
    return pl.pallas_call(
        functools.partial(_na_kernel, rows=rows),
        grid_spec=grid_spec,
        out_shape=jax.ShapeDtypeStruct((b, s, width), BF16),
        compiler_params=_cparams(("arbitrary", "arbitrary")),
        name="neighborhood_attention",
    )(rpb.astype(F32).reshape(-1), proj, proj, proj, proj, proj, proj, proj)


def _bf16_pieces(x, n):
    out, rest = [], float(x)
    for _ in range(n):
        piece = float(np.float32(rest).astype(BF16).astype(np.float32))
        out.append(piece)
        rest -= piece
    return out


LOG2E = math.log2(math.e)
LOG2E_PIECES = _bf16_pieces(LOG2E, 3)
N_ALIBI_COLS = 12
POS_LO_BITS = 7


def _alibi_cols(pos, slope, lane, unit_cols, sign):
    hi = (pos >> POS_LO_BITS).astype(F32) * (slope * float(1 << POS_LO_BITS))
    lo = (pos & ((1 << POS_LO_BITS) - 1)).astype(F32) * slope
    if sign > 0:
        return jnp.where(lane < 3, hi, jnp.where(lane < 6, lo, unit_cols))
    return jnp.where(lane < 6, unit_cols, jnp.where(lane < 9, hi, jnp.where(lane < N_ALIBI_COLS, lo, 0.0)))


def _diff_attn_kernel(slopes_ref, q_ref, k_ref, v_ref, cq_ref, ck_ref, lam_ref, og_ref, o_ref,
                      kaug_ref, qaug_ref, s_ref, bias_ref, acc_ref, *, lam_init, nsub, npv):
    h = pl.program_id(1)
    qi = pl.program_id(2)
    t = q_ref.shape[1]
    nk = k_ref.shape[1] // t
    nslab = t // LANE
    slope = slopes_ref[h]
    lane = lax.broadcasted_iota(jnp.int32, (t, LANE), 1)
    row = lax.broadcasted_iota(jnp.int32, (t, LANE), 0)

    @pl.when(qi == 0)
    def _():
        def body(r, carry):
            rs = pl.multiple_of(r * t, t)
            kx = _alibi_cols(row + rs, slope, lane, ck_ref[...], 1).astype(BF16)
            for c in range(2):
                kaug_ref[c, pl.ds(rs, t), 0:HEAD_DIM] = k_ref[0, pl.ds(rs, t), c * HEAD_DIM:(c + 1) * HEAD_DIM]
                kaug_ref[c, pl.ds(rs, t), HEAD_DIM:2 * HEAD_DIM] = kx
            return carry

        lax.fori_loop(0, nk, body, 0)
        ri = lax.broadcasted_iota(jnp.int32, (t, t), 0)
        ci = lax.broadcasted_iota(jnp.int32, (t, t), 1)
        bias_ref[0] = jnp.zeros((t, t), F32)
        bias_ref[1] = jnp.abs(ri - ci).astype(F32) * (-LOG2E * slope)

    qx = _alibi_cols(row + qi * t, slope, lane, cq_ref[...], -1)
    for c in range(2):
        qc = q_ref[0, :, c * HEAD_DIM:(c + 1) * HEAD_DIM]
        for side, ext in ((0, qx), (1, -qx), (2, jnp.zeros_like(qx))):
            qaug_ref[c, side, :, 0:HEAD_DIM] = qc
            qaug_ref[c, side, :, HEAD_DIM:2 * HEAD_DIM] = ext.astype(BF16)

    width = npv * t
    ngroups = nk // npv
    outs = []
    for c in range(2):
        def scores(g, m):
            for i in range(nsub):
                kb = g * nsub + i
                side = jnp.where(kb < qi, 0, jnp.where(kb > qi, 1, 2))
                ks = pl.multiple_of(kb * t, t)
                s = _dot_nt(qaug_ref[c, side], kaug_ref[c, pl.ds(ks, t), :])
                s = s + bias_ref[(kb == qi).astype(jnp.int32)]
                s_ref[kb] = s
                for u in range(nslab):
                    m = jnp.maximum(m, s[:, u * LANE:(u + 1) * LANE])
            return m

        m = lax.fori_loop(0, nk // nsub, scores, jnp.full((t, LANE), NEG_BIG, F32))
        mb = jnp.broadcast_to(jnp.max(m, axis=-1, keepdims=True), (t, LANE))
        acc_ref[...] = jnp.zeros_like(acc_ref)

        def pv(g, lrun):
            ps = []
            for i in range(npv):
                for u in range(nslab):
                    p = jnp.exp2(s_ref[g * npv + i, :, u * LANE:(u + 1) * LANE] - mb)
                    lrun = lrun + p
                    ps.append(p.astype(BF16))
            vs = v_ref[0, pl.ds(pl.multiple_of(g * width, width), width), :]
            acc_ref[...] += jnp.dot(jnp.concatenate(ps, axis=1), vs, preferred_element_type=F32)
            return lrun

        lrun = lax.fori_loop(0, ngroups, pv, jnp.zeros((t, LANE), F32))
        outs.append(acc_ref[...] / jnp.sum(lrun, axis=-1, keepdims=True))

    lam = (jnp.exp(jnp.sum(lam_ref[0:1, :] * lam_ref[1:2, :], axis=-1, keepdims=True))
           - jnp.exp(jnp.sum(lam_ref[2:3, :] * lam_ref[3:4, :], axis=-1, keepdims=True)) + lam_init)
    o = outs[0] - lam * outs[1]
    o_ref[0] = (_rms(o) * og_ref[...] * (1.0 - lam_init)).astype(o_ref.dtype)


def diff_attention(proj, lam_params, og, *, lam_init, t):
    b, s, _ = proj.shape
    assert s % t == 0 and s <= (1 << (2 * POS_LO_BITS + 1)) and t % LANE == 0
    nk = s // t
    nsub = max(n for n in (16, 8, 4, 3, 2, 1) if nk % n == 0)
    npv = max(n for n in (8, 4, 3, 2, 1) if nk % n == 0)
    vw = 2 * HEAD_DIM
    slopes = jnp.asarray(2.0 ** (-8.0 * np.arange(1, N_HEADS + 1) / N_HEADS), F32)
    pieces = np.zeros((1, LANE), np.float32)
    pieces[0, :6] = LOG2E_PIECES * 2
    cq = jnp.asarray(pieces)
    ck = jnp.asarray(-np.roll(pieces, 6, axis=1))
    const = lambda shape: pl.BlockSpec(shape, lambda bi, h, qi, sl: (0, 0))
    grid_spec = pltpu.PrefetchScalarGridSpec(
        num_scalar_prefetch=1,
        grid=(b, N_HEADS, nk),
        in_specs=[
            pl.BlockSpec((1, t, vw), lambda bi, h, qi, sl: (bi, qi, h)),
            pl.BlockSpec((1, s, vw), lambda bi, h, qi, sl: (bi, 0, N_HEADS + h)),
            pl.BlockSpec((1, s, vw), lambda bi, h, qi, sl: (bi, 0, 2 * N_HEADS + h)),
            const((1, LANE)), const((1, LANE)), const((4, HEAD_DIM)), const((1, vw)),
        ],
        out_specs=pl.BlockSpec((1, t, vw), lambda bi, h, qi, sl: (bi, qi, h)),
        scratch_shapes=[pltpu.VMEM((2, s, vw), BF16),
                        pltpu.VMEM((2, 3, t, vw), BF16),
                        pltpu.VMEM((nk, t, t), F32),
                        pltpu.VMEM((2, t, t), F32),
                        pltpu.VMEM((t, vw), F32)],
    )
    return pl.pallas_call(
        functools.partial(_diff_attn_kernel, lam_init=lam_init, nsub=nsub, npv=npv),
        grid_spec=grid_spec,
        out_shape=jax.ShapeDtypeStruct((b, s, N_HEADS * vw), BF16),
        compiler_params=_cparams(("arbitrary", "arbitrary", "arbitrary")),
        name="diff_attention",
    )(slopes, proj, proj, proj, cq, ck, lam_params, og)


def _row(v):
    return v.astype(F32).reshape(1, -1)


def _lane_bcast(v):
    return jnp.broadcast_to(v.astype(F32)[:, None], (v.shape[0], LANE))


def kernel(x, mem, norm_mix_g, norm_xattn_g, norm_mem_g, norm_ffn_g, w_in_ab, ret_decay_fwd, ret_decay_bwd, ret_out_g, na_q_g, na_k_g, na_rpb, w_out_ab, w_in_c, diff_q_g, diff_k_g, lambda_q1, lambda_k1, lambda_q2, lambda_k2, diff_out_g, w_out_c, w_xq, w_xkv, w_xo, xq_g, xk_g, w_ffn_in, w_ffn_out):
    b, s, d = x.shape
    n_mem = mem.shape[1]
    depth = norm_mix_g.shape[0]
    t = b * s
    scale = HEAD_DIM ** -0.5
    width = N_HEADS * HEAD_DIM
    xw = N_X_HEADS * HEAD_DIM
    tm, tn = 1024, 512
    tn_in = 1024

    xf = x.reshape(t, d)
    memf = mem.reshape(b * n_mem, d)
    ones_cols = lambda n: jnp.ones((1, n), F32)
    w_ffn_out_bf16 = w_ffn_out.astype(BF16)

    for i in range(depth):
        j = i // 2
        if i % 2 == 0:
            gcol = jnp.concatenate([
                ones_cols(4 * width),
                jnp.tile(_row(na_q_g[j]) * scale, (1, N_HEADS)),
                jnp.tile(_row(na_k_g[j]), (1, N_HEADS)),
                ones_cols(width)], axis=1)
            proj = norm_matmul(xf, _row(norm_mix_g[i]), w_in_ab, j, gcol,
                               4 * width // tn_in, 6 * width // tn_in, tm=tm, tn=tn_in)
            proj = proj.reshape(b, s, 7 * width)
            lgf = _lane_bcast(jax.nn.log_sigmoid(ret_decay_fwd[j].astype(F32)))
            lgb = _lane_bcast(jax.nn.log_sigmoid(ret_decay_bwd[j].astype(F32)))
            sb = ret_states(proj, lgb, chunk=RET_CHUNK)
            ret = ret_out(proj, sb, lgf, lgb, _row(ret_out_g[j]), chunk=RET_CHUNK)
            na = neighborhood_attention(proj, na_rpb[j])
            mixed = [ret.reshape(t, width), na.reshape(t, width)]
            xf = matmul_res(mixed, w_out_ab, j, xf, tm=tm, tn=tn_in, weight_outer=True)
        else:
            cw = 2 * width
            gcol = jnp.concatenate([
                jnp.tile(_row(diff_q_g[j]) * (scale * LOG2E), (1, 2 * N_HEADS)),
                jnp.tile(_row(diff_k_g[j]), (1, 2 * N_HEADS)),
                ones_cols(cw)], axis=1)
            proj = norm_matmul(xf, _row(norm_mix_g[i]), w_in_c, j, gcol, 0, 2 * cw // tn_in, tm=tm, tn=tn_in)
            proj = proj.reshape(b, s, 3 * cw)
            lam_params = jnp.stack([lambda_q1[j], lambda_k1[j], lambda_q2[j], lambda_k2[j]]).astype(F32)
            lam_init = 0.8 - 0.6 * math.exp(-0.3 * i)
            att = diff_attention(proj, lam_params, _row(diff_out_g[j]), lam_init=lam_init, t=512)
            xf = matmul_res([att.reshape(t, cw)], w_out_c, j, xf, tm=tm, tn=tn_in, weight_outer=True)

        kv_gcol = jnp.concatenate([jnp.tile(_row(xk_g[i]), (1, N_X_HEADS)), ones_cols(xw)], axis=1)
        kv = norm_matmul(memf, _row(norm_mem_g[i]), w_xkv, i, kv_gcol, 0, 1, tm=b * n_mem, tn=xw)
        gq = jnp.tile(_row(xq_g[i]) * scale, (1, N_X_HEADS))
        xf = xattn(xf, _row(norm_xattn_g[i]), w_xq, gq, kv, w_xo, i, tm=512, seq=s)

        hmid = swiglu_in(xf, _row(norm_ffn_g[i]), w_ffn_in, i, tm=tm, tn=tn)
        xf = matmul_res([hmid], w_ffn_out_bf16, i, xf, tm=tm, tn=tn)

    return xf.reshape(b, s, d)
```

```python
import functools
import math

import jax
import jax.numpy as jnp
import numpy as np
from jax import lax
from jax.experimental import pallas as pl
from jax.experimental.pallas import tpu as pltpu

F32 = jnp.float32
BF16 = jnp.bfloat16

HEAD_DIM = 128
N_HEADS = 8
N_X_HEADS = 4
GRID_W = 64
NA_WIN_ROWS = 8
NA_WIN_COLS = 16
NA_QROWS = 4
RET_CHUNK = 256
RMS_EPS = 1e-6
NEG_BIG = -1e30

LANE = 128
VMEM_LIMIT = 56 * 1024 * 1024


def _cparams(sem):
    return pltpu.CompilerParams(dimension_semantics=sem, vmem_limit_bytes=VMEM_LIMIT)


def _dot_nt(a, b):
    return lax.dot_general(a, b, (((1,), (1,)), ((), ())), preferred_element_type=F32)


def _dot_nn(a, b):
    return lax.dot_general(a, b, (((1,), (0,)), ((), ())), preferred_element_type=F32)


def _dot_tn(a, b):
    return lax.dot_general(a, b, (((0,), (0,)), ((), ())), preferred_element_type=F32)


def _layer_spec(w, layer, block, index_map):
    if w.ndim == 2:
        return pl.BlockSpec(block, index_map)
    return pl.BlockSpec((None,) + tuple(block), lambda *idx: (layer,) + tuple(index_map(*idx)))


def _rms(x):
    return x * lax.rsqrt(jnp.mean(x * x, axis=-1, keepdims=True) + RMS_EPS)


def _norm_rows_to_scratch(x_ref, g_ref, xn_ref, rows):
    n = x_ref.shape[0] // rows

    def body(r, c):
        sl = pl.ds(pl.multiple_of(r * rows, rows), rows)
        xn_ref[sl, :] = (_rms(x_ref[sl, :]) * g_ref[...]).astype(BF16)
        return c

    lax.fori_loop(0, n, body, 0)


def _norm_matmul_kernel(x_ref, g_ref, w_ref, gc_ref, o_ref, xn_ref, *, lo, hi):
    j = pl.program_id(1)

    @pl.when(j == 0)
    def _():
        _norm_rows_to_scratch(x_ref, g_ref, xn_ref, 128)

    acc = _dot_nn(xn_ref[...], w_ref[...])
    grouped = jnp.logical_and(j >= lo, j < hi)
    for c in range(acc.shape[1] // HEAD_DIM):
        sl = slice(c * HEAD_DIM, (c + 1) * HEAD_DIM)
        a = acc[:, sl]
        inv = lax.rsqrt(jnp.mean(a * a, axis=-1, keepdims=True) + RMS_EPS)
        o_ref[:, sl] = ((a * jnp.where(grouped, inv, 1.0)) * gc_ref[:, sl]).astype(o_ref.dtype)


def norm_matmul(x, g, w, layer, gcol, lo, hi, *, tm, tn, out_dtype=BF16):
    t, d = x.shape
    n = w.shape[-1]
    return pl.pallas_call(
        functools.partial(_norm_matmul_kernel, lo=lo, hi=hi),
        grid=(t // tm, n // tn),
        in_specs=[
            pl.BlockSpec((tm, d), lambda i, j: (i, 0)),
            pl.BlockSpec((1, d), lambda i, j: (0, 0)),
            _layer_spec(w, layer, (d, tn), lambda i, j: (0, j)),
            pl.BlockSpec((1, tn), lambda i, j: (0, j)),
        ],
        out_specs=pl.BlockSpec((tm, tn), lambda i, j: (i, j)),
        out_shape=jax.ShapeDtypeStruct((t, n), out_dtype),
        scratch_shapes=[pltpu.VMEM((tm, d), BF16)],
        compiler_params=_cparams(("parallel", "arbitrary")),
        name="norm_matmul",
    )(x, g, w, gcol)


def _swiglu_in_kernel(xn_ref, wg_ref, wu_ref, o_ref):
    xn = xn_ref[...]
    gate = _dot_nn(xn, wg_ref[...])
    up = _dot_nn(xn, wu_ref[...])
    o_ref[...] = (gate * jax.nn.sigmoid(gate) * up).astype(o_ref.dtype)


def swiglu_in(xn, w_in, layer, *, tm, tn):
    t, d = xn.shape
    hidden = w_in.shape[-1] // 2
    nh = hidden // tn
    return pl.pallas_call(
        _swiglu_in_kernel,
        grid=(t // tm, nh),
        in_specs=[
            pl.BlockSpec((tm, d), lambda i, j: (i, 0)),
            _layer_spec(w_in, layer, (d, tn), lambda i, j: (0, j)),
            _layer_spec(w_in, layer, (d, tn), lambda i, j: (0, j + nh)),
        ],
        out_specs=pl.BlockSpec((tm, tn), lambda i, j: (i, j)),
        out_shape=jax.ShapeDtypeStruct((t, hidden), BF16),
        compiler_params=_cparams(("parallel", "arbitrary")),
        name="swiglu_in",
    )(xn, w_in, w_in)


def _matmul_res_kernel(*refs):
    *a_refs, w_ref, r_ref, o_ref = refs
    acc = r_ref[...]
    k0 = 0
    for a_ref in a_refs:
        k1 = k0 + a_ref.shape[1]
        acc += _dot_nn(a_ref[...], w_ref[k0:k1, :])
        k0 = k1
    o_ref[...] = acc


def matmul_res(a_parts, w, layer, res, *, tm, tn, weight_outer=False):
    t = res.shape[0]
    k, n = w.shape[-2:]
    assert sum(a.shape[1] for a in a_parts) == k
    if weight_outer:
        grid, ij = (n // tn, t // tm), (lambda j, i: (i, j))
    else:
        grid, ij = (t // tm, n // tn), (lambda i, j: (i, j))
    row = lambda *g: (ij(*g)[0], 0)
    col = lambda *g: (0, ij(*g)[1])
    return pl.pallas_call(
        _matmul_res_kernel,
        grid=grid,
        in_specs=[pl.BlockSpec((tm, a.shape[1]), row) for a in a_parts] + [
            _layer_spec(w, layer, (k, tn), col),
            pl.BlockSpec((tm, tn), ij),
        ],
        out_specs=pl.BlockSpec((tm, tn), ij),
        out_shape=jax.ShapeDtypeStruct((t, n), F32),
        compiler_params=_cparams(("arbitrary", "arbitrary")),
        name="matmul_res",
    )(*a_parts, w, res)


def _xattn_kernel(x_ref, g_ref, wq_ref, gq_ref, kv_ref, wo_ref, gn_ref, o_ref, on_ref):
    x = x_ref[...]
    xn = (_rms(x) * g_ref[...]).astype(BF16)
    q = _dot_nn(xn, wq_ref[...])
    width = N_X_HEADS * HEAD_DIM
    outs = []
    for h in range(N_X_HEADS):
        sl = slice(h * HEAD_DIM, (h + 1) * HEAD_DIM)
        qh = (_rms(q[:, sl]) * gq_ref[:, sl]).astype(BF16)
        s = _dot_nt(qh, kv_ref[:, sl])
        p = jnp.exp(s - jnp.max(s, axis=-1, keepdims=True))
        l = jnp.sum(p, axis=-1, keepdims=True)
        vh = kv_ref[:, width + h * HEAD_DIM: width + (h + 1) * HEAD_DIM]
        oh = jnp.dot(p.astype(BF16), vh, preferred_element_type=F32) / l
        outs.append(oh.astype(BF16))
    o = jnp.concatenate(outs, axis=1)
    y = x + _dot_nn(o, wo_ref[...])
    o_ref[...] = y
    on_ref[...] = (_rms(y) * gn_ref[...]).astype(on_ref.dtype)


def xattn(x, g, wq, gq, kv, wo, layer, g_next, *, tm, seq):
    t, d = x.shape
    n_mem = kv.shape[0] // (t // seq)
    width = wq.shape[-1]
    per_b = seq // tm
    return pl.pallas_call(
        _xattn_kernel,
        grid=(t // tm,),
        in_specs=[
            pl.BlockSpec((tm, d), lambda i: (i, 0)),
            pl.BlockSpec((1, d), lambda i: (0, 0)),
            _layer_spec(wq, layer, (d, width), lambda i: (0, 0)),
            pl.BlockSpec((1, width), lambda i: (0, 0)),
            pl.BlockSpec((n_mem, 2 * width), lambda i: (i // per_b, 0)),
            _layer_spec(wo, layer, (width, d), lambda i: (0, 0)),
            pl.BlockSpec((1, d), lambda i: (0, 0)),
        ],
        out_specs=[pl.BlockSpec((tm, d), lambda i: (i, 0)), pl.BlockSpec((tm, d), lambda i: (i, 0))],
        out_shape=[jax.ShapeDtypeStruct((t, d), F32), jax.ShapeDtypeStruct((t, d), BF16)],
        compiler_params=_cparams(("parallel",)),
        name="xattn",
    )(x, g, wq, gq, kv, wo, g_next)


def _ret_state_kernel(k_ref, v_ref, lgb_ref, sb_ref, state_ref, kd_ref):
    c = pl.program_id(1)
    chunk = k_ref.shape[1]
    scale = HEAD_DIM ** -0.5

    @pl.when(c == 0)
    def _():
        state_ref[...] = jnp.zeros_like(state_ref)
        pos = lax.broadcasted_iota(jnp.int32, (chunk, HEAD_DIM), 0).astype(F32)
        for h in range(N_HEADS):
            kd_ref[h] = jnp.exp(lgb_ref[h:h + 1, :] * pos) * scale

    for h in range(N_HEADS):
        sl = slice(h * HEAD_DIM, (h + 1) * HEAD_DIM)
        st = state_ref[h]
        sb_ref[0, 0, h] = st.astype(BF16)
        kd = (k_ref[0, :, sl].astype(F32) * kd_ref[h]).astype(BF16)
        cdec = jnp.exp(lgb_ref[h:h + 1, :] * float(chunk))
        state_ref[h] = st * cdec + _dot_tn(kd, v_ref[0, :, sl])


def ret_states(proj, lgb, *, chunk):
    b, s, _ = proj.shape
    nc = s // chunk
    width = N_HEADS * HEAD_DIM
    return pl.pallas_call(
        _ret_state_kernel,
        grid=(b, nc),
        in_specs=[
            pl.BlockSpec((1, chunk, width), lambda bi, c: (bi, nc - 1 - c, 1)),
            pl.BlockSpec((1, chunk, width), lambda bi, c: (bi, nc - 1 - c, 2)),
            pl.BlockSpec((N_HEADS, LANE), lambda bi, c: (0, 0)),
        ],
        out_specs=pl.BlockSpec((1, 1, N_HEADS, HEAD_DIM, HEAD_DIM), lambda bi, c: (bi, nc - 1 - c, 0, 0, 0)),
        out_shape=jax.ShapeDtypeStruct((b, nc, N_HEADS, HEAD_DIM, HEAD_DIM), BF16),
        scratch_shapes=[pltpu.VMEM((N_HEADS, HEAD_DIM, HEAD_DIM), F32),
                        pltpu.VMEM((N_HEADS, chunk, HEAD_DIM), F32)],
        compiler_params=_cparams(("parallel", "arbitrary")),
        name="ret_states",
    )(proj, proj, lgb)


def _ret_out_kernel(q_ref, k_ref, v_ref, rg_ref, sb_ref, lgf_ref, lgb_ref, og_ref, o_ref,
                    state_ref, dmat_ref, qdf_ref, qdb_ref, kdf_ref):
    c = pl.program_id(1)
    chunk = q_ref.shape[1]
    scale = HEAD_DIM ** -0.5

    @pl.when(c == 0)
    def _():
        state_ref[...] = jnp.zeros_like(state_ref)
        pos = lax.broadcasted_iota(jnp.int32, (chunk, HEAD_DIM), 0).astype(F32)
        ri = lax.broadcasted_iota(jnp.int32, (chunk, chunk), 0)
        ci = lax.broadcasted_iota(jnp.int32, (chunk, chunk), 1)
        diff = (ri - ci).astype(F32)
        for h in range(N_HEADS):
            lf = lgf_ref[h:h + 1, :]
            lb = lgb_ref[h:h + 1, :]
            dfwd = jnp.exp(lf[:, :1] * jnp.maximum(diff, 0.0))
            dbwd = jnp.exp(lb[:, :1] * jnp.maximum(-diff, 0.0))
            dmat_ref[h] = jnp.where(diff >= 0.0, dfwd, dbwd) * scale
            qdf_ref[h] = jnp.exp(lf * (pos + 1.0))
            qdb_ref[h] = jnp.exp(lb * (float(chunk) - pos))
            kdf_ref[h] = jnp.exp(lf * (float(chunk) - 1.0 - pos)) * scale

    for h in range(N_HEADS):
        sl = slice(h * HEAD_DIM, (h + 1) * HEAD_DIM)
        qh = q_ref[0, :, sl]
        kh = k_ref[0, :, sl]
        vh = v_ref[0, :, sl]
        qf32 = qh.astype(F32)
        sd = (_dot_nt(qh, kh) * dmat_ref[h]).astype(BF16)
        st = state_ref[h]
        out = jnp.dot(sd, vh, preferred_element_type=F32)
        out += jnp.dot((qf32 * qdf_ref[h]).astype(BF16), st.astype(BF16), preferred_element_type=F32)
        out += jnp.dot((qf32 * qdb_ref[h]).astype(BF16), sb_ref[0, 0, h], preferred_element_type=F32)
        gate = rg_ref[0, :, sl].astype(F32)
        y = _rms(out) * og_ref[...]
        o_ref[0, :, sl] = (y * (gate * jax.nn.sigmoid(gate))).astype(o_ref.dtype)
        kd = (kh.astype(F32) * kdf_ref[h]).astype(BF16)
        cdec = jnp.exp(lgf_ref[h:h + 1, :] * float(chunk))
        state_ref[h] = st * cdec + _dot_tn(kd, vh)


def ret_out(proj, sb, lgf, lgb, og, *, chunk):
    b, s, _ = proj.shape
    nc = s // chunk
    width = N_HEADS * HEAD_DIM
    col = lambda k: pl.BlockSpec((1, chunk, width), lambda bi, c, k=k: (bi, c, k))
    return pl.pallas_call(
        _ret_out_kernel,
        grid=(b, nc),
        in_specs=[
            col(0), col(1), col(2), col(3),
            pl.BlockSpec((1, 1, N_HEADS, HEAD_DIM, HEAD_DIM), lambda bi, c: (bi, c, 0, 0, 0)),
            pl.BlockSpec((N_HEADS, LANE), lambda bi, c: (0, 0)),
            pl.BlockSpec((N_HEADS, LANE), lambda bi, c: (0, 0)),
            pl.BlockSpec((1, HEAD_DIM), lambda bi, c: (0, 0)),
        ],
        out_specs=pl.BlockSpec((1, chunk, width), lambda bi, c: (bi, c, 0)),
        out_shape=jax.ShapeDtypeStruct((b, s, width), BF16),
        scratch_shapes=[pltpu.VMEM((N_HEADS, HEAD_DIM, HEAD_DIM), F32),
                        pltpu.VMEM((N_HEADS, chunk, chunk), F32),
                        pltpu.VMEM((N_HEADS, chunk, HEAD_DIM), F32),
                        pltpu.VMEM((N_HEADS, chunk, HEAD_DIM), F32),
                        pltpu.VMEM((N_HEADS, chunk, HEAD_DIM), F32)],
        compiler_params=_cparams(("parallel", "arbitrary")),
        name="ret_out",
    )(proj, proj, proj, proj, sb, lgf, lgb, og)


N_DR = 2 * NA_WIN_ROWS - 1
N_DC = 2 * NA_WIN_COLS - 1


def _na_kernel(rpb_ref, q_ref, kp_ref, kc_ref, kn_ref, vp_ref, vc_ref, vn_ref, o_ref,
               bcol_ref, slab_ref, *, rows):
    qb = pl.program_id(1)
    nb = rows // NA_QROWS
    neg_tile = jnp.full((GRID_W, GRID_W), NEG_BIG, F32)

    @pl.when(qb == 0)
    def _():
        qc = lax.broadcasted_iota(jnp.int32, (GRID_W, GRID_W), 0)
        kc = lax.broadcasted_iota(jnp.int32, (GRID_W, GRID_W), 1)
        cs = jnp.clip(qc - NA_WIN_COLS // 2, 0, GRID_W - NA_WIN_COLS)
        col_ok = jnp.logical_and(kc >= cs, kc < cs + NA_WIN_COLS)
        dc = kc - qc + (NA_WIN_COLS - 1)
        for h in range(N_HEADS):
            for dr in range(N_DR):
                tile = neg_tile
                for d in range(N_DC):
                    tile = jnp.where(dc == d, rpb_ref[(h * N_DR + dr) * N_DC + d], tile)
                bcol_ref[h, dr] = jnp.where(col_ok, tile, NEG_BIG)

    def assemble(qb_static):
        for h in range(N_HEADS):
            for j in range(NA_QROWS):
                r = NA_QROWS * qb_static + j
                rs = min(max(r - NA_WIN_ROWS // 2, 0), rows - NA_WIN_ROWS)
                for i2 in range(0, 3 * NA_QROWS, 2):
                    halves = []
                    for i in (i2, i2 + 1):
                        kr = NA_QROWS * (qb_static - 1) + i
                        ok = rs <= kr < rs + NA_WIN_ROWS
                        halves.append(bcol_ref[h, kr - r + NA_WIN_ROWS - 1] if ok else neg_tile)
                    slab_ref[h, j * GRID_W:(j + 1) * GRID_W, i2 * GRID_W:(i2 + 2) * GRID_W] = (
                        jnp.concatenate(halves, axis=1))

    for qb_static in (0, 1, nb - 1):
        pl.when(qb == qb_static)(functools.partial(assemble, qb_static))

    for h in range(N_HEADS):
        sl = slice(h * HEAD_DIM, (h + 1) * HEAD_DIM)
        k = jnp.concatenate([kp_ref[0, :, sl], kc_ref[0, :, sl], kn_ref[0, :, sl]], axis=0)
        v = jnp.concatenate([vp_ref[0, :, sl], vc_ref[0, :, sl], vn_ref[0, :, sl]], axis=0)
        s = _dot_nt(q_ref[0, :, sl], k) + slab_ref[h]
        p = jnp.exp(s - jnp.max(s, axis=-1, keepdims=True))
        l = jnp.sum(p, axis=-1, keepdims=True)
        o = jnp.dot(p.astype(BF16), v, preferred_element_type=F32) / l
        o_ref[0, :, sl] = o.astype(o_ref.dtype)


def neighborhood_attention(proj, rpb):
    b, s, _ = proj.shape
    rows = s // GRID_W
    assert NA_QROWS >= NA_WIN_ROWS // 2 and NA_QROWS % 2 == 0 and rows % NA_QROWS == 0
    assert rows >= 3 * NA_QROWS and rows >= NA_WIN_ROWS
    nq = NA_QROWS * GRID_W
    nb = rows // NA_QROWS
    width = N_HEADS * HEAD_DIM

    def blk(group, shift):
        return pl.BlockSpec((1, nq, width),
                            lambda bi, qb, rp: (bi, jnp.clip(qb + shift, 0, nb - 1), group))

    grid_spec = pltpu.PrefetchScalarGridSpec(
        num_scalar_prefetch=1,
        grid=(b, nb),
        in_specs=[blk(4, 0), blk(5, -1), blk(5, 0), blk(5, 1), blk(6, -1), blk(6, 0), blk(6, 1)],
        out_specs=pl.BlockSpec((1, nq, width), lambda bi, qb, rp: (bi, qb, 0)),
        scratch_shapes=[pltpu.VMEM((N_HEADS, N_DR, GRID_W, GRID_W), F32),
                        pltpu.VMEM((N_HEADS, nq, 3 * nq), F32)],
    )
    return pl.pallas_call(
        functools.partial(_na_kernel, rows=rows),
        grid_spec=grid_spec,
        out_shape=jax.ShapeDtypeStruct((b, s, width), BF16),
        compiler_params=_cparams(("arbitrary", "arbitrary")),
        name="neighborhood_attention",
    )(rpb.astype(F32).reshape(-1), proj, proj, proj, proj, proj, proj, proj)


def _bf16_pieces(x, n):
    out, rest = [], float(x)
    for _ in range(n):
        piece = float(np.float32(rest).astype(BF16).astype(np.float32))
        out.append(piece)
        rest -= piece
    return out


LOG2E = math.log2(math.e)
LOG2E_PIECES = _bf16_pieces(LOG2E, 3)
N_ALIBI_COLS = 12
POS_LO_BITS = 7


def _alibi_cols(pos, slope, lane, unit_cols, sign):
    hi = (pos >> POS_LO_BITS).astype(F32) * (slope * float(1 << POS_LO_BITS))
    lo = (pos & ((1 << POS_LO_BITS) - 1)).astype(F32) * slope
    if sign > 0:
        return jnp.where(lane < 3, hi, jnp.where(lane < 6, lo, unit_cols))
    return jnp.where(lane < 6, unit_cols, jnp.where(lane < 9, hi, jnp.where(lane < N_ALIBI_COLS, lo, 0.0)))


def _diff_attn_kernel(slopes_ref, q_ref, k_ref, v_ref, cq_ref, ck_ref, lam_ref, og_ref, o_ref,
                      kaug_ref, qaug_ref, s_ref, bias_ref, acc_ref, *, lam_init, nsub, npv):
    h = pl.program_id(1)
    qi = pl.program_id(2)
    t = q_ref.shape[1]
    nk = k_ref.shape[1] // t
    nslab = t // LANE
    slope = slopes_ref[h]
    lane = lax.broadcasted_iota(jnp.int32, (t, LANE), 1)
    row = lax.broadcasted_iota(jnp.int32, (t, LANE), 0)

    @pl.when(qi == 0)
    def _():
        def body(r, carry):
            rs = pl.multiple_of(r * t, t)
            kx = _alibi_cols(row + rs, slope, lane, ck_ref[...], 1).astype(BF16)
            for c in range(2):
                kaug_ref[c, pl.ds(rs, t), 0:HEAD_DIM] = k_ref[0, pl.ds(rs, t), c * HEAD_DIM:(c + 1) * HEAD_DIM]
                kaug_ref[c, pl.ds(rs, t), HEAD_DIM:2 * HEAD_DIM] = kx
            return carry

        lax.fori_loop(0, nk, body, 0)
        ri = lax.broadcasted_iota(jnp.int32, (t, t), 0)
        ci = lax.broadcasted_iota(jnp.int32, (t, t), 1)
        bias_ref[0] = jnp.zeros((t, t), F32)
        bias_ref[1] = jnp.abs(ri - ci).astype(F32) * (-LOG2E * slope)

    qx = _alibi_cols(row + qi * t, slope, lane, cq_ref[...], -1)
    for c in range(2):
        qc = q_ref[0, :, c * HEAD_DIM:(c + 1) * HEAD_DIM]
        for side, ext in ((0, qx), (1, -qx), (2, jnp.zeros_like(qx))):
            qaug_ref[c, side, :, 0:HEAD_DIM] = qc
            qaug_ref[c, side, :, HEAD_DIM:2 * HEAD_DIM] = ext.astype(BF16)

    width = npv * t
    ngroups = nk // npv
    outs = []
    for c in range(2):
        def scores(g, m):
            for i in range(nsub):
                kb = g * nsub + i
                side = jnp.where(kb < qi, 0, jnp.where(kb > qi, 1, 2))
                ks = pl.multiple_of(kb * t, t)
                s = _dot_nt(qaug_ref[c, side], kaug_ref[c, pl.ds(ks, t), :])
                s = s + bias_ref[(kb == qi).astype(jnp.int32)]
                s_ref[kb] = s
                for u in range(nslab):
                    m = jnp.maximum(m, s[:, u * LANE:(u + 1) * LANE])
            return m

        m = lax.fori_loop(0, nk // nsub, scores, jnp.full((t, LANE), NEG_BIG, F32))
        mb = jnp.broadcast_to(jnp.max(m, axis=-1, keepdims=True), (t, LANE))
        acc_ref[...] = jnp.zeros_like(acc_ref)

        def pv(g, lrun):
            ps = []
            for i in range(npv):
                for u in range(nslab):
                    p = jnp.exp2(s_ref[g * npv + i, :, u * LANE:(u + 1) * LANE] - mb)
                    lrun = lrun + p
                    ps.append(p.astype(BF16))
            vs = v_ref[0, pl.ds(pl.multiple_of(g * width, width), width), :]
            acc_ref[...] += jnp.dot(jnp.concatenate(ps, axis=1), vs, preferred_element_type=F32)
            return lrun

        trips = ngroups + jnp.minimum(qi, 0)
        lrun = lax.fori_loop(0, trips, pv, jnp.zeros((t, LANE), F32))
        outs.append(acc_ref[...] / jnp.sum(lrun, axis=-1, keepdims=True))

    lam = (jnp.exp(jnp.sum(lam_ref[0:1, :] * lam_ref[1:2, :], axis=-1, keepdims=True))
           - jnp.exp(jnp.sum(lam_ref[2:3, :] * lam_ref[3:4, :], axis=-1, keepdims=True)) + lam_init)
    o = outs[0] - lam * outs[1]
    o_ref[0] = (_rms(o) * og_ref[...] * (1.0 - lam_init)).astype(o_ref.dtype)


def diff_attention(proj, lam_params, og, *, lam_init, t):
    b, s, _ = proj.shape
    assert s % t == 0 and s <= (1 << (2 * POS_LO_BITS + 1)) and t % LANE == 0
    nk = s // t
    nsub = max(n for n in (16, 8, 4, 3, 2, 1) if nk % n == 0)
    npv = max(n for n in (16, 8, 4, 3, 2, 1) if nk % n == 0)
    vw = 2 * HEAD_DIM
    slopes = jnp.asarray(2.0 ** (-8.0 * np.arange(1, N_HEADS + 1) / N_HEADS), F32)
    pieces = np.zeros((1, LANE), np.float32)
    pieces[0, :6] = LOG2E_PIECES * 2
    cq = jnp.asarray(pieces)
    ck = jnp.asarray(-np.roll(pieces, 6, axis=1))
    const = lambda shape: pl.BlockSpec(shape, lambda bi, h, qi, sl: (0, 0))
    grid_spec = pltpu.PrefetchScalarGridSpec(
        num_scalar_prefetch=1,
        grid=(b, N_HEADS, nk),
        in_specs=[
            pl.BlockSpec((1, t, vw), lambda bi, h, qi, sl: (bi, qi, h)),
            pl.BlockSpec((1, s, vw), lambda bi, h, qi, sl: (bi, 0, N_HEADS + h)),
            pl.BlockSpec((1, s, vw), lambda bi, h, qi, sl: (bi, 0, 2 * N_HEADS + h)),
            const((1, LANE)), const((1, LANE)), const((4, HEAD_DIM)), const((1, vw)),
        ],
        out_specs=pl.BlockSpec((1, t, vw), lambda bi, h, qi, sl: (bi, qi, h)),
        scratch_shapes=[pltpu.VMEM((2, s, vw), BF16),
                        pltpu.VMEM((2, 3, t, vw), BF16),
                        pltpu.VMEM((nk, t, t), F32),
                        pltpu.VMEM((2, t, t), F32),
                        pltpu.VMEM((t, vw), F32)],
    )
    return pl.pallas_call(
        functools.partial(_diff_attn_kernel, lam_init=lam_init, nsub=nsub, npv=npv),
        grid_spec=grid_spec,
        out_shape=jax.ShapeDtypeStruct((b, s, N_HEADS * vw), BF16),
        compiler_params=_cparams(("arbitrary", "arbitrary", "arbitrary")),
        name="diff_attention",
    )(slopes, proj, proj, proj, cq, ck, lam_params, og)


def _row(v):
    return v.astype(F32).reshape(1, -1)


def _lane_bcast(v):
    return jnp.broadcast_to(v.astype(F32)[:, None], (v.shape[0], LANE))


def kernel(x, mem, norm_mix_g, norm_xattn_g, norm_mem_g, norm_ffn_g, w_in_ab, ret_decay_fwd, ret_decay_bwd, ret_out_g, na_q_g, na_k_g, na_rpb, w_out_ab, w_in_c, diff_q_g, diff_k_g, lambda_q1, lambda_k1, lambda_q2, lambda_k2, diff_out_g, w_out_c, w_xq, w_xkv, w_xo, xq_g, xk_g, w_ffn_in, w_ffn_out):
    b, s, d = x.shape
    n_mem = mem.shape[1]
    depth = norm_mix_g.shape[0]
    t = b * s
    scale = HEAD_DIM ** -0.5
    width = N_HEADS * HEAD_DIM
    xw = N_X_HEADS * HEAD_DIM
    tm, tn = 1024, 512
    tn_in = 1024

    xf = x.reshape(t, d)
    memf = mem.reshape(b * n_mem, d)
    ones_cols = lambda n: jnp.ones((1, n), F32)
    w_ffn_out_bf16 = w_ffn_out.astype(BF16)

    for i in range(depth):
        j = i // 2
        if i % 2 == 0:
            gcol = jnp.concatenate([
                ones_cols(4 * width),
                jnp.tile(_row(na_q_g[j]) * scale, (1, N_HEADS)),
                jnp.tile(_row(na_k_g[j]), (1, N_HEADS)),
                ones_cols(width)], axis=1)
            proj = norm_matmul(xf, _row(norm_mix_g[i]), w_in_ab, j, gcol,
                               4 * width // tn_in, 6 * width // tn_in, tm=tm, tn=tn_in)
            proj = proj.reshape(b, s, 7 * width)
            lgf = _lane_bcast(jax.nn.log_sigmoid(ret_decay_fwd[j].astype(F32)))
            lgb = _lane_bcast(jax.nn.log_sigmoid(ret_decay_bwd[j].astype(F32)))
            sb = ret_states(proj, lgb, chunk=RET_CHUNK)
            ret = ret_out(proj, sb, lgf, lgb, _row(ret_out_g[j]), chunk=RET_CHUNK)
            na = neighborhood_attention(proj, na_rpb[j])
            mixed = [ret.reshape(t, width), na.reshape(t, width)]
            xf = matmul_res(mixed, w_out_ab, j, xf, tm=tm, tn=tn_in, weight_outer=True)
        else:
            cw = 2 * width
            gcol = jnp.concatenate([
                jnp.tile(_row(diff_q_g[j]) * (scale * LOG2E), (1, 2 * N_HEADS)),
                jnp.tile(_row(diff_k_g[j]), (1, 2 * N_HEADS)),
                ones_cols(cw)], axis=1)
            proj = norm_matmul(xf, _row(norm_mix_g[i]), w_in_c, j, gcol, 0, 2 * cw // tn_in, tm=tm, tn=tn_in)
            proj = proj.reshape(b, s, 3 * cw)
            lam_params = jnp.stack([lambda_q1[j], lambda_k1[j], lambda_q2[j], lambda_k2[j]]).astype(F32)
            lam_init = 0.8 - 0.6 * math.exp(-0.3 * i)
            att = diff_attention(proj, lam_params, _row(diff_out_g[j]), lam_init=lam_init, t=512)
            xf = matmul_res([att.reshape(t, cw)], w_out_c, j, xf, tm=tm, tn=tn_in, weight_outer=True)

        kv_gcol = jnp.concatenate([jnp.tile(_row(xk_g[i]), (1, N_X_HEADS)), ones_cols(xw)], axis=1)
        kv = norm_matmul(memf, _row(norm_mem_g[i]), w_xkv, i, kv_gcol, 0, 1, tm=b * n_mem, tn=xw)
        gq = jnp.tile(_row(xq_g[i]) * scale, (1, N_X_HEADS))
        xf, xn_ffn = xattn(xf, _row(norm_xattn_g[i]), w_xq, gq, kv, w_xo, i, _row(norm_ffn_g[i]), tm=512, seq=s)

        hmid = swiglu_in(xn_ffn, w_ffn_in, i, tm=tm, tn=tn)
        xf = matmul_res([hmid], w_ffn_out_bf16, i, xf, tm=tm, tn=tn)

    return xf.reshape(b, s, d)
```

```python
import functools
import math

import jax
import jax.numpy as jnp
import numpy as np
from jax import lax
from jax.experimental import pallas as pl
from jax.experimental.pallas import tpu as pltpu

F32 = jnp.float32
BF16 = jnp.bfloat16

HEAD_DIM = 128
N_HEADS = 8
N_X_HEADS = 4
GRID_W = 64
NA_WIN_ROWS = 8
NA_WIN_COLS = 16
NA_QROWS = 4
RET_CHUNK = 256
RMS_EPS = 1e-6
NEG_BIG = -1e30

LANE = 128
VMEM_LIMIT = 56 * 1024 * 1024


def _cparams(sem):
    return pltpu.CompilerParams(dimension_semantics=sem, vmem_limit_bytes=VMEM_LIMIT)


def _dot_nt(a, b):
    return lax.dot_general(a, b, (((1,), (1,)), ((), ())), preferred_element_type=F32)


def _dot_nn(a, b):
    return lax.dot_general(a, b, (((1,), (0,)), ((), ())), preferred_element_type=F32)


def _dot_tn(a, b):
    return lax.dot_general(a, b, (((0,), (0,)), ((), ())), preferred_element_type=F32)


def _layer_spec(w, layer, block, index_map):
    if w.ndim == 2:
        return pl.BlockSpec(block, index_map)
    return pl.BlockSpec((None,) + tuple(block), lambda *idx: (layer,) + tuple(index_map(*idx)))


def _rms(x):
    return x * lax.rsqrt(jnp.mean(x * x, axis=-1, keepdims=True) + RMS_EPS)


def _norm_rows_to_scratch(x_ref, g_ref, xn_ref, rows):
    n = x_ref.shape[0] // rows

    def body(r, c):
        sl = pl.ds(pl.multiple_of(r * rows, rows), rows)
        xn_ref[sl, :] = (_rms(x_ref[sl, :]) * g_ref[...]).astype(BF16)
        return c

    lax.fori_loop(0, n, body, 0)


def _norm_matmul_kernel(x_ref, g_ref, w_ref, gc_ref, o_ref, xn_ref, *, lo, hi):
    j = pl.program_id(1)

    @pl.when(j == 0)
    def _():
        _norm_rows_to_scratch(x_ref, g_ref, xn_ref, 128)

    grouped = jnp.logical_and(j >= lo, j < hi)

    @pl.when(grouped)
    def _():
        acc = _dot_nn(xn_ref[...], w_ref[...])
        for c in range(acc.shape[1] // HEAD_DIM):
            sl = slice(c * HEAD_DIM, (c + 1) * HEAD_DIM)
            o_ref[:, sl] = (_rms(acc[:, sl]) * gc_ref[:, sl]).astype(o_ref.dtype)

    @pl.when(jnp.logical_not(grouped))
    def _():
        o_ref[...] = _dot_nn(xn_ref[...], w_ref[...]).astype(o_ref.dtype)


def norm_matmul(x, g, w, layer, gcol, lo, hi, *, tm, tn, out_dtype=BF16):
    t, d = x.shape
    n = w.shape[-1]
    return pl.pallas_call(
        functools.partial(_norm_matmul_kernel, lo=lo, hi=hi),
        grid=(t // tm, n // tn),
        in_specs=[
            pl.BlockSpec((tm, d), lambda i, j: (i, 0)),
            pl.BlockSpec((1, d), lambda i, j: (0, 0)),
            _layer_spec(w, layer, (d, tn), lambda i, j: (0, j)),
            pl.BlockSpec((1, tn), lambda i, j: (0, j)),
        ],
        out_specs=pl.BlockSpec((tm, tn), lambda i, j: (i, j)),
        out_shape=jax.ShapeDtypeStruct((t, n), out_dtype),
        scratch_shapes=[pltpu.VMEM((tm, d), BF16)],
        compiler_params=_cparams(("parallel", "arbitrary")),
        name="norm_matmul",
    )(x, g, w, gcol)


def _swiglu_in_kernel(xn_ref, wg_ref, wu_ref, o_ref):
    xn = xn_ref[...]
    gate = _dot_nn(xn, wg_ref[...])
    up = _dot_nn(xn, wu_ref[...])
    o_ref[...] = (gate * jax.nn.sigmoid(gate) * up).astype(o_ref.dtype)


def swiglu_in(xn, w_in, layer, *, tm, tn):
    t, d = xn.shape
    hidden = w_in.shape[-1] // 2
    nh = hidden // tn
    return pl.pallas_call(
        _swiglu_in_kernel,
        grid=(t // tm, nh),
        in_specs=[
            pl.BlockSpec((tm, d), lambda i, j: (i, 0)),
            _layer_spec(w_in, layer, (d, tn), lambda i, j: (0, j)),
            _layer_spec(w_in, layer, (d, tn), lambda i, j: (0, j + nh)),
        ],
        out_specs=pl.BlockSpec((tm, tn), lambda i, j: (i, j)),
        out_shape=jax.ShapeDtypeStruct((t, hidden), BF16),
        compiler_params=_cparams(("parallel", "arbitrary")),
        name="swiglu_in",
    )(xn, w_in, w_in)


def _matmul_res_kernel(*refs):
    *a_refs, w_ref, r_ref, o_ref = refs
    acc = r_ref[...]
    k0 = 0
    for a_ref in a_refs:
        k1 = k0 + a_ref.shape[1]
        acc += _dot_nn(a_ref[...], w_ref[k0:k1, :])
        k0 = k1
    o_ref[...] = acc


def matmul_res(a_parts, w, layer, res, *, tm, tn, weight_outer=False):
    t = res.shape[0]
    k, n = w.shape[-2:]
    assert sum(a.shape[1] for a in a_parts) == k
    if weight_outer:
        grid, ij = (n // tn, t // tm), (lambda j, i: (i, j))
    else:
        grid, ij = (t // tm, n // tn), (lambda i, j: (i, j))
    row = lambda *g: (ij(*g)[0], 0)
    col = lambda *g: (0, ij(*g)[1])
    return pl.pallas_call(
        _matmul_res_kernel,
        grid=grid,
        in_specs=[pl.BlockSpec((tm, a.shape[1]), row) for a in a_parts] + [
            _layer_spec(w, layer, (k, tn), col),
            pl.BlockSpec((tm, tn), ij),
        ],
        out_specs=pl.BlockSpec((tm, tn), ij),
        out_shape=jax.ShapeDtypeStruct((t, n), F32),
        compiler_params=_cparams(("arbitrary", "arbitrary")),
        name="matmul_res",
    )(*a_parts, w, res)


def _xattn_kernel(x_ref, g_ref, wq_ref, gq_ref, kv_ref, wo_ref, gn_ref, o_ref, on_ref):
    x = x_ref[...]
    xn = (_rms(x) * g_ref[...]).astype(BF16)
    q = _dot_nn(xn, wq_ref[...])
    width = N_X_HEADS * HEAD_DIM
    outs = []
    for h in range(N_X_HEADS):
        sl = slice(h * HEAD_DIM, (h + 1) * HEAD_DIM)
        qh = (_rms(q[:, sl]) * gq_ref[:, sl]).astype(BF16)
        s = _dot_nt(qh, kv_ref[:, sl])
        p = jnp.exp(s - jnp.max(s, axis=-1, keepdims=True))
        l = jnp.sum(p, axis=-1, keepdims=True)
        vh = kv_ref[:, width + h * HEAD_DIM: width + (h + 1) * HEAD_DIM]
        oh = jnp.dot(p.astype(BF16), vh, preferred_element_type=F32) / l
        outs.append(oh.astype(BF16))
    o = jnp.concatenate(outs, axis=1)
    y = x + _dot_nn(o, wo_ref[...])
    o_ref[...] = y
    on_ref[...] = (_rms(y) * gn_ref[...]).astype(on_ref.dtype)


def xattn(x, g, wq, gq, kv, wo, layer, g_next, *, tm, seq):
    t, d = x.shape
    n_mem = kv.shape[0] // (t // seq)
    width = wq.shape[-1]
    per_b = seq // tm
    return pl.pallas_call(
        _xattn_kernel,
        grid=(t // tm,),
        in_specs=[
            pl.BlockSpec((tm, d), lambda i: (i, 0)),
            pl.BlockSpec((1, d), lambda i: (0, 0)),
            _layer_spec(wq, layer, (d, width), lambda i: (0, 0)),
            pl.BlockSpec((1, width), lambda i: (0, 0)),
            pl.BlockSpec((n_mem, 2 * width), lambda i: (i // per_b, 0)),
            _layer_spec(wo, layer, (width, d), lambda i: (0, 0)),
            pl.BlockSpec((1, d), lambda i: (0, 0)),
        ],
        out_specs=[pl.BlockSpec((tm, d), lambda i: (i, 0)), pl.BlockSpec((tm, d), lambda i: (i, 0))],
        out_shape=[jax.ShapeDtypeStruct((t, d), F32), jax.ShapeDtypeStruct((t, d), BF16)],
        compiler_params=_cparams(("parallel",)),
        name="xattn",
    )(x, g, wq, gq, kv, wo, g_next)


def _ret_state_kernel(k_ref, v_ref, lgb_ref, sb_ref, state_ref, kd_ref, *, chunk):
    c = pl.program_id(1)
    per_step = k_ref.shape[1] // chunk
    scale = HEAD_DIM ** -0.5

    @pl.when(c == 0)
    def _():
        state_ref[...] = jnp.zeros_like(state_ref)
        pos = lax.broadcasted_iota(jnp.int32, (chunk, HEAD_DIM), 0).astype(F32)
        for h in range(N_HEADS):
            kd_ref[h] = jnp.exp(lgb_ref[h:h + 1, :] * pos) * scale

    for u in reversed(range(per_step)):
        rows = slice(u * chunk, (u + 1) * chunk)
        for h in range(N_HEADS):
            sl = slice(h * HEAD_DIM, (h + 1) * HEAD_DIM)
            st = state_ref[h]
            sb_ref[0, u, h] = st.astype(BF16)
            kd = (k_ref[0, rows, sl].astype(F32) * kd_ref[h]).astype(BF16)
            cdec = jnp.exp(lgb_ref[h:h + 1, :] * float(chunk))
            state_ref[h] = st * cdec + _dot_tn(kd, v_ref[0, rows, sl])


def ret_states(proj, lgb, *, chunk, per_step):
    b, s, _ = proj.shape
    nc = s // chunk
    assert nc % per_step == 0
    ns = nc // per_step
    width = N_HEADS * HEAD_DIM
    return pl.pallas_call(
        functools.partial(_ret_state_kernel, chunk=chunk),
        grid=(b, ns),
        in_specs=[
            pl.BlockSpec((1, per_step * chunk, width), lambda bi, c: (bi, ns - 1 - c, 1)),
            pl.BlockSpec((1, per_step * chunk, width), lambda bi, c: (bi, ns - 1 - c, 2)),
            pl.BlockSpec((N_HEADS, LANE), lambda bi, c: (0, 0)),
        ],
        out_specs=pl.BlockSpec((1, per_step, N_HEADS, HEAD_DIM, HEAD_DIM),
                               lambda bi, c: (bi, ns - 1 - c, 0, 0, 0)),
        out_shape=jax.ShapeDtypeStruct((b, nc, N_HEADS, HEAD_DIM, HEAD_DIM), BF16),
        scratch_shapes=[pltpu.VMEM((N_HEADS, HEAD_DIM, HEAD_DIM), F32),
                        pltpu.VMEM((N_HEADS, chunk, HEAD_DIM), F32)],
        compiler_params=_cparams(("parallel", "arbitrary")),
        name="ret_states",
    )(proj, proj, lgb)


def _ret_out_kernel(q_ref, k_ref, v_ref, rg_ref, sb_ref, lgf_ref, lgb_ref, og_ref, o_ref,
                    state_ref, dmat_ref, qdf_ref, qdb_ref, kdf_ref):
    c = pl.program_id(1)
    chunk = q_ref.shape[1]
    scale = HEAD_DIM ** -0.5

    @pl.when(c == 0)
    def _():
        state_ref[...] = jnp.zeros_like(state_ref)
        pos = lax.broadcasted_iota(jnp.int32, (chunk, HEAD_DIM), 0).astype(F32)
        ri = lax.broadcasted_iota(jnp.int32, (chunk, chunk), 0)
        ci = lax.broadcasted_iota(jnp.int32, (chunk, chunk), 1)
        diff = (ri - ci).astype(F32)
        for h in range(N_HEADS):
            lf = lgf_ref[h:h + 1, :]
            lb = lgb_ref[h:h + 1, :]
            dfwd = jnp.exp(lf[:, :1] * jnp.maximum(diff, 0.0))
            dbwd = jnp.exp(lb[:, :1] * jnp.maximum(-diff, 0.0))
            dmat_ref[h] = jnp.where(diff >= 0.0, dfwd, dbwd) * scale
            qdf_ref[h] = jnp.exp(lf * (pos + 1.0))
            qdb_ref[h] = jnp.exp(lb * (float(chunk) - pos))
            kdf_ref[h] = jnp.exp(lf * (float(chunk) - 1.0 - pos)) * scale

    for h in range(N_HEADS):
        sl = slice(h * HEAD_DIM, (h + 1) * HEAD_DIM)
        qh = q_ref[0, :, sl]
        kh = k_ref[0, :, sl]
        vh = v_ref[0, :, sl]
        qf32 = qh.astype(F32)
        sd = (_dot_nt(qh, kh) * dmat_ref[h]).astype(BF16)
        st = state_ref[h]
        out = jnp.dot(sd, vh, preferred_element_type=F32)
        out += jnp.dot((qf32 * qdf_ref[h]).astype(BF16), st.astype(BF16), preferred_element_type=F32)
        out += jnp.dot((qf32 * qdb_ref[h]).astype(BF16), sb_ref[0, 0, h], preferred_element_type=F32)
        gate = rg_ref[0, :, sl].astype(F32)
        y = _rms(out) * og_ref[...]
        o_ref[0, :, sl] = (y * (gate * jax.nn.sigmoid(gate))).astype(o_ref.dtype)
        kd = (kh.astype(F32) * kdf_ref[h]).astype(BF16)
        cdec = jnp.exp(lgf_ref[h:h + 1, :] * float(chunk))
        state_ref[h] = st * cdec + _dot_tn(kd, vh)


def ret_out(proj, sb, lgf, lgb, og, *, chunk):
    b, s, _ = proj.shape
    nc = s // chunk
    width = N_HEADS * HEAD_DIM
    col = lambda k: pl.BlockSpec((1, chunk, width), lambda bi, c, k=k: (bi, c, k))
    return pl.pallas_call(
        _ret_out_kernel,
        grid=(b, nc),
        in_specs=[
            col(0), col(1), col(2), col(3),
            pl.BlockSpec((1, 1, N_HEADS, HEAD_DIM, HEAD_DIM), lambda bi, c: (bi, c, 0, 0, 0)),
            pl.BlockSpec((N_HEADS, LANE), lambda bi, c: (0, 0)),
            pl.BlockSpec((N_HEADS, LANE), lambda bi, c: (0, 0)),
            pl.BlockSpec((1, HEAD_DIM), lambda bi, c: (0, 0)),
        ],
        out_specs=pl.BlockSpec((1, chunk, width), lambda bi, c: (bi, c, 0)),
        out_shape=jax.ShapeDtypeStruct((b, s, width), BF16),
        scratch_shapes=[pltpu.VMEM((N_HEADS, HEAD_DIM, HEAD_DIM), F32),
                        pltpu.VMEM((N_HEADS, chunk, chunk), F32),
                        pltpu.VMEM((N_HEADS, chunk, HEAD_DIM), F32),
                        pltpu.VMEM((N_HEADS, chunk, HEAD_DIM), F32),
                        pltpu.VMEM((N_HEADS, chunk, HEAD_DIM), F32)],
        compiler_params=_cparams(("parallel", "arbitrary")),
        name="ret_out",
    )(proj, proj, proj, proj, sb, lgf, lgb, og)


N_DR = 2 * NA_WIN_ROWS - 1
N_DC = 2 * NA_WIN_COLS - 1


def _na_kernel(rpb_ref, q_ref, kp_ref, kc_ref, kn_ref, vp_ref, vc_ref, vn_ref, o_ref,
               bcol_ref, slab_ref, *, rows):
    qb = pl.program_id(1)
    nb = rows // NA_QROWS
    neg_tile = jnp.full((GRID_W, GRID_W), NEG_BIG, F32)

    @pl.when(qb == 0)
    def _():
        qc = lax.broadcasted_iota(jnp.int32, (GRID_W, GRID_W), 0)
        kc = lax.broadcasted_iota(jnp.int32, (GRID_W, GRID_W), 1)
        cs = jnp.clip(qc - NA_WIN_COLS // 2, 0, GRID_W - NA_WIN_COLS)
        col_ok = jnp.logical_and(kc >= cs, kc < cs + NA_WIN_COLS)
        dc = kc - qc + (NA_WIN_COLS - 1)
        for h in range(N_HEADS):
            for dr in range(N_DR):
                tile = neg_tile
                for d in range(N_DC):
                    tile = jnp.where(dc == d, rpb_ref[(h * N_DR + dr) * N_DC + d], tile)
                bcol_ref[h, dr] = jnp.where(col_ok, tile, NEG_BIG)

    def assemble(qb_static):
        for h in range(N_HEADS):
            for j in range(NA_QROWS):
                r = NA_QROWS * qb_static + j
                rs = min(max(r - NA_WIN_ROWS // 2, 0), rows - NA_WIN_ROWS)
                for i2 in range(0, 3 * NA_QROWS, 2):
                    halves = []
                    for i in (i2, i2 + 1):
                        kr = NA_QROWS * (qb_static - 1) + i
                        ok = rs <= kr < rs + NA_WIN_ROWS
                        halves.append(bcol_ref[h, kr - r + NA_WIN_ROWS - 1] if ok else neg_tile)
                    slab_ref[h, j * GRID_W:(j + 1) * GRID_W, i2 * GRID_W:(i2 + 2) * GRID_W] = (
                        jnp.concatenate(halves, axis=1))

    for qb_static in (0, 1, nb - 1):
        pl.when(qb == qb_static)(functools.partial(assemble, qb_static))

    for h in range(N_HEADS):
        sl = slice(h * HEAD_DIM, (h + 1) * HEAD_DIM)
        k = jnp.concatenate([kp_ref[0, :, sl], kc_ref[0, :, sl], kn_ref[0, :, sl]], axis=0)
        v = jnp.concatenate([vp_ref[0, :, sl], vc_ref[0, :, sl], vn_ref[0, :, sl]], axis=0)
        s = _dot_nt(q_ref[0, :, sl], k) + slab_ref[h]
        p = jnp.exp(s - jnp.max(s, axis=-1, keepdims=True))
        l = jnp.sum(p, axis=-1, keepdims=True)
        o = jnp.dot(p.astype(BF16), v, preferred_element_type=F32) / l
        o_ref[0, :, sl] = o.astype(o_ref.dtype)


def neighborhood_attention(proj, rpb):
    b, s, _ = proj.shape
    rows = s // GRID_W
    assert NA_QROWS >= NA_WIN_ROWS // 2 and NA_QROWS % 2 == 0 and rows % NA_QROWS == 0
    assert rows >= 3 * NA_QROWS and rows >= NA_WIN_ROWS
    nq = NA_QROWS * GRID_W
    nb = rows // NA_QROWS
    width = N_HEADS * HEAD_DIM

    def blk(group, shift):
        return pl.BlockSpec((1, nq, width),
                            lambda bi, qb, rp: (bi, jnp.clip(qb + shift, 0, nb - 1), group))

    grid_spec = pltpu.PrefetchScalarGridSpec(
        num_scalar_prefetch=1,
        grid=(b, nb),
        in_specs=[blk(4, 0), blk(5, -1), blk(5, 0), blk(5, 1), blk(6, -1), blk(6, 0), blk(6, 1)],
        out_specs=pl.BlockSpec((1, nq, width), lambda bi, qb, rp: (bi, qb, 0)),
        scratch_shapes=[pltpu.VMEM((N_HEADS, N_DR, GRID_W, GRID_W), F32),
                        pltpu.VMEM((N_HEADS, nq, 3 * nq), F32)],
    )
    return pl.pallas_call(
        functools.partial(_na_kernel, rows=rows),
        grid_spec=grid_spec,
        out_shape=jax.ShapeDtypeStruct((b, s, width), BF16),
        compiler_params=_cparams(("arbitrary", "arbitrary")),
        name="neighborhood_attention",
    )(rpb.astype(F32).reshape(-1), proj, proj, proj, proj, proj, proj, proj)


def _bf16_pieces(x, n):
    out, rest = [], float(x)
    for _ in range(n):
        piece = float(np.float32(rest).astype(BF16).astype(np.float32))
        out.append(piece)
        rest -= piece
    return out


LOG2E = math.log2(math.e)
LOG2E_PIECES = _bf16_pieces(LOG2E, 3)
N_ALIBI_COLS = 12
POS_LO_BITS = 7


def _alibi_cols(pos, slope, lane, unit_cols, sign):
    hi = (pos >> POS_LO_BITS).astype(F32) * (slope * float(1 << POS_LO_BITS))
    lo = (pos & ((1 << POS_LO_BITS) - 1)).astype(F32) * slope
    if sign > 0:
        return jnp.where(lane < 3, hi, jnp.where(lane < 6, lo, unit_cols))
    return jnp.where(lane < 6, unit_cols, jnp.where(lane < 9, hi, jnp.where(lane < N_ALIBI_COLS, lo, 0.0)))


def _diff_attn_kernel(slopes_ref, q_ref, k_ref, v_ref, cq_ref, ck_ref, lam_ref, og_ref, o_ref,
                      kaug_ref, qaug_ref, s_ref, bias_ref, acc_ref, *, lam_init, nsub, npv):
    h = pl.program_id(1)
    qi = pl.program_id(2)
    t = q_ref.shape[1]
    nk = k_ref.shape[1] // t
    nslab = t // LANE
    slope = slopes_ref[h]
    lane = lax.broadcasted_iota(jnp.int32, (t, LANE), 1)
    row = lax.broadcasted_iota(jnp.int32, (t, LANE), 0)

    @pl.when(qi == 0)
    def _():
        def body(r, carry):
            rs = pl.multiple_of(r * t, t)
            kx = _alibi_cols(row + rs, slope, lane, ck_ref[...], 1).astype(BF16)
            for c in range(2):
                kaug_ref[c, pl.ds(rs, t), 0:HEAD_DIM] = k_ref[0, pl.ds(rs, t), c * HEAD_DIM:(c + 1) * HEAD_DIM]
                kaug_ref[c, pl.ds(rs, t), HEAD_DIM:2 * HEAD_DIM] = kx
            return carry

        lax.fori_loop(0, nk, body, 0)
        ri = lax.broadcasted_iota(jnp.int32, (t, t), 0)
        ci = lax.broadcasted_iota(jnp.int32, (t, t), 1)
        bias_ref[0] = jnp.zeros((t, t), F32)
        bias_ref[1] = jnp.abs(ri - ci).astype(F32) * (-LOG2E * slope)

    qx = _alibi_cols(row + qi * t, slope, lane, cq_ref[...], -1)
    for c in range(2):
        qc = q_ref[0, :, c * HEAD_DIM:(c + 1) * HEAD_DIM]
        for side, ext in ((0, qx), (1, -qx), (2, jnp.zeros_like(qx))):
            qaug_ref[c, side, :, 0:HEAD_DIM] = qc
            qaug_ref[c, side, :, HEAD_DIM:2 * HEAD_DIM] = ext.astype(BF16)

    width = npv * t
    ngroups = nk // npv
    outs = []
    for c in range(2):
        def scores(g, m):
            for i in range(nsub):
                kb = g * nsub + i
                side = jnp.where(kb < qi, 0, jnp.where(kb > qi, 1, 2))
                ks = pl.multiple_of(kb * t, t)
                s = _dot_nt(qaug_ref[c, side], kaug_ref[c, pl.ds(ks, t), :])
                s = s + bias_ref[(kb == qi).astype(jnp.int32)]
                s_ref[kb] = s
                for u in range(nslab):
                    m = jnp.maximum(m, s[:, u * LANE:(u + 1) * LANE])
            return m

        m = lax.fori_loop(0, nk // nsub, scores, jnp.full((t, LANE), NEG_BIG, F32))
        mb = jnp.broadcast_to(jnp.max(m, axis=-1, keepdims=True), (t, LANE))
        acc_ref[...] = jnp.zeros_like(acc_ref)

        def pv(g, lrun):
            ps = []
            for i in range(npv):
                for u in range(nslab):
                    p = jnp.exp2(s_ref[g * npv + i, :, u * LANE:(u + 1) * LANE] - mb)
                    lrun = lrun + p
                    ps.append(p.astype(BF16))
            vs = v_ref[0, pl.ds(pl.multiple_of(g * width, width), width), :]
            acc_ref[...] += jnp.dot(jnp.concatenate(ps, axis=1), vs, preferred_element_type=F32)
            return lrun

        trips = ngroups + jnp.minimum(qi, 0)
        lrun = lax.fori_loop(0, trips, pv, jnp.zeros((t, LANE), F32))
        outs.append(acc_ref[...] / jnp.sum(lrun, axis=-1, keepdims=True))

    lam = (jnp.exp(jnp.sum(lam_ref[0:1, :] * lam_ref[1:2, :], axis=-1, keepdims=True))
           - jnp.exp(jnp.sum(lam_ref[2:3, :] * lam_ref[3:4, :], axis=-1, keepdims=True)) + lam_init)
    o = outs[0] - lam * outs[1]
    o_ref[0] = (_rms(o) * og_ref[...] * (1.0 - lam_init)).astype(o_ref.dtype)


def diff_attention(proj, lam_params, og, *, lam_init, t):
    b, s, _ = proj.shape
    assert s % t == 0 and s <= (1 << (2 * POS_LO_BITS + 1)) and t % LANE == 0
    nk = s // t
    nsub = max(n for n in (16, 8, 4, 3, 2, 1) if nk % n == 0)
    npv = max(n for n in (16, 8, 4, 3, 2, 1) if nk % n == 0)
    vw = 2 * HEAD_DIM
    slopes = jnp.asarray(2.0 ** (-8.0 * np.arange(1, N_HEADS + 1) / N_HEADS), F32)
    pieces = np.zeros((1, LANE), np.float32)
    pieces[0, :6] = LOG2E_PIECES * 2
    cq = jnp.asarray(pieces)
    ck = jnp.asarray(-np.roll(pieces, 6, axis=1))
    const = lambda shape: pl.BlockSpec(shape, lambda bi, h, qi, sl: (0, 0))
    grid_spec = pltpu.PrefetchScalarGridSpec(
        num_scalar_prefetch=1,
        grid=(b, N_HEADS, nk),
        in_specs=[
            pl.BlockSpec((1, t, vw), lambda bi, h, qi, sl: (bi, qi, h)),
            pl.BlockSpec((1, s, vw), lambda bi, h, qi, sl: (bi, 0, N_HEADS + h)),
            pl.BlockSpec((1, s, vw), lambda bi, h, qi, sl: (bi, 0, 2 * N_HEADS + h)),
            const((1, LANE)), const((1, LANE)), const((4, HEAD_DIM)), const((1, vw)),
        ],
        out_specs=pl.BlockSpec((1, t, vw), lambda bi, h, qi, sl: (bi, qi, h)),
        scratch_shapes=[pltpu.VMEM((2, s, vw), BF16),
                        pltpu.VMEM((2, 3, t, vw), BF16),
                        pltpu.VMEM((nk, t, t), F32),
                        pltpu.VMEM((2, t, t), F32),
                        pltpu.VMEM((t, vw), F32)],
    )
    return pl.pallas_call(
        functools.partial(_diff_attn_kernel, lam_init=lam_init, nsub=nsub, npv=npv),
        grid_spec=grid_spec,
        out_shape=jax.ShapeDtypeStruct((b, s, N_HEADS * vw), BF16),
        compiler_params=_cparams(("arbitrary", "arbitrary", "arbitrary")),
        name="diff_attention",
    )(slopes, proj, proj, proj, cq, ck, lam_params, og)


def _row(v):
    return v.astype(F32).reshape(1, -1)


def _lane_bcast(v):
    return jnp.broadcast_to(v.astype(F32)[:, None], (v.shape[0], LANE))


def kernel(x, mem, norm_mix_g, norm_xattn_g, norm_mem_g, norm_ffn_g, w_in_ab, ret_decay_fwd, ret_decay_bwd, ret_out_g, na_q_g, na_k_g, na_rpb, w_out_ab, w_in_c, diff_q_g, diff_k_g, lambda_q1, lambda_k1, lambda_q2, lambda_k2, diff_out_g, w_out_c, w_xq, w_xkv, w_xo, xq_g, xk_g, w_ffn_in, w_ffn_out):
    b, s, d = x.shape
    n_mem = mem.shape[1]
    depth = norm_mix_g.shape[0]
    t = b * s
    scale = HEAD_DIM ** -0.5
    width = N_HEADS * HEAD_DIM
    xw = N_X_HEADS * HEAD_DIM
    tm, tn = 1024, 512
    tn_in = 1024

    xf = x.reshape(t, d)
    memf = mem.reshape(b * n_mem, d)
    ones_cols = lambda n: jnp.ones((1, n), F32)
    w_ffn_out_bf16 = w_ffn_out.astype(BF16)

    for i in range(depth):
        j = i // 2
        if i % 2 == 0:
            gcol = jnp.concatenate([
                ones_cols(4 * width),
                jnp.tile(_row(na_q_g[j]) * scale, (1, N_HEADS)),
                jnp.tile(_row(na_k_g[j]), (1, N_HEADS)),
                ones_cols(width)], axis=1)
            proj = norm_matmul(xf, _row(norm_mix_g[i]), w_in_ab, j, gcol,
                               4 * width // tn_in, 6 * width // tn_in, tm=tm, tn=tn_in)
            proj = proj.reshape(b, s, 7 * width)
            lgf = _lane_bcast(jax.nn.log_sigmoid(ret_decay_fwd[j].astype(F32)))
            lgb = _lane_bcast(jax.nn.log_sigmoid(ret_decay_bwd[j].astype(F32)))
            sb = ret_states(proj, lgb, chunk=RET_CHUNK, per_step=4 if (s // RET_CHUNK) % 4 == 0 else 1)
            ret = ret_out(proj, sb, lgf, lgb, _row(ret_out_g[j]), chunk=RET_CHUNK)
            na = neighborhood_attention(proj, na_rpb[j])
            mixed = [ret.reshape(t, width), na.reshape(t, width)]
            xf = matmul_res(mixed, w_out_ab, j, xf, tm=tm, tn=tn_in, weight_outer=True)
        else:
            cw = 2 * width
            gcol = jnp.concatenate([
                jnp.tile(_row(diff_q_g[j]) * (scale * LOG2E), (1, 2 * N_HEADS)),
                jnp.tile(_row(diff_k_g[j]), (1, 2 * N_HEADS)),
                ones_cols(cw)], axis=1)
            proj = norm_matmul(xf, _row(norm_mix_g[i]), w_in_c, j, gcol, 0, 2 * cw // tn_in, tm=tm, tn=tn_in)
            proj = proj.reshape(b, s, 3 * cw)
            lam_params = jnp.stack([lambda_q1[j], lambda_k1[j], lambda_q2[j], lambda_k2[j]]).astype(F32)
            lam_init = 0.8 - 0.6 * math.exp(-0.3 * i)
            att = diff_attention(proj, lam_params, _row(diff_out_g[j]), lam_init=lam_init, t=512)
            xf = matmul_res([att.reshape(t, cw)], w_out_c, j, xf, tm=tm, tn=tn_in, weight_outer=True)

        kv_gcol = jnp.concatenate([jnp.tile(_row(xk_g[i]), (1, N_X_HEADS)), ones_cols(xw)], axis=1)
        kv = norm_matmul(memf, _row(norm_mem_g[i]), w_xkv, i, kv_gcol, 0, 1, tm=b * n_mem, tn=xw)
        gq = jnp.tile(_row(xq_g[i]) * scale, (1, N_X_HEADS))
        xf, xn_ffn = xattn(xf, _row(norm_xattn_g[i]), w_xq, gq, kv, w_xo, i, _row(norm_ffn_g[i]), tm=512, seq=s)

        hmid = swiglu_in(xn_ffn, w_ffn_in, i, tm=tm, tn=tn)
        xf = matmul_res([hmid], w_ffn_out_bf16, i, xf, tm=tm, tn=tn)

    return xf.reshape(b, s, d)
```

```python
import functools
import math

import jax
import jax.numpy as jnp
import numpy as np
from jax import lax
from jax.experimental import pallas as pl
from jax.experimental.pallas import tpu as pltpu

F32 = jnp.float32
BF16 = jnp.bfloat16

HEAD_DIM = 128
N_HEADS = 8
N_X_HEADS = 4
GRID_W = 64
NA_WIN_ROWS = 8
NA_WIN_COLS = 16
NA_QROWS = 4
RET_CHUNK = 256
RET_STATE_CHUNKS_PER_STEP = 4

ROW_TILE = 1024
COL_TILE = 512
MIXER_COL_TILE = 1024
XATTN_ROW_TILE = 512
ATTN_TILE = 512
NORM_ROW_CHUNK = 128
RMS_EPS = 1e-6
NEG_BIG = -1e30

LANE = 128
VMEM_LIMIT = 56 * 1024 * 1024


def _cparams(sem):
    return pltpu.CompilerParams(dimension_semantics=sem, vmem_limit_bytes=VMEM_LIMIT)


def _dot_nt(a, b):
    return lax.dot_general(a, b, (((1,), (1,)), ((), ())), preferred_element_type=F32)


def _dot_nn(a, b):
    return lax.dot_general(a, b, (((1,), (0,)), ((), ())), preferred_element_type=F32)


def _dot_tn(a, b):
    return lax.dot_general(a, b, (((0,), (0,)), ((), ())), preferred_element_type=F32)


def _layer_spec(w, layer, block, index_map):
    if w.ndim == 2:
        return pl.BlockSpec(block, index_map)
    return pl.BlockSpec((None,) + tuple(block), lambda *idx: (layer,) + tuple(index_map(*idx)))


def _rms(x):
    return x * lax.rsqrt(jnp.mean(x * x, axis=-1, keepdims=True) + RMS_EPS)


def _norm_rows_to_scratch(x_ref, g_ref, xn_ref, rows):
    n = x_ref.shape[0] // rows

    def body(r, c):
        sl = pl.ds(pl.multiple_of(r * rows, rows), rows)
        xn_ref[sl, :] = (_rms(x_ref[sl, :]) * g_ref[...]).astype(BF16)
        return c

    lax.fori_loop(0, n, body, 0)


def _norm_matmul_kernel(x_ref, g_ref, w_ref, gc_ref, o_ref, xn_ref, *, lo, hi):
    j = pl.program_id(1)

    @pl.when(j == 0)
    def _():
        _norm_rows_to_scratch(x_ref, g_ref, xn_ref, NORM_ROW_CHUNK)

    grouped = jnp.logical_and(j >= lo, j < hi)

    @pl.when(grouped)
    def _():
        acc = _dot_nn(xn_ref[...], w_ref[...])
        for c in range(acc.shape[1] // HEAD_DIM):
            sl = slice(c * HEAD_DIM, (c + 1) * HEAD_DIM)
            o_ref[:, sl] = (_rms(acc[:, sl]) * gc_ref[:, sl]).astype(o_ref.dtype)

    @pl.when(jnp.logical_not(grouped))
    def _():
        o_ref[...] = _dot_nn(xn_ref[...], w_ref[...]).astype(o_ref.dtype)


def norm_matmul(x, g, w, layer, gcol, lo, hi, *, tm, tn, out_dtype=BF16):
    t, d = x.shape
    n = w.shape[-1]
    return pl.pallas_call(
        functools.partial(_norm_matmul_kernel, lo=lo, hi=hi),
        grid=(t // tm, n // tn),
        in_specs=[
            pl.BlockSpec((tm, d), lambda i, j: (i, 0)),
            pl.BlockSpec((1, d), lambda i, j: (0, 0)),
            _layer_spec(w, layer, (d, tn), lambda i, j: (0, j)),
            pl.BlockSpec((1, tn), lambda i, j: (0, j)),
        ],
        out_specs=pl.BlockSpec((tm, tn), lambda i, j: (i, j)),
        out_shape=jax.ShapeDtypeStruct((t, n), out_dtype),
        scratch_shapes=[pltpu.VMEM((tm, d), BF16)],
        compiler_params=_cparams(("parallel", "arbitrary")),
        name="norm_matmul",
    )(x, g, w, gcol)


def _swiglu_in_kernel(xn_ref, wg_ref, wu_ref, o_ref):
    xn = xn_ref[...]
    gate = _dot_nn(xn, wg_ref[...])
    up = _dot_nn(xn, wu_ref[...])
    o_ref[...] = (gate * jax.nn.sigmoid(gate) * up).astype(o_ref.dtype)


def swiglu_in(xn, w_in, layer, *, tm, tn):
    t, d = xn.shape
    hidden = w_in.shape[-1] // 2
    nh = hidden // tn
    return pl.pallas_call(
        _swiglu_in_kernel,
        grid=(t // tm, nh),
        in_specs=[
            pl.BlockSpec((tm, d), lambda i, j: (i, 0)),
            _layer_spec(w_in, layer, (d, tn), lambda i, j: (0, j)),
            _layer_spec(w_in, layer, (d, tn), lambda i, j: (0, j + nh)),
        ],
        out_specs=pl.BlockSpec((tm, tn), lambda i, j: (i, j)),
        out_shape=jax.ShapeDtypeStruct((t, hidden), BF16),
        compiler_params=_cparams(("parallel", "arbitrary")),
        name="swiglu_in",
    )(xn, w_in, w_in)


def _matmul_res_kernel(*refs):
    *a_refs, w_ref, r_ref, o_ref = refs
    acc = r_ref[...]
    k0 = 0
    for a_ref in a_refs:
        k1 = k0 + a_ref.shape[1]
        acc += _dot_nn(a_ref[...], w_ref[k0:k1, :])
        k0 = k1
    o_ref[...] = acc


def matmul_res(a_parts, w, layer, res, *, tm, tn, weight_outer=False):
    t = res.shape[0]
    k, n = w.shape[-2:]
    assert sum(a.shape[1] for a in a_parts) == k
    if weight_outer:
        grid, ij = (n // tn, t // tm), (lambda j, i: (i, j))
    else:
        grid, ij = (t // tm, n // tn), (lambda i, j: (i, j))
    row = lambda *g: (ij(*g)[0], 0)
    col = lambda *g: (0, ij(*g)[1])
    return pl.pallas_call(
        _matmul_res_kernel,
        grid=grid,
        in_specs=[pl.BlockSpec((tm, a.shape[1]), row) for a in a_parts] + [
            _layer_spec(w, layer, (k, tn), col),
            pl.BlockSpec((tm, tn), ij),
        ],
        out_specs=pl.BlockSpec((tm, tn), ij),
        out_shape=jax.ShapeDtypeStruct((t, n), F32),
        compiler_params=_cparams(("arbitrary", "arbitrary")),
        name="matmul_res",
    )(*a_parts, w, res)


def _xattn_kernel(x_ref, g_ref, wq_ref, gq_ref, kv_ref, wo_ref, gn_ref, o_ref, on_ref):
    x = x_ref[...]
    xn = (_rms(x) * g_ref[...]).astype(BF16)
    q = _dot_nn(xn, wq_ref[...])
    width = N_X_HEADS * HEAD_DIM
    outs = []
    for h in range(N_X_HEADS):
        sl = slice(h * HEAD_DIM, (h + 1) * HEAD_DIM)
        qh = (_rms(q[:, sl]) * gq_ref[:, sl]).astype(BF16)
        s = _dot_nt(qh, kv_ref[:, sl])
        p = jnp.exp(s - jnp.max(s, axis=-1, keepdims=True))
        l = jnp.sum(p, axis=-1, keepdims=True)
        vh = kv_ref[:, width + h * HEAD_DIM: width + (h + 1) * HEAD_DIM]
        oh = jnp.dot(p.astype(BF16), vh, preferred_element_type=F32) / l
        outs.append(oh.astype(BF16))
    o = jnp.concatenate(outs, axis=1)
    y = x + _dot_nn(o, wo_ref[...])
    o_ref[...] = y
    on_ref[...] = (_rms(y) * gn_ref[...]).astype(on_ref.dtype)


def xattn(x, g, wq, gq, kv, wo, layer, g_next, *, tm, seq):
    t, d = x.shape
    n_mem = kv.shape[0] // (t // seq)
    width = wq.shape[-1]
    per_b = seq // tm
    return pl.pallas_call(
        _xattn_kernel,
        grid=(t // tm,),
        in_specs=[
            pl.BlockSpec((tm, d), lambda i: (i, 0)),
            pl.BlockSpec((1, d), lambda i: (0, 0)),
            _layer_spec(wq, layer, (d, width), lambda i: (0, 0)),
            pl.BlockSpec((1, width), lambda i: (0, 0)),
            pl.BlockSpec((n_mem, 2 * width), lambda i: (i // per_b, 0)),
            _layer_spec(wo, layer, (width, d), lambda i: (0, 0)),
            pl.BlockSpec((1, d), lambda i: (0, 0)),
        ],
        out_specs=[pl.BlockSpec((tm, d), lambda i: (i, 0)), pl.BlockSpec((tm, d), lambda i: (i, 0))],
        out_shape=[jax.ShapeDtypeStruct((t, d), F32), jax.ShapeDtypeStruct((t, d), BF16)],
        compiler_params=_cparams(("parallel",)),
        name="xattn",
    )(x, g, wq, gq, kv, wo, g_next)


def _ret_state_kernel(k_ref, v_ref, lgb_ref, sb_ref, state_ref, kd_ref, *, chunk):
    c = pl.program_id(1)
    per_step = k_ref.shape[1] // chunk
    scale = HEAD_DIM ** -0.5

    @pl.when(c == 0)
    def _():
        state_ref[...] = jnp.zeros_like(state_ref)
        pos = lax.broadcasted_iota(jnp.int32, (chunk, HEAD_DIM), 0).astype(F32)
        for h in range(N_HEADS):
            kd_ref[h] = jnp.exp(lgb_ref[h:h + 1, :] * pos) * scale

    for u in reversed(range(per_step)):
        rows = slice(u * chunk, (u + 1) * chunk)
        for h in range(N_HEADS):
            sl = slice(h * HEAD_DIM, (h + 1) * HEAD_DIM)
            st = state_ref[h]
            sb_ref[0, u, h] = st.astype(BF16)
            kd = (k_ref[0, rows, sl].astype(F32) * kd_ref[h]).astype(BF16)
            cdec = jnp.exp(lgb_ref[h:h + 1, :] * float(chunk))
            state_ref[h] = st * cdec + _dot_tn(kd, v_ref[0, rows, sl])


def ret_states(proj, lgb, *, chunk, per_step):
    b, s, _ = proj.shape
    nc = s // chunk
    assert nc % per_step == 0
    ns = nc // per_step
    width = N_HEADS * HEAD_DIM
    return pl.pallas_call(
        functools.partial(_ret_state_kernel, chunk=chunk),
        grid=(b, ns),
        in_specs=[
            pl.BlockSpec((1, per_step * chunk, width), lambda bi, c: (bi, ns - 1 - c, 1)),
            pl.BlockSpec((1, per_step * chunk, width), lambda bi, c: (bi, ns - 1 - c, 2)),
            pl.BlockSpec((N_HEADS, LANE), lambda bi, c: (0, 0)),
        ],
        out_specs=pl.BlockSpec((1, per_step, N_HEADS, HEAD_DIM, HEAD_DIM),
                               lambda bi, c: (bi, ns - 1 - c, 0, 0, 0)),
        out_shape=jax.ShapeDtypeStruct((b, nc, N_HEADS, HEAD_DIM, HEAD_DIM), BF16),
        scratch_shapes=[pltpu.VMEM((N_HEADS, HEAD_DIM, HEAD_DIM), F32),
                        pltpu.VMEM((N_HEADS, chunk, HEAD_DIM), F32)],
        compiler_params=_cparams(("parallel", "arbitrary")),
        name="ret_states",
    )(proj, proj, lgb)


def _ret_out_kernel(q_ref, k_ref, v_ref, rg_ref, sb_ref, lgf_ref, lgb_ref, og_ref, o_ref,
                    state_ref, dmat_ref, qdf_ref, qdb_ref, kdf_ref):
    c = pl.program_id(1)
    chunk = q_ref.shape[1]
    scale = HEAD_DIM ** -0.5

    @pl.when(c == 0)
    def _():
        state_ref[...] = jnp.zeros_like(state_ref)
        pos = lax.broadcasted_iota(jnp.int32, (chunk, HEAD_DIM), 0).astype(F32)
        ri = lax.broadcasted_iota(jnp.int32, (chunk, chunk), 0)
        ci = lax.broadcasted_iota(jnp.int32, (chunk, chunk), 1)
        diff = (ri - ci).astype(F32)
        for h in range(N_HEADS):
            lf = lgf_ref[h:h + 1, :]
            lb = lgb_ref[h:h + 1, :]
            dfwd = jnp.exp(lf[:, :1] * jnp.maximum(diff, 0.0))
            dbwd = jnp.exp(lb[:, :1] * jnp.maximum(-diff, 0.0))
            dmat_ref[h] = jnp.where(diff >= 0.0, dfwd, dbwd) * scale
            qdf_ref[h] = jnp.exp(lf * (pos + 1.0))
            qdb_ref[h] = jnp.exp(lb * (float(chunk) - pos))
            kdf_ref[h] = jnp.exp(lf * (float(chunk) - 1.0 - pos)) * scale

    for h in range(N_HEADS):
        sl = slice(h * HEAD_DIM, (h + 1) * HEAD_DIM)
        qh = q_ref[0, :, sl]
        kh = k_ref[0, :, sl]
        vh = v_ref[0, :, sl]
        qf32 = qh.astype(F32)
        sd = (_dot_nt(qh, kh) * dmat_ref[h]).astype(BF16)
        st = state_ref[h]
        out = jnp.dot(sd, vh, preferred_element_type=F32)
        out += jnp.dot((qf32 * qdf_ref[h]).astype(BF16), st.astype(BF16), preferred_element_type=F32)
        out += jnp.dot((qf32 * qdb_ref[h]).astype(BF16), sb_ref[0, 0, h], preferred_element_type=F32)
        gate = rg_ref[0, :, sl].astype(F32)
        y = _rms(out) * og_ref[...]
        o_ref[0, :, sl] = (y * (gate * jax.nn.sigmoid(gate))).astype(o_ref.dtype)
        kd = (kh.astype(F32) * kdf_ref[h]).astype(BF16)
        cdec = jnp.exp(lgf_ref[h:h + 1, :] * float(chunk))
        state_ref[h] = st * cdec + _dot_tn(kd, vh)


def ret_out(proj, sb, lgf, lgb, og, *, chunk):
    b, s, _ = proj.shape
    nc = s // chunk
    width = N_HEADS * HEAD_DIM
    col = lambda k: pl.BlockSpec((1, chunk, width), lambda bi, c, k=k: (bi, c, k))
    return pl.pallas_call(
        _ret_out_kernel,
        grid=(b, nc),
        in_specs=[
            col(0), col(1), col(2), col(3),
            pl.BlockSpec((1, 1, N_HEADS, HEAD_DIM, HEAD_DIM), lambda bi, c: (bi, c, 0, 0, 0)),
            pl.BlockSpec((N_HEADS, LANE), lambda bi, c: (0, 0)),
            pl.BlockSpec((N_HEADS, LANE), lambda bi, c: (0, 0)),
            pl.BlockSpec((1, HEAD_DIM), lambda bi, c: (0, 0)),
        ],
        out_specs=pl.BlockSpec((1, chunk, width), lambda bi, c: (bi, c, 0)),
        out_shape=jax.ShapeDtypeStruct((b, s, width), BF16),
        scratch_shapes=[pltpu.VMEM((N_HEADS, HEAD_DIM, HEAD_DIM), F32),
                        pltpu.VMEM((N_HEADS, chunk, chunk), F32),
                        pltpu.VMEM((N_HEADS, chunk, HEAD_DIM), F32),
                        pltpu.VMEM((N_HEADS, chunk, HEAD_DIM), F32),
                        pltpu.VMEM((N_HEADS, chunk, HEAD_DIM), F32)],
        compiler_params=_cparams(("parallel", "arbitrary")),
        name="ret_out",
    )(proj, proj, proj, proj, sb, lgf, lgb, og)


N_DR = 2 * NA_WIN_ROWS - 1
N_DC = 2 * NA_WIN_COLS - 1


def _na_kernel(rpb_ref, q_ref, kp_ref, kc_ref, kn_ref, vp_ref, vc_ref, vn_ref, o_ref,
               bcol_ref, slab_ref, *, rows):
    qb = pl.program_id(1)
    nb = rows // NA_QROWS
    neg_tile = jnp.full((GRID_W, GRID_W), NEG_BIG, F32)

    @pl.when(qb == 0)
    def _():
        qc = lax.broadcasted_iota(jnp.int32, (GRID_W, GRID_W), 0)
        kc = lax.broadcasted_iota(jnp.int32, (GRID_W, GRID_W), 1)
        cs = jnp.clip(qc - NA_WIN_COLS // 2, 0, GRID_W - NA_WIN_COLS)
        col_ok = jnp.logical_and(kc >= cs, kc < cs + NA_WIN_COLS)
        dc = kc - qc + (NA_WIN_COLS - 1)
        for h in range(N_HEADS):
            for dr in range(N_DR):
                tile = neg_tile
                for d in range(N_DC):
                    tile = jnp.where(dc == d, rpb_ref[(h * N_DR + dr) * N_DC + d], tile)
                bcol_ref[h, dr] = jnp.where(col_ok, tile, NEG_BIG)

    def assemble(qb_static):
        for h in range(N_HEADS):
            for j in range(NA_QROWS):
                r = NA_QROWS * qb_static + j
                rs = min(max(r - NA_WIN_ROWS // 2, 0), rows - NA_WIN_ROWS)
                for i2 in range(0, 3 * NA_QROWS, 2):
                    halves = []
                    for i in (i2, i2 + 1):
                        kr = NA_QROWS * (qb_static - 1) + i
                        ok = rs <= kr < rs + NA_WIN_ROWS
                        halves.append(bcol_ref[h, kr - r + NA_WIN_ROWS - 1] if ok else neg_tile)
                    slab_ref[h, j * GRID_W:(j + 1) * GRID_W, i2 * GRID_W:(i2 + 2) * GRID_W] = (
                        jnp.concatenate(halves, axis=1))

    for qb_static in (0, 1, nb - 1):
        pl.when(qb == qb_static)(functools.partial(assemble, qb_static))

    for h in range(N_HEADS):
        sl = slice(h * HEAD_DIM, (h + 1) * HEAD_DIM)
        k = jnp.concatenate([kp_ref[0, :, sl], kc_ref[0, :, sl], kn_ref[0, :, sl]], axis=0)
        v = jnp.concatenate([vp_ref[0, :, sl], vc_ref[0, :, sl], vn_ref[0, :, sl]], axis=0)
        s = _dot_nt(q_ref[0, :, sl], k) + slab_ref[h]
        p = jnp.exp(s - jnp.max(s, axis=-1, keepdims=True))
        l = jnp.sum(p, axis=-1, keepdims=True)
        o = jnp.dot(p.astype(BF16), v, preferred_element_type=F32) / l
        o_ref[0, :, sl] = o.astype(o_ref.dtype)


def neighborhood_attention(proj, rpb):
    b, s, _ = proj.shape
    rows = s // GRID_W
    assert NA_QROWS >= NA_WIN_ROWS // 2 and NA_QROWS % 2 == 0 and rows % NA_QROWS == 0
    assert rows >= 3 * NA_QROWS and rows >= NA_WIN_ROWS
    nq = NA_QROWS * GRID_W
    nb = rows // NA_QROWS
    width = N_HEADS * HEAD_DIM

    def blk(group, shift):
        return pl.BlockSpec((1, nq, width),
                            lambda bi, qb, rp: (bi, jnp.clip(qb + shift, 0, nb - 1), group))

    grid_spec = pltpu.PrefetchScalarGridSpec(
        num_scalar_prefetch=1,
        grid=(b, nb),
        in_specs=[blk(4, 0), blk(5, -1), blk(5, 0), blk(5, 1), blk(6, -1), blk(6, 0), blk(6, 1)],
        out_specs=pl.BlockSpec((1, nq, width), lambda bi, qb, rp: (bi, qb, 0)),
        scratch_shapes=[pltpu.VMEM((N_HEADS, N_DR, GRID_W, GRID_W), F32),
                        pltpu.VMEM((N_HEADS, nq, 3 * nq), F32)],
    )
    return pl.pallas_call(
        functools.partial(_na_kernel, rows=rows),
        grid_spec=grid_spec,
        out_shape=jax.ShapeDtypeStruct((b, s, width), BF16),
        compiler_params=_cparams(("arbitrary", "arbitrary")),
        name="neighborhood_attention",
    )(rpb.astype(F32).reshape(-1), proj, proj, proj, proj, proj, proj, proj)


def _bf16_pieces(x, n):
    out, rest = [], float(x)
    for _ in range(n):
        piece = float(np.float32(rest).astype(BF16).astype(np.float32))
        out.append(piece)
        rest -= piece
    return out


LOG2E = math.log2(math.e)
LOG2E_PIECES = _bf16_pieces(LOG2E, 3)
N_ALIBI_COLS = 12
POS_LO_BITS = 7


def _alibi_cols(pos, slope, lane, unit_cols, sign):
    hi = (pos >> POS_LO_BITS).astype(F32) * (slope * float(1 << POS_LO_BITS))
    lo = (pos & ((1 << POS_LO_BITS) - 1)).astype(F32) * slope
    if sign > 0:
        return jnp.where(lane < 3, hi, jnp.where(lane < 6, lo, unit_cols))
    return jnp.where(lane < 6, unit_cols, jnp.where(lane < 9, hi, jnp.where(lane < N_ALIBI_COLS, lo, 0.0)))


def _diff_attn_kernel(slopes_ref, q_ref, k_ref, v_ref, cq_ref, ck_ref, lam_ref, og_ref, o_ref,
                      kaug_ref, qaug_ref, s_ref, bias_ref, acc_ref, *, lam_init, nsub, npv):
    h = pl.program_id(1)
    qi = pl.program_id(2)
    t = q_ref.shape[1]
    nk = k_ref.shape[1] // t
    nslab = t // LANE
    slope = slopes_ref[h]
    lane = lax.broadcasted_iota(jnp.int32, (t, LANE), 1)
    row = lax.broadcasted_iota(jnp.int32, (t, LANE), 0)

    @pl.when(qi == 0)
    def _():
        def body(r, carry):
            rs = pl.multiple_of(r * t, t)
            kx = _alibi_cols(row + rs, slope, lane, ck_ref[...], 1).astype(BF16)
            for c in range(2):
                kaug_ref[c, pl.ds(rs, t), 0:HEAD_DIM] = k_ref[0, pl.ds(rs, t), c * HEAD_DIM:(c + 1) * HEAD_DIM]
                kaug_ref[c, pl.ds(rs, t), HEAD_DIM:2 * HEAD_DIM] = kx
            return carry

        lax.fori_loop(0, nk, body, 0)
        ri = lax.broadcasted_iota(jnp.int32, (t, t), 0)
        ci = lax.broadcasted_iota(jnp.int32, (t, t), 1)
        bias_ref[0] = jnp.zeros((t, t), F32)
        bias_ref[1] = jnp.abs(ri - ci).astype(F32) * (-LOG2E * slope)

    qx = _alibi_cols(row + qi * t, slope, lane, cq_ref[...], -1)
    for c in range(2):
        qc = q_ref[0, :, c * HEAD_DIM:(c + 1) * HEAD_DIM]
        for side, ext in ((0, qx), (1, -qx), (2, jnp.zeros_like(qx))):
            qaug_ref[c, side, :, 0:HEAD_DIM] = qc
            qaug_ref[c, side, :, HEAD_DIM:2 * HEAD_DIM] = ext.astype(BF16)

    width = npv * t
    ngroups = nk // npv
    outs = []
    for c in range(2):
        def scores(g, m):
            for i in range(nsub):
                kb = g * nsub + i
                side = jnp.where(kb < qi, 0, jnp.where(kb > qi, 1, 2))
                ks = pl.multiple_of(kb * t, t)
                s = _dot_nt(qaug_ref[c, side], kaug_ref[c, pl.ds(ks, t), :])
                s = s + bias_ref[(kb == qi).astype(jnp.int32)]
                s_ref[kb] = s
                for u in range(nslab):
                    m = jnp.maximum(m, s[:, u * LANE:(u + 1) * LANE])
            return m

        m = lax.fori_loop(0, nk // nsub, scores, jnp.full((t, LANE), NEG_BIG, F32))
        mb = jnp.broadcast_to(jnp.max(m, axis=-1, keepdims=True), (t, LANE))
        if ngroups > 1:
            acc_ref[...] = jnp.zeros_like(acc_ref)

        def pv(g, lrun):
            ps = []
            for i in range(npv):
                for u in range(nslab):
                    p = jnp.exp2(s_ref[g * npv + i, :, u * LANE:(u + 1) * LANE] - mb)
                    lrun = lrun + p
                    ps.append(p.astype(BF16))
            vs = v_ref[0, pl.ds(pl.multiple_of(g * width, width), width), :]
            pv_part = jnp.dot(jnp.concatenate(ps, axis=1), vs, preferred_element_type=F32)
            acc_ref[...] = pv_part if ngroups == 1 else acc_ref[...] + pv_part
            return lrun

        trips = ngroups + jnp.minimum(qi, 0)
        lrun = lax.fori_loop(0, trips, pv, jnp.zeros((t, LANE), F32))
        outs.append(acc_ref[...] / jnp.sum(lrun, axis=-1, keepdims=True))

    lam = (jnp.exp(jnp.sum(lam_ref[0:1, :] * lam_ref[1:2, :], axis=-1, keepdims=True))
           - jnp.exp(jnp.sum(lam_ref[2:3, :] * lam_ref[3:4, :], axis=-1, keepdims=True)) + lam_init)
    o = outs[0] - lam * outs[1]
    o_ref[0] = (_rms(o) * og_ref[...] * (1.0 - lam_init)).astype(o_ref.dtype)


def diff_attention(proj, lam_params, og, *, lam_init, t):
    b, s, _ = proj.shape
    assert s % t == 0 and s <= (1 << (2 * POS_LO_BITS + 1)) and t % LANE == 0
    nk = s // t
    nsub = max(n for n in (16, 8, 4, 3, 2, 1) if nk % n == 0)
    npv = max(n for n in (16, 8, 4, 3, 2, 1) if nk % n == 0)
    vw = 2 * HEAD_DIM
    slopes = jnp.asarray(2.0 ** (-8.0 * np.arange(1, N_HEADS + 1) / N_HEADS), F32)
    pieces = np.zeros((1, LANE), np.float32)
    pieces[0, :6] = LOG2E_PIECES * 2
    cq = jnp.asarray(pieces)
    ck = jnp.asarray(-np.roll(pieces, 6, axis=1))
    const = lambda shape: pl.BlockSpec(shape, lambda bi, h, qi, sl: (0, 0))
    grid_spec = pltpu.PrefetchScalarGridSpec(
        num_scalar_prefetch=1,
        grid=(b, N_HEADS, nk),
        in_specs=[
            pl.BlockSpec((1, t, vw), lambda bi, h, qi, sl: (bi, qi, h)),
            pl.BlockSpec((1, s, vw), lambda bi, h, qi, sl: (bi, 0, N_HEADS + h)),
            pl.BlockSpec((1, s, vw), lambda bi, h, qi, sl: (bi, 0, 2 * N_HEADS + h)),
            const((1, LANE)), const((1, LANE)), const((4, HEAD_DIM)), const((1, vw)),
        ],
        out_specs=pl.BlockSpec((1, t, vw), lambda bi, h, qi, sl: (bi, qi, h)),
        scratch_shapes=[pltpu.VMEM((2, s, vw), BF16),
                        pltpu.VMEM((2, 3, t, vw), BF16),
                        pltpu.VMEM((nk, t, t), F32),
                        pltpu.VMEM((2, t, t), F32),
                        pltpu.VMEM((t, vw), F32)],
    )
    return pl.pallas_call(
        functools.partial(_diff_attn_kernel, lam_init=lam_init, nsub=nsub, npv=npv),
        grid_spec=grid_spec,
        out_shape=jax.ShapeDtypeStruct((b, s, N_HEADS * vw), BF16),
        compiler_params=_cparams(("arbitrary", "arbitrary", "arbitrary")),
        name="diff_attention",
    )(slopes, proj, proj, proj, cq, ck, lam_params, og)


def _row(v):
    return v.astype(F32).reshape(1, -1)


def _lane_bcast(v):
    return jnp.broadcast_to(v.astype(F32)[:, None], (v.shape[0], LANE))


def kernel(x, mem, norm_mix_g, norm_xattn_g, norm_mem_g, norm_ffn_g, w_in_ab, ret_decay_fwd, ret_decay_bwd, ret_out_g, na_q_g, na_k_g, na_rpb, w_out_ab, w_in_c, diff_q_g, diff_k_g, lambda_q1, lambda_k1, lambda_q2, lambda_k2, diff_out_g, w_out_c, w_xq, w_xkv, w_xo, xq_g, xk_g, w_ffn_in, w_ffn_out):
    b, s, d = x.shape
    n_mem = mem.shape[1]
    depth = norm_mix_g.shape[0]
    t = b * s
    scale = HEAD_DIM ** -0.5
    width = N_HEADS * HEAD_DIM
    xw = N_X_HEADS * HEAD_DIM
    tm, tn, tn_in = ROW_TILE, COL_TILE, MIXER_COL_TILE

    xf = x.reshape(t, d)
    memf = mem.reshape(b * n_mem, d)
    ones_cols = lambda n: jnp.ones((1, n), F32)
    w_ffn_out_bf16 = w_ffn_out.astype(BF16)

    for i in range(depth):
        j = i // 2
        if i % 2 == 0:
            gcol = jnp.concatenate([
                ones_cols(4 * width),
                jnp.tile(_row(na_q_g[j]) * scale, (1, N_HEADS)),
                jnp.tile(_row(na_k_g[j]), (1, N_HEADS)),
                ones_cols(width)], axis=1)
            proj = norm_matmul(xf, _row(norm_mix_g[i]), w_in_ab, j, gcol,
                               4 * width // tn_in, 6 * width // tn_in, tm=tm, tn=tn_in)
            proj = proj.reshape(b, s, 7 * width)
            lgf = _lane_bcast(jax.nn.log_sigmoid(ret_decay_fwd[j].astype(F32)))
            lgb = _lane_bcast(jax.nn.log_sigmoid(ret_decay_bwd[j].astype(F32)))
            n_chunks = s // RET_CHUNK
            sb = ret_states(proj, lgb, chunk=RET_CHUNK, per_step=math.gcd(n_chunks, RET_STATE_CHUNKS_PER_STEP))
            ret = ret_out(proj, sb, lgf, lgb, _row(ret_out_g[j]), chunk=RET_CHUNK)
            na = neighborhood_attention(proj, na_rpb[j])
            mixed = [ret.reshape(t, width), na.reshape(t, width)]
            xf = matmul_res(mixed, w_out_ab, j, xf, tm=tm, tn=tn_in, weight_outer=True)
        else:
            cw = 2 * width
            gcol = jnp.concatenate([
                jnp.tile(_row(diff_q_g[j]) * (scale * LOG2E), (1, 2 * N_HEADS)),
                jnp.tile(_row(diff_k_g[j]), (1, 2 * N_HEADS)),
                ones_cols(cw)], axis=1)
            proj = norm_matmul(xf, _row(norm_mix_g[i]), w_in_c, j, gcol, 0, 2 * cw // tn_in, tm=tm, tn=tn_in)
            proj = proj.reshape(b, s, 3 * cw)
            lam_params = jnp.stack([lambda_q1[j], lambda_k1[j], lambda_q2[j], lambda_k2[j]]).astype(F32)
            lam_init = 0.8 - 0.6 * math.exp(-0.3 * i)
            att = diff_attention(proj, lam_params, _row(diff_out_g[j]), lam_init=lam_init, t=ATTN_TILE)
            xf = matmul_res([att.reshape(t, cw)], w_out_c, j, xf, tm=tm, tn=tn_in, weight_outer=True)

        kv_gcol = jnp.concatenate([jnp.tile(_row(xk_g[i]), (1, N_X_HEADS)), ones_cols(xw)], axis=1)
        kv = norm_matmul(memf, _row(norm_mem_g[i]), w_xkv, i, kv_gcol, 0, 1, tm=b * n_mem, tn=xw)
        gq = jnp.tile(_row(xq_g[i]) * scale, (1, N_X_HEADS))
        xf, xn_ffn = xattn(xf, _row(norm_xattn_g[i]), w_xq, gq, kv, w_xo, i, _row(norm_ffn_g[i]),
                           tm=XATTN_ROW_TILE, seq=s)

        hmid = swiglu_in(xn_ffn, w_ffn_in, i, tm=tm, tn=tn)
        xf = matmul_res([hmid], w_ffn_out_bf16, i, xf, tm=tm, tn=tn)

    return xf.reshape(b, s, d)
```

```python
import functools
import math

import jax
import jax.numpy as jnp
import numpy as np
from jax import lax
from jax.experimental import pallas as pl
from jax.experimental.pallas import tpu as pltpu

F32 = jnp.float32
BF16 = jnp.bfloat16

HEAD_DIM = 128
N_HEADS = 8
N_X_HEADS = 4
GRID_W = 64
NA_WIN_ROWS = 8
NA_WIN_COLS = 16
NA_QROWS = 4
RET_CHUNK = 256
RET_STATE_CHUNKS_PER_STEP = 4

ROW_TILE = 1024
COL_TILE = 512
MIXER_COL_TILE = 1024
XATTN_ROW_TILE = 512
ATTN_TILE = 512
NORM_ROW_CHUNK = 128
X_TILE_SLABS = 4
RMS_EPS = 1e-6
NEG_BIG = -1e30

LANE = 128
VMEM_LIMIT = 56 * 1024 * 1024


def _cparams(sem):
    return pltpu.CompilerParams(dimension_semantics=sem, vmem_limit_bytes=VMEM_LIMIT)


def _dot_nt(a, b):
    return lax.dot_general(a, b, (((1,), (1,)), ((), ())), preferred_element_type=F32)


def _dot_nn(a, b):
    return lax.dot_general(a, b, (((1,), (0,)), ((), ())), preferred_element_type=F32)


def _dot_tn(a, b):
    return lax.dot_general(a, b, (((0,), (0,)), ((), ())), preferred_element_type=F32)


def _layer_spec(w, layer, block, index_map):
    if w.ndim == 2:
        return pl.BlockSpec(block, index_map)
    return pl.BlockSpec((None,) + tuple(block), lambda *idx: (layer,) + tuple(index_map(*idx)))


def _rms(x):
    return x * lax.rsqrt(jnp.mean(x * x, axis=-1, keepdims=True) + RMS_EPS)


def _norm_rows_to_scratch(x_refs, g_ref, xn_ref, rows):
    slab = x_refs[0].shape[0]
    for q, x_ref in enumerate(x_refs):
        def body(r, c, x_ref=x_ref, q=q):
            src = pl.ds(pl.multiple_of(r * rows, rows), rows)
            dst = pl.ds(pl.multiple_of(q * slab + r * rows, rows), rows)
            xn_ref[dst, :] = (_rms(x_ref[src, :]) * g_ref[...]).astype(BF16)
            return c

        lax.fori_loop(0, slab // rows, body, 0)


def _norm_matmul_kernel(*refs, lo, hi):
    *x_refs, g_ref, w_ref, gc_ref, o_ref, xn_ref = refs
    j = pl.program_id(1)

    @pl.when(j == 0)
    def _():
        _norm_rows_to_scratch(x_refs, g_ref, xn_ref, NORM_ROW_CHUNK)

    grouped = jnp.logical_and(j >= lo, j < hi)

    @pl.when(grouped)
    def _():
        acc = _dot_nn(xn_ref[...], w_ref[...])
        for c in range(acc.shape[1] // HEAD_DIM):
            sl = slice(c * HEAD_DIM, (c + 1) * HEAD_DIM)
            o_ref[:, sl] = (_rms(acc[:, sl]) * gc_ref[:, sl]).astype(o_ref.dtype)

    @pl.when(jnp.logical_not(grouped))
    def _():
        o_ref[...] = _dot_nn(xn_ref[...], w_ref[...]).astype(o_ref.dtype)


def norm_matmul(x, g, w, layer, gcol, lo, hi, *, tm, tn, out_dtype=BF16):
    t, d = x.shape
    n = w.shape[-1]
    ni, nj = t // tm, n // tn
    nslabs = X_TILE_SLABS if nj > X_TILE_SLABS else 1
    slab = tm // nslabs

    def x_spec(q):
        first_next = nj - nslabs + q

        def index(i, j):
            tile = i if nslabs == 1 else jnp.minimum(i + (j >= first_next).astype(jnp.int32), ni - 1)
            return tile * nslabs + q, 0
        return pl.BlockSpec((slab, d), index)

    return pl.pallas_call(
        functools.partial(_norm_matmul_kernel, lo=lo, hi=hi),
        grid=(ni, nj),
        in_specs=[x_spec(q) for q in range(nslabs)] + [
            pl.BlockSpec((1, d), lambda i, j: (0, 0)),
            _layer_spec(w, layer, (d, tn), lambda i, j: (0, j)),
            pl.BlockSpec((1, tn), lambda i, j: (0, j)),
        ],
        out_specs=pl.BlockSpec((tm, tn), lambda i, j: (i, j)),
        out_shape=jax.ShapeDtypeStruct((t, n), out_dtype),
        scratch_shapes=[pltpu.VMEM((tm, d), BF16)],
        compiler_params=_cparams(("arbitrary", "arbitrary")),
        name="norm_matmul",
    )(*([x] * nslabs), g, w, gcol)


def _swiglu_in_kernel(xn_ref, wg_ref, wu_ref, o_ref):
    xn = xn_ref[...]
    gate = _dot_nn(xn, wg_ref[...])
    up = _dot_nn(xn, wu_ref[...])
    o_ref[...] = (gate * jax.nn.sigmoid(gate) * up).astype(o_ref.dtype)


def swiglu_in(xn, w_in, layer, *, tm, tn):
    t, d = xn.shape
    hidden = w_in.shape[-1] // 2
    nh = hidden // tn
    return pl.pallas_call(
        _swiglu_in_kernel,
        grid=(t // tm, nh),
        in_specs=[
            pl.BlockSpec((tm, d), lambda i, j: (i, 0)),
            _layer_spec(w_in, layer, (d, tn), lambda i, j: (0, j)),
            _layer_spec(w_in, layer, (d, tn), lambda i, j: (0, j + nh)),
        ],
        out_specs=pl.BlockSpec((tm, tn), lambda i, j: (i, j)),
        out_shape=jax.ShapeDtypeStruct((t, hidden), BF16),
        compiler_params=_cparams(("parallel", "arbitrary")),
        name="swiglu_in",
    )(xn, w_in, w_in)


def _matmul_res_kernel(*refs):
    *a_refs, w_ref, r_ref, o_ref = refs
    acc = r_ref[...]
    k0 = 0
    for a_ref in a_refs:
        k1 = k0 + a_ref.shape[1]
        acc += _dot_nn(a_ref[...], w_ref[k0:k1, :])
        k0 = k1
    o_ref[...] = acc


def matmul_res(a_parts, w, layer, res, *, tm, tn, weight_outer=False):
    t = res.shape[0]
    k, n = w.shape[-2:]
    assert sum(a.shape[1] for a in a_parts) == k
    if weight_outer:
        grid, ij = (n // tn, t // tm), (lambda j, i: (i, j))
    else:
        grid, ij = (t // tm, n // tn), (lambda i, j: (i, j))
    row = lambda *g: (ij(*g)[0], 0)
    col = lambda *g: (0, ij(*g)[1])
    return pl.pallas_call(
        _matmul_res_kernel,
        grid=grid,
        in_specs=[pl.BlockSpec((tm, a.shape[1]), row) for a in a_parts] + [
            _layer_spec(w, layer, (k, tn), col),
            pl.BlockSpec((tm, tn), ij),
        ],
        out_specs=pl.BlockSpec((tm, tn), ij),
        out_shape=jax.ShapeDtypeStruct((t, n), F32),
        compiler_params=_cparams(("arbitrary", "arbitrary")),
        name="matmul_res",
    )(*a_parts, w, res)


def _xattn_kernel(x_ref, g_ref, wq_ref, gq_ref, kv_ref, wo_ref, gn_ref, o_ref, on_ref):
    x = x_ref[...]
    xn = (_rms(x) * g_ref[...]).astype(BF16)
    q = _dot_nn(xn, wq_ref[...])
    width = N_X_HEADS * HEAD_DIM
    outs = []
    for h in range(N_X_HEADS):
        sl = slice(h * HEAD_DIM, (h + 1) * HEAD_DIM)
        qh = (_rms(q[:, sl]) * gq_ref[:, sl]).astype(BF16)
        s = _dot_nt(qh, kv_ref[:, sl])
        p = jnp.exp(s - jnp.max(s, axis=-1, keepdims=True))
        l = jnp.sum(p, axis=-1, keepdims=True)
        vh = kv_ref[:, width + h * HEAD_DIM: width + (h + 1) * HEAD_DIM]
        oh = jnp.dot(p.astype(BF16), vh, preferred_element_type=F32) / l
        outs.append(oh.astype(BF16))
    o = jnp.concatenate(outs, axis=1)
    y = x + _dot_nn(o, wo_ref[...])
    o_ref[...] = y
    on_ref[...] = (_rms(y) * gn_ref[...]).astype(on_ref.dtype)


def xattn(x, g, wq, gq, kv, wo, layer, g_next, *, tm, seq):
    t, d = x.shape
    n_mem = kv.shape[0] // (t // seq)
    width = wq.shape[-1]
    per_b = seq // tm
    return pl.pallas_call(
        _xattn_kernel,
        grid=(t // tm,),
        in_specs=[
            pl.BlockSpec((tm, d), lambda i: (i, 0)),
            pl.BlockSpec((1, d), lambda i: (0, 0)),
            _layer_spec(wq, layer, (d, width), lambda i: (0, 0)),
            pl.BlockSpec((1, width), lambda i: (0, 0)),
            pl.BlockSpec((n_mem, 2 * width), lambda i: (i // per_b, 0)),
            _layer_spec(wo, layer, (width, d), lambda i: (0, 0)),
            pl.BlockSpec((1, d), lambda i: (0, 0)),
        ],
        out_specs=[pl.BlockSpec((tm, d), lambda i: (i, 0)), pl.BlockSpec((tm, d), lambda i: (i, 0))],
        out_shape=[jax.ShapeDtypeStruct((t, d), F32), jax.ShapeDtypeStruct((t, d), BF16)],
        compiler_params=_cparams(("parallel",)),
        name="xattn",
    )(x, g, wq, gq, kv, wo, g_next)


def _ret_state_kernel(k_ref, v_ref, lgb_ref, sb_ref, state_ref, kd_ref, *, chunk):
    c = pl.program_id(1)
    per_step = k_ref.shape[1] // chunk
    scale = HEAD_DIM ** -0.5

    @pl.when(c == 0)
    def _():
        state_ref[...] = jnp.zeros_like(state_ref)
        pos = lax.broadcasted_iota(jnp.int32, (chunk, HEAD_DIM), 0).astype(F32)
        for h in range(N_HEADS):
            kd_ref[h] = jnp.exp(lgb_ref[h:h + 1, :] * pos) * scale

    for u in reversed(range(per_step)):
        rows = slice(u * chunk, (u + 1) * chunk)
        for h in range(N_HEADS):
            sl = slice(h * HEAD_DIM, (h + 1) * HEAD_DIM)
            st = state_ref[h]
            sb_ref[0, u, h] = st.astype(BF16)
            kd = (k_ref[0, rows, sl].astype(F32) * kd_ref[h]).astype(BF16)
            cdec = jnp.exp(lgb_ref[h:h + 1, :] * float(chunk))
            state_ref[h] = st * cdec + _dot_tn(kd, v_ref[0, rows, sl])


def ret_states(proj, lgb, *, chunk, per_step):
    b, s, _ = proj.shape
    nc = s // chunk
    assert nc % per_step == 0
    ns = nc // per_step
    width = N_HEADS * HEAD_DIM
    return pl.pallas_call(
        functools.partial(_ret_state_kernel, chunk=chunk),
        grid=(b, ns),
        in_specs=[
            pl.BlockSpec((1, per_step * chunk, width), lambda bi, c: (bi, ns - 1 - c, 1)),
            pl.BlockSpec((1, per_step * chunk, width), lambda bi, c: (bi, ns - 1 - c, 2)),
            pl.BlockSpec((N_HEADS, LANE), lambda bi, c: (0, 0)),
        ],
        out_specs=pl.BlockSpec((1, per_step, N_HEADS, HEAD_DIM, HEAD_DIM),
                               lambda bi, c: (bi, ns - 1 - c, 0, 0, 0)),
        out_shape=jax.ShapeDtypeStruct((b, nc, N_HEADS, HEAD_DIM, HEAD_DIM), BF16),
        scratch_shapes=[pltpu.VMEM((N_HEADS, HEAD_DIM, HEAD_DIM), F32),
                        pltpu.VMEM((N_HEADS, chunk, HEAD_DIM), F32)],
        compiler_params=_cparams(("parallel", "arbitrary")),
        name="ret_states",
    )(proj, proj, lgb)


def _ret_out_kernel(q_ref, k_ref, v_ref, rg_ref, sb_ref, lgf_ref, lgb_ref, og_ref, o_ref,
                    state_ref, dmat_ref, qdf_ref, qdb_ref, kdf_ref):
    c = pl.program_id(1)
    chunk = q_ref.shape[1]
    scale = HEAD_DIM ** -0.5

    @pl.when(c == 0)
    def _():
        state_ref[...] = jnp.zeros_like(state_ref)
        pos = lax.broadcasted_iota(jnp.int32, (chunk, HEAD_DIM), 0).astype(F32)
        ri = lax.broadcasted_iota(jnp.int32, (chunk, chunk), 0)
        ci = lax.broadcasted_iota(jnp.int32, (chunk, chunk), 1)
        diff = (ri - ci).astype(F32)
        for h in range(N_HEADS):
            lf = lgf_ref[h:h + 1, :]
            lb = lgb_ref[h:h + 1, :]
            dfwd = jnp.exp(lf[:, :1] * jnp.maximum(diff, 0.0))
            dbwd = jnp.exp(lb[:, :1] * jnp.maximum(-diff, 0.0))
            dmat_ref[h] = jnp.where(diff >= 0.0, dfwd, dbwd) * scale
            qdf_ref[h] = jnp.exp(lf * (pos + 1.0))
            qdb_ref[h] = jnp.exp(lb * (float(chunk) - pos))
            kdf_ref[h] = jnp.exp(lf * (float(chunk) - 1.0 - pos)) * scale

    for h in range(N_HEADS):
        sl = slice(h * HEAD_DIM, (h + 1) * HEAD_DIM)
        qh = q_ref[0, :, sl]
        kh = k_ref[0, :, sl]
        vh = v_ref[0, :, sl]
        qf32 = qh.astype(F32)
        sd = (_dot_nt(qh, kh) * dmat_ref[h]).astype(BF16)
        st = state_ref[h]
        out = jnp.dot(sd, vh, preferred_element_type=F32)
        out += jnp.dot((qf32 * qdf_ref[h]).astype(BF16), st.astype(BF16), preferred_element_type=F32)
        out += jnp.dot((qf32 * qdb_ref[h]).astype(BF16), sb_ref[0, 0, h], preferred_element_type=F32)
        gate = rg_ref[0, :, sl].astype(F32)
        y = _rms(out) * og_ref[...]
        o_ref[0, :, sl] = (y * (gate * jax.nn.sigmoid(gate))).astype(o_ref.dtype)
        kd = (kh.astype(F32) * kdf_ref[h]).astype(BF16)
        cdec = jnp.exp(lgf_ref[h:h + 1, :] * float(chunk))
        state_ref[h] = st * cdec + _dot_tn(kd, vh)


def ret_out(proj, sb, lgf, lgb, og, *, chunk):
    b, s, _ = proj.shape
    nc = s // chunk
    width = N_HEADS * HEAD_DIM
    col = lambda k: pl.BlockSpec((1, chunk, width), lambda bi, c, k=k: (bi, c, k))
    return pl.pallas_call(
        _ret_out_kernel,
        grid=(b, nc),
        in_specs=[
            col(0), col(1), col(2), col(3),
            pl.BlockSpec((1, 1, N_HEADS, HEAD_DIM, HEAD_DIM), lambda bi, c: (bi, c, 0, 0, 0)),
            pl.BlockSpec((N_HEADS, LANE), lambda bi, c: (0, 0)),
            pl.BlockSpec((N_HEADS, LANE), lambda bi, c: (0, 0)),
            pl.BlockSpec((1, HEAD_DIM), lambda bi, c: (0, 0)),
        ],
        out_specs=pl.BlockSpec((1, chunk, width), lambda bi, c: (bi, c, 0)),
        out_shape=jax.ShapeDtypeStruct((b, s, width), BF16),
        scratch_shapes=[pltpu.VMEM((N_HEADS, HEAD_DIM, HEAD_DIM), F32),
                        pltpu.VMEM((N_HEADS, chunk, chunk), F32),
                        pltpu.VMEM((N_HEADS, chunk, HEAD_DIM), F32),
                        pltpu.VMEM((N_HEADS, chunk, HEAD_DIM), F32),
                        pltpu.VMEM((N_HEADS, chunk, HEAD_DIM), F32)],
        compiler_params=_cparams(("parallel", "arbitrary")),
        name="ret_out",
    )(proj, proj, proj, proj, sb, lgf, lgb, og)


N_DR = 2 * NA_WIN_ROWS - 1
N_DC = 2 * NA_WIN_COLS - 1


def _na_kernel(rpb_ref, q_ref, kp_ref, kc_ref, kn_ref, vp_ref, vc_ref, vn_ref, o_ref,
               bcol_ref, slab_ref, *, rows):
    qb = pl.program_id(1)
    nb = rows // NA_QROWS
    neg_tile = jnp.full((GRID_W, GRID_W), NEG_BIG, F32)

    @pl.when(qb == 0)
    def _():
        qc = lax.broadcasted_iota(jnp.int32, (GRID_W, GRID_W), 0)
        kc = lax.broadcasted_iota(jnp.int32, (GRID_W, GRID_W), 1)
        cs = jnp.clip(qc - NA_WIN_COLS // 2, 0, GRID_W - NA_WIN_COLS)
        col_ok = jnp.logical_and(kc >= cs, kc < cs + NA_WIN_COLS)
        dc = kc - qc + (NA_WIN_COLS - 1)
        for h in range(N_HEADS):
            for dr in range(N_DR):
                tile = neg_tile
                for d in range(N_DC):
                    tile = jnp.where(dc == d, rpb_ref[(h * N_DR + dr) * N_DC + d], tile)
                bcol_ref[h, dr] = jnp.where(col_ok, tile, NEG_BIG)

    def assemble(qb_static):
        for h in range(N_HEADS):
            for j in range(NA_QROWS):
                r = NA_QROWS * qb_static + j
                rs = min(max(r - NA_WIN_ROWS // 2, 0), rows - NA_WIN_ROWS)
                for i2 in range(0, 3 * NA_QROWS, 2):
                    halves = []
                    for i in (i2, i2 + 1):
                        kr = NA_QROWS * (qb_static - 1) + i
                        ok = rs <= kr < rs + NA_WIN_ROWS
                        halves.append(bcol_ref[h, kr - r + NA_WIN_ROWS - 1] if ok else neg_tile)
                    slab_ref[h, j * GRID_W:(j + 1) * GRID_W, i2 * GRID_W:(i2 + 2) * GRID_W] = (
                        jnp.concatenate(halves, axis=1))

    for qb_static in (0, 1, nb - 1):
        pl.when(qb == qb_static)(functools.partial(assemble, qb_static))

    for h in range(N_HEADS):
        sl = slice(h * HEAD_DIM, (h + 1) * HEAD_DIM)
        k = jnp.concatenate([kp_ref[0, :, sl], kc_ref[0, :, sl], kn_ref[0, :, sl]], axis=0)
        v = jnp.concatenate([vp_ref[0, :, sl], vc_ref[0, :, sl], vn_ref[0, :, sl]], axis=0)
        s = _dot_nt(q_ref[0, :, sl], k) + slab_ref[h]
        p = jnp.exp(s - jnp.max(s, axis=-1, keepdims=True))
        l = jnp.sum(p, axis=-1, keepdims=True)
        o = jnp.dot(p.astype(BF16), v, preferred_element_type=F32) / l
        o_ref[0, :, sl] = o.astype(o_ref.dtype)


def neighborhood_attention(proj, rpb):
    b, s, _ = proj.shape
    rows = s // GRID_W
    assert NA_QROWS >= NA_WIN_ROWS // 2 and NA_QROWS % 2 == 0 and rows % NA_QROWS == 0
    assert rows >= 3 * NA_QROWS and rows >= NA_WIN_ROWS
    nq = NA_QROWS * GRID_W
    nb = rows // NA_QROWS
    width = N_HEADS * HEAD_DIM

    def blk(group, shift):
        return pl.BlockSpec((1, nq, width),
                            lambda bi, qb, rp: (bi, jnp.clip(qb + shift, 0, nb - 1), group))

    grid_spec = pltpu.PrefetchScalarGridSpec(
        num_scalar_prefetch=1,
        grid=(b, nb),
        in_specs=[blk(4, 0), blk(5, -1), blk(5, 0), blk(5, 1), blk(6, -1), blk(6, 0), blk(6, 1)],
        out_specs=pl.BlockSpec((1, nq, width), lambda bi, qb, rp: (bi, qb, 0)),
        scratch_shapes=[pltpu.VMEM((N_HEADS, N_DR, GRID_W, GRID_W), F32),
                        pltpu.VMEM((N_HEADS, nq, 3 * nq), F32)],
    )
    return pl.pallas_call(
        functools.partial(_na_kernel, rows=rows),
        grid_spec=grid_spec,
        out_shape=jax.ShapeDtypeStruct((b, s, width), BF16),
        compiler_params=_cparams(("arbitrary", "arbitrary")),
        name="neighborhood_attention",
    )(rpb.astype(F32).reshape(-1), proj, proj, proj, proj, proj, proj, proj)


def _bf16_pieces(x, n):
    out, rest = [], float(x)
    for _ in range(n):
        piece = float(np.float32(rest).astype(BF16).astype(np.float32))
        out.append(piece)
        rest -= piece
    return out


LOG2E = math.log2(math.e)
LOG2E_PIECES = _bf16_pieces(LOG2E, 3)
N_ALIBI_COLS = 12
POS_LO_BITS = 7


def _alibi_cols(pos, slope, lane, unit_cols, sign):
    hi = (pos >> POS_LO_BITS).astype(F32) * (slope * float(1 << POS_LO_BITS))
    lo = (pos & ((1 << POS_LO_BITS) - 1)).astype(F32) * slope
    if sign > 0:
        return jnp.where(lane < 3, hi, jnp.where(lane < 6, lo, unit_cols))
    return jnp.where(lane < 6, unit_cols, jnp.where(lane < 9, hi, jnp.where(lane < N_ALIBI_COLS, lo, 0.0)))


def _diff_attn_kernel(slopes_ref, q_ref, k_ref, v_ref, cq_ref, ck_ref, lam_ref, og_ref, o_ref,
                      kaug_ref, qaug_ref, s_ref, bias_ref, acc_ref, *, lam_init, nsub, npv):
    h = pl.program_id(1)
    qi = pl.program_id(2)
    t = q_ref.shape[1]
    nk = k_ref.shape[1] // t
    nslab = t // LANE
    slope = slopes_ref[h]
    lane = lax.broadcasted_iota(jnp.int32, (t, LANE), 1)
    row = lax.broadcasted_iota(jnp.int32, (t, LANE), 0)

    @pl.when(qi == 0)
    def _():
        def body(r, carry):
            rs = pl.multiple_of(r * t, t)
            kx = _alibi_cols(row + rs, slope, lane, ck_ref[...], 1).astype(BF16)
            for c in range(2):
                kaug_ref[c, pl.ds(rs, t), 0:HEAD_DIM] = k_ref[0, pl.ds(rs, t), c * HEAD_DIM:(c + 1) * HEAD_DIM]
                kaug_ref[c, pl.ds(rs, t), HEAD_DIM:2 * HEAD_DIM] = kx
            return carry

        lax.fori_loop(0, nk, body, 0)
        ri = lax.broadcasted_iota(jnp.int32, (t, t), 0)
        ci = lax.broadcasted_iota(jnp.int32, (t, t), 1)
        bias_ref[0] = jnp.zeros((t, t), F32)
        bias_ref[1] = jnp.abs(ri - ci).astype(F32) * (-LOG2E * slope)

    qx = _alibi_cols(row + qi * t, slope, lane, cq_ref[...], -1)
    for c in range(2):
        qc = q_ref[0, :, c * HEAD_DIM:(c + 1) * HEAD_DIM]
        for side, ext in ((0, qx), (1, -qx), (2, jnp.zeros_like(qx))):
            qaug_ref[c, side, :, 0:HEAD_DIM] = qc
            qaug_ref[c, side, :, HEAD_DIM:2 * HEAD_DIM] = ext.astype(BF16)

    width = npv * t
    ngroups = nk // npv
    outs = []
    for c in range(2):
        def scores(g, m):
            for i in range(nsub):
                kb = g * nsub + i
                side = jnp.where(kb < qi, 0, jnp.where(kb > qi, 1, 2))
                ks = pl.multiple_of(kb * t, t)
                s = _dot_nt(qaug_ref[c, side], kaug_ref[c, pl.ds(ks, t), :])
                s = s + bias_ref[(kb == qi).astype(jnp.int32)]
                s_ref[kb] = s
                for u in range(nslab):
                    m = jnp.maximum(m, s[:, u * LANE:(u + 1) * LANE])
            return m

        m = lax.fori_loop(0, nk // nsub, scores, jnp.full((t, LANE), NEG_BIG, F32))
        mb = jnp.broadcast_to(jnp.max(m, axis=-1, keepdims=True), (t, LANE))
        if ngroups > 1:
            acc_ref[...] = jnp.zeros_like(acc_ref)

        def pv(g, lrun):
            ps = []
            for i in range(npv):
                for u in range(nslab):
                    p = jnp.exp2(s_ref[g * npv + i, :, u * LANE:(u + 1) * LANE] - mb)
                    lrun = lrun + p
                    ps.append(p.astype(BF16))
            vs = v_ref[0, pl.ds(pl.multiple_of(g * width, width), width), :]
            pv_part = jnp.dot(jnp.concatenate(ps, axis=1), vs, preferred_element_type=F32)
            acc_ref[...] = pv_part if ngroups == 1 else acc_ref[...] + pv_part
            return lrun

        trips = ngroups + jnp.minimum(qi, 0)
        lrun = lax.fori_loop(0, trips, pv, jnp.zeros((t, LANE), F32))
        outs.append(acc_ref[...] / jnp.sum(lrun, axis=-1, keepdims=True))

    lam = (jnp.exp(jnp.sum(lam_ref[0:1, :] * lam_ref[1:2, :], axis=-1, keepdims=True))
           - jnp.exp(jnp.sum(lam_ref[2:3, :] * lam_ref[3:4, :], axis=-1, keepdims=True)) + lam_init)
    o = outs[0] - lam * outs[1]
    o_ref[0] = (_rms(o) * og_ref[...] * (1.0 - lam_init)).astype(o_ref.dtype)


def diff_attention(proj, lam_params, og, *, lam_init, t):
    b, s, _ = proj.shape
    assert s % t == 0 and s <= (1 << (2 * POS_LO_BITS + 1)) and t % LANE == 0
    nk = s // t
    nsub = max(n for n in (16, 8, 4, 3, 2, 1) if nk % n == 0)
    npv = max(n for n in (16, 8, 4, 3, 2, 1) if nk % n == 0)
    vw = 2 * HEAD_DIM
    slopes = jnp.asarray(2.0 ** (-8.0 * np.arange(1, N_HEADS + 1) / N_HEADS), F32)
    pieces = np.zeros((1, LANE), np.float32)
    pieces[0, :6] = LOG2E_PIECES * 2
    cq = jnp.asarray(pieces)
    ck = jnp.asarray(-np.roll(pieces, 6, axis=1))
    const = lambda shape: pl.BlockSpec(shape, lambda bi, h, qi, sl: (0, 0))
    grid_spec = pltpu.PrefetchScalarGridSpec(
        num_scalar_prefetch=1,
        grid=(b, N_HEADS, nk),
        in_specs=[
            pl.BlockSpec((1, t, vw), lambda bi, h, qi, sl: (bi, qi, h)),
            pl.BlockSpec((1, s, vw), lambda bi, h, qi, sl: (bi, 0, N_HEADS + h)),
            pl.BlockSpec((1, s, vw), lambda bi, h, qi, sl: (bi, 0, 2 * N_HEADS + h)),
            const((1, LANE)), const((1, LANE)), const((4, HEAD_DIM)), const((1, vw)),
        ],
        out_specs=pl.BlockSpec((1, t, vw), lambda bi, h, qi, sl: (bi, qi, h)),
        scratch_shapes=[pltpu.VMEM((2, s, vw), BF16),
                        pltpu.VMEM((2, 3, t, vw), BF16),
                        pltpu.VMEM((nk, t, t), F32),
                        pltpu.VMEM((2, t, t), F32),
                        pltpu.VMEM((t, vw), F32)],
    )
    return pl.pallas_call(
        functools.partial(_diff_attn_kernel, lam_init=lam_init, nsub=nsub, npv=npv),
        grid_spec=grid_spec,
        out_shape=jax.ShapeDtypeStruct((b, s, N_HEADS * vw), BF16),
        compiler_params=_cparams(("arbitrary", "arbitrary", "arbitrary")),
        name="diff_attention",
    )(slopes, proj, proj, proj, cq, ck, lam_params, og)


def _row(v):
    return v.astype(F32).reshape(1, -1)


def _lane_bcast(v):
    return jnp.broadcast_to(v.astype(F32)[:, None], (v.shape[0], LANE))


def kernel(x, mem, norm_mix_g, norm_xattn_g, norm_mem_g, norm_ffn_g, w_in_ab, ret_decay_fwd, ret_decay_bwd, ret_out_g, na_q_g, na_k_g, na_rpb, w_out_ab, w_in_c, diff_q_g, diff_k_g, lambda_q1, lambda_k1, lambda_q2, lambda_k2, diff_out_g, w_out_c, w_xq, w_xkv, w_xo, xq_g, xk_g, w_ffn_in, w_ffn_out):
    b, s, d = x.shape
    n_mem = mem.shape[1]
    depth = norm_mix_g.shape[0]
    t = b * s
    scale = HEAD_DIM ** -0.5
    width = N_HEADS * HEAD_DIM
    xw = N_X_HEADS * HEAD_DIM
    tm, tn, tn_in = ROW_TILE, COL_TILE, MIXER_COL_TILE

    xf = x.reshape(t, d)
    memf = mem.reshape(b * n_mem, d)
    ones_cols = lambda n: jnp.ones((1, n), F32)
    w_ffn_out_bf16 = w_ffn_out.astype(BF16)

    for i in range(depth):
        j = i // 2
        if i % 2 == 0:
            gcol = jnp.concatenate([
                ones_cols(4 * width),
                jnp.tile(_row(na_q_g[j]) * scale, (1, N_HEADS)),
                jnp.tile(_row(na_k_g[j]), (1, N_HEADS)),
                ones_cols(width)], axis=1)
            proj = norm_matmul(xf, _row(norm_mix_g[i]), w_in_ab, j, gcol,
                               4 * width // tn_in, 6 * width // tn_in, tm=tm, tn=tn_in)
            proj = proj.reshape(b, s, 7 * width)
            lgf = _lane_bcast(jax.nn.log_sigmoid(ret_decay_fwd[j].astype(F32)))
            lgb = _lane_bcast(jax.nn.log_sigmoid(ret_decay_bwd[j].astype(F32)))
            n_chunks = s // RET_CHUNK
            sb = ret_states(proj, lgb, chunk=RET_CHUNK, per_step=math.gcd(n_chunks, RET_STATE_CHUNKS_PER_STEP))
            ret = ret_out(proj, sb, lgf, lgb, _row(ret_out_g[j]), chunk=RET_CHUNK)
            na = neighborhood_attention(proj, na_rpb[j])
            mixed = [ret.reshape(t, width), na.reshape(t, width)]
            xf = matmul_res(mixed, w_out_ab, j, xf, tm=tm, tn=tn_in, weight_outer=True)
        else:
            cw = 2 * width
            gcol = jnp.concatenate([
                jnp.tile(_row(diff_q_g[j]) * (scale * LOG2E), (1, 2 * N_HEADS)),
                jnp.tile(_row(diff_k_g[j]), (1, 2 * N_HEADS)),
                ones_cols(cw)], axis=1)
            proj = norm_matmul(xf, _row(norm_mix_g[i]), w_in_c, j, gcol, 0, 2 * cw // tn_in, tm=tm, tn=tn_in)
            proj = proj.reshape(b, s, 3 * cw)
            lam_params = jnp.stack([lambda_q1[j], lambda_k1[j], lambda_q2[j], lambda_k2[j]]).astype(F32)
            lam_init = 0.8 - 0.6 * math.exp(-0.3 * i)
            att = diff_attention(proj, lam_params, _row(diff_out_g[j]), lam_init=lam_init, t=ATTN_TILE)
            xf = matmul_res([att.reshape(t, cw)], w_out_c, j, xf, tm=tm, tn=tn_in, weight_outer=True)

        kv_gcol = jnp.concatenate([jnp.tile(_row(xk_g[i]), (1, N_X_HEADS)), ones_cols(xw)], axis=1)
        kv = norm_matmul(memf, _row(norm_mem_g[i]), w_xkv, i, kv_gcol, 0, 1, tm=b * n_mem, tn=xw)
        gq = jnp.tile(_row(xq_g[i]) * scale, (1, N_X_HEADS))
        xf, xn_ffn = xattn(xf, _row(norm_xattn_g[i]), w_xq, gq, kv, w_xo, i, _row(norm_ffn_g[i]),
                           tm=XATTN_ROW_TILE, seq=s)

        hmid = swiglu_in(xn_ffn, w_ffn_in, i, tm=tm, tn=tn)
        xf = matmul_res([hmid], w_ffn_out_bf16, i, xf, tm=tm, tn=tn)

    return xf.reshape(b, s, d)
```

```python
import functools
import math

import jax
import jax.numpy as jnp
import numpy as np
from jax import lax
from jax.experimental import pallas as pl
from jax.experimental.pallas import tpu as pltpu

F32 = jnp.float32
BF16 = jnp.bfloat16

HEAD_DIM = 128
N_HEADS = 8
N_X_HEADS = 4
GRID_W = 64
NA_WIN_ROWS = 8
NA_WIN_COLS = 16
NA_QROWS = 4
RET_CHUNK = 256
RET_STATE_CHUNKS_PER_STEP = 4

ROW_TILE = 1024
COL_TILE = 512
MIXER_COL_TILE = 1024
XATTN_ROW_TILE = 512
ATTN_TILE = 512
NORM_ROW_CHUNK = 128
X_TILE_SLABS = 4
RMS_EPS = 1e-6
NEG_BIG = -1e30

LANE = 128
VMEM_LIMIT = 56 * 1024 * 1024


def _cparams(sem):
    return pltpu.CompilerParams(dimension_semantics=sem, vmem_limit_bytes=VMEM_LIMIT)


def _dot_nt(a, b):
    return lax.dot_general(a, b, (((1,), (1,)), ((), ())), preferred_element_type=F32)


def _dot_nn(a, b):
    return lax.dot_general(a, b, (((1,), (0,)), ((), ())), preferred_element_type=F32)


def _dot_tn(a, b):
    return lax.dot_general(a, b, (((0,), (0,)), ((), ())), preferred_element_type=F32)


def _layer_spec(w, layer, block, index_map):
    if w.ndim == 2:
        return pl.BlockSpec(block, index_map)
    return pl.BlockSpec((None,) + tuple(block), lambda *idx: (layer,) + tuple(index_map(*idx)))


def _rms(x):
    return x * lax.rsqrt(jnp.mean(x * x, axis=-1, keepdims=True) + RMS_EPS)


def _norm_rows_to_scratch(x_refs, g_ref, xn_ref, rows):
    slab = x_refs[0].shape[0]
    for q, x_ref in enumerate(x_refs):
        def body(r, c, x_ref=x_ref, q=q):
            src = pl.ds(pl.multiple_of(r * rows, rows), rows)
            dst = pl.ds(pl.multiple_of(q * slab + r * rows, rows), rows)
            xn_ref[dst, :] = (_rms(x_ref[src, :]) * g_ref[...]).astype(BF16)
            return c

        lax.fori_loop(0, slab // rows, body, 0)


def _norm_matmul_kernel(*refs, lo, hi):
    *x_refs, g_ref, w_ref, gc_ref, o_ref, xn_ref = refs
    j = pl.program_id(1)

    @pl.when(j == 0)
    def _():
        _norm_rows_to_scratch(x_refs, g_ref, xn_ref, NORM_ROW_CHUNK)

    grouped = jnp.logical_and(j >= lo, j < hi)

    @pl.when(grouped)
    def _():
        acc = _dot_nn(xn_ref[...], w_ref[...])
        for c in range(acc.shape[1] // HEAD_DIM):
            sl = slice(c * HEAD_DIM, (c + 1) * HEAD_DIM)
            o_ref[:, sl] = (_rms(acc[:, sl]) * gc_ref[:, sl]).astype(o_ref.dtype)

    @pl.when(jnp.logical_not(grouped))
    def _():
        o_ref[...] = _dot_nn(xn_ref[...], w_ref[...]).astype(o_ref.dtype)


def norm_matmul(x, g, w, layer, gcol, lo, hi, *, tm, tn, out_dtype=BF16):
    t, d = x.shape
    n = w.shape[-1]
    ni, nj = t // tm, n // tn
    nslabs = X_TILE_SLABS if nj > X_TILE_SLABS else 1
    slab = tm // nslabs

    def x_spec(q):
        first_next = nj - nslabs + q

        def index(i, j):
            tile = i if nslabs == 1 else jnp.minimum(i + (j >= first_next).astype(jnp.int32), ni - 1)
            return tile * nslabs + q, 0
        return pl.BlockSpec((slab, d), index)

    return pl.pallas_call(
        functools.partial(_norm_matmul_kernel, lo=lo, hi=hi),
        grid=(ni, nj),
        in_specs=[x_spec(q) for q in range(nslabs)] + [
            pl.BlockSpec((1, d), lambda i, j: (0, 0)),
            _layer_spec(w, layer, (d, tn), lambda i, j: (0, j)),
            pl.BlockSpec((1, tn), lambda i, j: (0, j)),
        ],
        out_specs=pl.BlockSpec((tm, tn), lambda i, j: (i, j)),
        out_shape=jax.ShapeDtypeStruct((t, n), out_dtype),
        scratch_shapes=[pltpu.VMEM((tm, d), BF16)],
        compiler_params=_cparams(("arbitrary", "arbitrary")),
        name="norm_matmul",
    )(*([x] * nslabs), g, w, gcol)


def _swiglu_in_kernel(xn_ref, wg_ref, wu_ref, o_ref):
    xn = xn_ref[...]
    gate = _dot_nn(xn, wg_ref[...])
    up = _dot_nn(xn, wu_ref[...])
    o_ref[...] = (gate * jax.nn.sigmoid(gate) * up).astype(o_ref.dtype)


def swiglu_in(xn, w_in, layer, *, tm, tn):
    t, d = xn.shape
    hidden = w_in.shape[-1] // 2
    nh = hidden // tn
    return pl.pallas_call(
        _swiglu_in_kernel,
        grid=(t // tm, nh),
        in_specs=[
            pl.BlockSpec((tm, d), lambda i, j: (i, 0)),
            _layer_spec(w_in, layer, (d, tn), lambda i, j: (0, j)),
            _layer_spec(w_in, layer, (d, tn), lambda i, j: (0, j + nh)),
        ],
        out_specs=pl.BlockSpec((tm, tn), lambda i, j: (i, j)),
        out_shape=jax.ShapeDtypeStruct((t, hidden), BF16),
        compiler_params=_cparams(("parallel", "arbitrary")),
        name="swiglu_in",
    )(xn, w_in, w_in)


def _matmul_res_kernel(*refs):
    *a_refs, w_ref, r_ref, o_ref = refs
    acc = r_ref[...]
    k0 = 0
    for a_ref in a_refs:
        k1 = k0 + a_ref.shape[1]
        acc += _dot_nn(a_ref[...], w_ref[k0:k1, :])
        k0 = k1
    o_ref[...] = acc


def matmul_res(a_parts, w, layer, res, *, tm, tn, weight_outer=False):
    t = res.shape[0]
    k, n = w.shape[-2:]
    assert sum(a.shape[1] for a in a_parts) == k
    if weight_outer:
        grid, ij = (n // tn, t // tm), (lambda j, i: (i, j))
    else:
        grid, ij = (t // tm, n // tn), (lambda i, j: (i, j))
    row = lambda *g: (ij(*g)[0], 0)
    col = lambda *g: (0, ij(*g)[1])
    return pl.pallas_call(
        _matmul_res_kernel,
        grid=grid,
        in_specs=[pl.BlockSpec((tm, a.shape[1]), row) for a in a_parts] + [
            _layer_spec(w, layer, (k, tn), col),
            pl.BlockSpec((tm, tn), ij),
        ],
        out_specs=pl.BlockSpec((tm, tn), ij),
        out_shape=jax.ShapeDtypeStruct((t, n), F32),
        compiler_params=_cparams(("arbitrary", "arbitrary")),
        name="matmul_res",
    )(*a_parts, w, res)


def _xattn_kernel(x_ref, g_ref, wq_ref, gq_ref, kv_ref, wo_ref, gn_ref, o_ref, on_ref):
    x = x_ref[...]
    xn = (_rms(x) * g_ref[...]).astype(BF16)
    q = _dot_nn(xn, wq_ref[...])
    width = N_X_HEADS * HEAD_DIM
    outs = []
    for h in range(N_X_HEADS):
        sl = slice(h * HEAD_DIM, (h + 1) * HEAD_DIM)
        qh = (_rms(q[:, sl]) * gq_ref[:, sl]).astype(BF16)
        s = _dot_nt(qh, kv_ref[:, sl])
        p = jnp.exp(s - jnp.max(s, axis=-1, keepdims=True))
        l = jnp.sum(p, axis=-1, keepdims=True)
        vh = kv_ref[:, width + h * HEAD_DIM: width + (h + 1) * HEAD_DIM]
        oh = jnp.dot(p.astype(BF16), vh, preferred_element_type=F32) / l
        outs.append(oh.astype(BF16))
    o = jnp.concatenate(outs, axis=1)
    y = x + _dot_nn(o, wo_ref[...])
    o_ref[...] = y
    on_ref[...] = (_rms(y) * gn_ref[...]).astype(on_ref.dtype)


def xattn(x, g, wq, gq, kv, wo, layer, g_next, *, tm, seq):
    t, d = x.shape
    n_mem = kv.shape[0] // (t // seq)
    width = wq.shape[-1]
    per_b = seq // tm
    return pl.pallas_call(
        _xattn_kernel,
        grid=(t // tm,),
        in_specs=[
            pl.BlockSpec((tm, d), lambda i: (i, 0)),
            pl.BlockSpec((1, d), lambda i: (0, 0)),
            _layer_spec(wq, layer, (d, width), lambda i: (0, 0)),
            pl.BlockSpec((1, width), lambda i: (0, 0)),
            pl.BlockSpec((n_mem, 2 * width), lambda i: (i // per_b, 0)),
            _layer_spec(wo, layer, (width, d), lambda i: (0, 0)),
            pl.BlockSpec((1, d), lambda i: (0, 0)),
        ],
        out_specs=[pl.BlockSpec((tm, d), lambda i: (i, 0)), pl.BlockSpec((tm, d), lambda i: (i, 0))],
        out_shape=[jax.ShapeDtypeStruct((t, d), F32), jax.ShapeDtypeStruct((t, d), BF16)],
        compiler_params=_cparams(("parallel",)),
        name="xattn",
    )(x, g, wq, gq, kv, wo, g_next)


def _ret_state_kernel(k_ref, v_ref, lgb_ref, sb_ref, state_ref, kd_ref, *, chunk):
    c = pl.program_id(1)
    per_step = k_ref.shape[1] // chunk
    scale = HEAD_DIM ** -0.5

    @pl.when(c == 0)
    def _():
        state_ref[...] = jnp.zeros_like(state_ref)
        pos = lax.broadcasted_iota(jnp.int32, (chunk, HEAD_DIM), 0).astype(F32)
        for h in range(N_HEADS):
            kd_ref[h] = jnp.exp(lgb_ref[h:h + 1, :] * pos) * scale

    for u in reversed(range(per_step)):
        rows = slice(u * chunk, (u + 1) * chunk)
        for h in range(N_HEADS):
            sl = slice(h * HEAD_DIM, (h + 1) * HEAD_DIM)
            st = state_ref[h]
            sb_ref[0, u, h] = st.astype(BF16)
            kd = (k_ref[0, rows, sl].astype(F32) * kd_ref[h]).astype(BF16)
            cdec = jnp.exp(lgb_ref[h:h + 1, :] * float(chunk))
            state_ref[h] = st * cdec + _dot_tn(kd, v_ref[0, rows, sl])


def ret_states(proj, lgb, *, chunk, per_step):
    b, s, _ = proj.shape
    nc = s // chunk
    assert nc % per_step == 0
    ns = nc // per_step
    width = N_HEADS * HEAD_DIM
    return pl.pallas_call(
        functools.partial(_ret_state_kernel, chunk=chunk),
        grid=(b, ns),
        in_specs=[
            pl.BlockSpec((1, per_step * chunk, width), lambda bi, c: (bi, ns - 1 - c, 1)),
            pl.BlockSpec((1, per_step * chunk, width), lambda bi, c: (bi, ns - 1 - c, 2)),
            pl.BlockSpec((N_HEADS, LANE), lambda bi, c: (0, 0)),
        ],
        out_specs=pl.BlockSpec((1, per_step, N_HEADS, HEAD_DIM, HEAD_DIM),
                               lambda bi, c: (bi, ns - 1 - c, 0, 0, 0)),
        out_shape=jax.ShapeDtypeStruct((b, nc, N_HEADS, HEAD_DIM, HEAD_DIM), BF16),
        scratch_shapes=[pltpu.VMEM((N_HEADS, HEAD_DIM, HEAD_DIM), F32),
                        pltpu.VMEM((N_HEADS, chunk, HEAD_DIM), F32)],
        compiler_params=_cparams(("parallel", "arbitrary")),
        name="ret_states",
    )(proj, proj, lgb)


def _ret_out_kernel(q_ref, k_ref, v_ref, rg_ref, sb_ref, lgf_ref, lgb_ref, og_ref, o_ref,
                    state_ref, dmat_ref, qdf_ref, qdb_ref, kdf_ref):
    c = pl.program_id(1)
    chunk = q_ref.shape[1]
    scale = HEAD_DIM ** -0.5

    @pl.when(c == 0)
    def _():
        state_ref[...] = jnp.zeros_like(state_ref)
        pos = lax.broadcasted_iota(jnp.int32, (chunk, HEAD_DIM), 0).astype(F32)
        ri = lax.broadcasted_iota(jnp.int32, (chunk, chunk), 0)
        ci = lax.broadcasted_iota(jnp.int32, (chunk, chunk), 1)
        diff = (ri - ci).astype(F32)
        for h in range(N_HEADS):
            lf = lgf_ref[h:h + 1, :]
            lb = lgb_ref[h:h + 1, :]
            dfwd = jnp.exp(lf[:, :1] * jnp.maximum(diff, 0.0))
            dbwd = jnp.exp(lb[:, :1] * jnp.maximum(-diff, 0.0))
            dmat_ref[h] = jnp.where(diff >= 0.0, dfwd, dbwd) * scale
            qdf_ref[h] = jnp.exp(lf * (pos + 1.0))
            qdb_ref[h] = jnp.exp(lb * (float(chunk) - pos))
            kdf_ref[h] = jnp.exp(lf * (float(chunk) - 1.0 - pos)) * scale

    for h in range(N_HEADS):
        sl = slice(h * HEAD_DIM, (h + 1) * HEAD_DIM)
        qh = q_ref[0, :, sl]
        kh = k_ref[0, :, sl]
        vh = v_ref[0, :, sl]
        qf32 = qh.astype(F32)
        sd = (_dot_nt(qh, kh) * dmat_ref[h]).astype(BF16)
        st = state_ref[h]
        out = jnp.dot(sd, vh, preferred_element_type=F32)
        out += jnp.dot((qf32 * qdf_ref[h]).astype(BF16), st.astype(BF16), preferred_element_type=F32)
        out += jnp.dot((qf32 * qdb_ref[h]).astype(BF16), sb_ref[0, 0, h], preferred_element_type=F32)
        gate = rg_ref[0, :, sl].astype(F32)
        y = _rms(out) * og_ref[...]
        o_ref[0, :, sl] = (y * (gate * jax.nn.sigmoid(gate))).astype(o_ref.dtype)
        kd = (kh.astype(F32) * kdf_ref[h]).astype(BF16)
        cdec = jnp.exp(lgf_ref[h:h + 1, :] * float(chunk))
        state_ref[h] = st * cdec + _dot_tn(kd, vh)


def ret_out(proj, sb, lgf, lgb, og, *, chunk):
    b, s, _ = proj.shape
    nc = s // chunk
    width = N_HEADS * HEAD_DIM
    col = lambda k: pl.BlockSpec((1, chunk, width), lambda bi, c, k=k: (bi, c, k))
    return pl.pallas_call(
        _ret_out_kernel,
        grid=(b, nc),
        in_specs=[
            col(0), col(1), col(2), col(3),
            pl.BlockSpec((1, 1, N_HEADS, HEAD_DIM, HEAD_DIM), lambda bi, c: (bi, c, 0, 0, 0)),
            pl.BlockSpec((N_HEADS, LANE), lambda bi, c: (0, 0)),
            pl.BlockSpec((N_HEADS, LANE), lambda bi, c: (0, 0)),
            pl.BlockSpec((1, HEAD_DIM), lambda bi, c: (0, 0)),
        ],
        out_specs=pl.BlockSpec((1, chunk, width), lambda bi, c: (bi, c, 0)),
        out_shape=jax.ShapeDtypeStruct((b, s, width), BF16),
        scratch_shapes=[pltpu.VMEM((N_HEADS, HEAD_DIM, HEAD_DIM), F32),
                        pltpu.VMEM((N_HEADS, chunk, chunk), F32),
                        pltpu.VMEM((N_HEADS, chunk, HEAD_DIM), F32),
                        pltpu.VMEM((N_HEADS, chunk, HEAD_DIM), F32),
                        pltpu.VMEM((N_HEADS, chunk, HEAD_DIM), F32)],
        compiler_params=_cparams(("parallel", "arbitrary")),
        name="ret_out",
    )(proj, proj, proj, proj, sb, lgf, lgb, og)


N_DR = 2 * NA_WIN_ROWS - 1
N_DC = 2 * NA_WIN_COLS - 1


def _na_kernel(rpb_ref, q_ref, kp_ref, kc_ref, kn_ref, vp_ref, vc_ref, vn_ref, o_ref,
               bcol_ref, slab_ref, *, rows):
    qb = pl.program_id(1)
    nb = rows // NA_QROWS
    neg_tile = jnp.full((GRID_W, GRID_W), NEG_BIG, F32)

    @pl.when(qb == 0)
    def _():
        qc = lax.broadcasted_iota(jnp.int32, (GRID_W, GRID_W), 0)
        kc = lax.broadcasted_iota(jnp.int32, (GRID_W, GRID_W), 1)
        cs = jnp.clip(qc - NA_WIN_COLS // 2, 0, GRID_W - NA_WIN_COLS)
        col_ok = jnp.logical_and(kc >= cs, kc < cs + NA_WIN_COLS)
        dc = kc - qc + (NA_WIN_COLS - 1)
        for h in range(N_HEADS):
            for dr in range(N_DR):
                tile = neg_tile
                for d in range(N_DC):
                    tile = jnp.where(dc == d, rpb_ref[(h * N_DR + dr) * N_DC + d], tile)
                bcol_ref[h, dr] = jnp.where(col_ok, tile, NEG_BIG)

    def assemble(qb_static):
        for h in range(N_HEADS):
            for j in range(NA_QROWS):
                r = NA_QROWS * qb_static + j
                rs = min(max(r - NA_WIN_ROWS // 2, 0), rows - NA_WIN_ROWS)
                for i2 in range(0, 3 * NA_QROWS, 2):
                    halves = []
                    for i in (i2, i2 + 1):
                        kr = NA_QROWS * (qb_static - 1) + i
                        ok = rs <= kr < rs + NA_WIN_ROWS
                        halves.append(bcol_ref[h, kr - r + NA_WIN_ROWS - 1] if ok else neg_tile)
                    slab_ref[h, j * GRID_W:(j + 1) * GRID_W, i2 * GRID_W:(i2 + 2) * GRID_W] = (
                        jnp.concatenate(halves, axis=1))

    for qb_static in (0, 1, nb - 1):
        pl.when(qb == qb_static)(functools.partial(assemble, qb_static))

    for h in range(N_HEADS):
        sl = slice(h * HEAD_DIM, (h + 1) * HEAD_DIM)
        k = jnp.concatenate([kp_ref[0, :, sl], kc_ref[0, :, sl], kn_ref[0, :, sl]], axis=0)
        v = jnp.concatenate([vp_ref[0, :, sl], vc_ref[0, :, sl], vn_ref[0, :, sl]], axis=0)
        s = _dot_nt(q_ref[0, :, sl], k) + slab_ref[h]
        p = jnp.exp(s - jnp.max(s, axis=-1, keepdims=True))
        l = jnp.sum(p, axis=-1, keepdims=True)
        o = jnp.dot(p.astype(BF16), v, preferred_element_type=F32) / l
        o_ref[0, :, sl] = o.astype(o_ref.dtype)


def neighborhood_attention(proj, rpb):
    b, s, _ = proj.shape
    rows = s // GRID_W
    assert NA_QROWS >= NA_WIN_ROWS // 2 and NA_QROWS % 2 == 0 and rows % NA_QROWS == 0
    assert rows >= 3 * NA_QROWS and rows >= NA_WIN_ROWS
    nq = NA_QROWS * GRID_W
    nb = rows // NA_QROWS
    width = N_HEADS * HEAD_DIM

    def blk(group, shift):
        return pl.BlockSpec((1, nq, width),
                            lambda bi, qb, rp: (bi, jnp.clip(qb + shift, 0, nb - 1), group))

    grid_spec = pltpu.PrefetchScalarGridSpec(
        num_scalar_prefetch=1,
        grid=(b, nb),
        in_specs=[blk(4, 0), blk(5, -1), blk(5, 0), blk(5, 1), blk(6, -1), blk(6, 0), blk(6, 1)],
        out_specs=pl.BlockSpec((1, nq, width), lambda bi, qb, rp: (bi, qb, 0)),
        scratch_shapes=[pltpu.VMEM((N_HEADS, N_DR, GRID_W, GRID_W), F32),
                        pltpu.VMEM((N_HEADS, nq, 3 * nq), F32)],
    )
    return pl.pallas_call(
        functools.partial(_na_kernel, rows=rows),
        grid_spec=grid_spec,
        out_shape=jax.ShapeDtypeStruct((b, s, width), BF16),
        compiler_params=_cparams(("arbitrary", "arbitrary")),
        name="neighborhood_attention",
    )(rpb.astype(F32).reshape(-1), proj, proj, proj, proj, proj, proj, proj)


def _bf16_pieces(x, n):
    out, rest = [], float(x)
    for _ in range(n):
        piece = float(np.float32(rest).astype(BF16).astype(np.float32))
        out.append(piece)
        rest -= piece
    return out


LOG2E = math.log2(math.e)
LOG2E_PIECES = _bf16_pieces(LOG2E, 3)
N_ALIBI_COLS = 12
POS_LO_BITS = 7


def _alibi_cols(pos, slope, lane, unit_cols, sign):
    hi = (pos >> POS_LO_BITS).astype(F32) * (slope * float(1 << POS_LO_BITS))
    lo = (pos & ((1 << POS_LO_BITS) - 1)).astype(F32) * slope
    if sign > 0:
        return jnp.where(lane < 3, hi, jnp.where(lane < 6, lo, unit_cols))
    return jnp.where(lane < 6, unit_cols, jnp.where(lane < 9, hi, jnp.where(lane < N_ALIBI_COLS, lo, 0.0)))


def _diff_attn_kernel(slopes_ref, q_ref, k_ref, v_ref, cq_ref, ck_ref, lam_ref, og_ref, o_ref,
                      kaug_ref, qaug_ref, s_ref, bias_ref, acc_ref, *, lam_init, nsub, npv):
    h = pl.program_id(1)
    qi = pl.program_id(2)
    t = q_ref.shape[1]
    nk = k_ref.shape[1] // t
    nslab = t // LANE
    slope = slopes_ref[h]
    lane = lax.broadcasted_iota(jnp.int32, (t, LANE), 1)
    row = lax.broadcasted_iota(jnp.int32, (t, LANE), 0)

    @pl.when(qi == 0)
    def _():
        def body(r, carry):
            rs = pl.multiple_of(r * t, t)
            kx = _alibi_cols(row + rs, slope, lane, ck_ref[...], 1).astype(BF16)
            for c in range(2):
                kaug_ref[c, pl.ds(rs, t), 0:HEAD_DIM] = k_ref[0, pl.ds(rs, t), c * HEAD_DIM:(c + 1) * HEAD_DIM]
                kaug_ref[c, pl.ds(rs, t), HEAD_DIM:2 * HEAD_DIM] = kx
            return carry

        lax.fori_loop(0, nk, body, 0)
        ri = lax.broadcasted_iota(jnp.int32, (t, t), 0)
        ci = lax.broadcasted_iota(jnp.int32, (t, t), 1)
        bias_ref[0] = jnp.zeros((t, t), F32)
        bias_ref[1] = jnp.abs(ri - ci).astype(F32) * (-LOG2E * slope)

    qx = _alibi_cols(row + qi * t, slope, lane, cq_ref[...], -1)
    for c in range(2):
        qc = q_ref[0, :, c * HEAD_DIM:(c + 1) * HEAD_DIM]
        for side, ext in ((0, qx), (1, -qx), (2, jnp.zeros_like(qx))):
            qaug_ref[c, side, :, 0:HEAD_DIM] = qc
            qaug_ref[c, side, :, HEAD_DIM:2 * HEAD_DIM] = ext.astype(BF16)

    width = npv * t
    ngroups = nk // npv
    outs = []
    for c in range(2):
        def scores(g, m):
            for i in range(nsub):
                kb = g * nsub + i
                side = jnp.where(kb < qi, 0, jnp.where(kb > qi, 1, 2))
                ks = pl.multiple_of(kb * t, t)
                s = _dot_nt(qaug_ref[c, side], kaug_ref[c, pl.ds(ks, t), :])
                s = s + bias_ref[(kb == qi).astype(jnp.int32)]
                s_ref[kb] = s
                for u in range(nslab):
                    m = jnp.maximum(m, s[:, u * LANE:(u + 1) * LANE])
            return m

        m = lax.fori_loop(0, nk // nsub, scores, jnp.full((t, LANE), NEG_BIG, F32))
        mb = jnp.broadcast_to(jnp.max(m, axis=-1, keepdims=True), (t, LANE))
        if ngroups > 1:
            acc_ref[...] = jnp.zeros_like(acc_ref)

        def pv(g, lrun):
            ps = []
            for i in range(npv):
                for u in range(nslab):
                    p = jnp.exp2(s_ref[g * npv + i, :, u * LANE:(u + 1) * LANE] - mb)
                    lrun = lrun + p
                    ps.append(p.astype(BF16))
            vs = v_ref[0, pl.ds(pl.multiple_of(g * width, width), width), :]
            pv_part = jnp.dot(jnp.concatenate(ps, axis=1), vs, preferred_element_type=F32)
            acc_ref[...] = pv_part if ngroups == 1 else acc_ref[...] + pv_part
            return lrun

        trips = ngroups + jnp.minimum(qi, 0)
        lrun = lax.fori_loop(0, trips, pv, jnp.zeros((t, LANE), F32))
        outs.append(acc_ref[...] / jnp.sum(lrun, axis=-1, keepdims=True))

    lam = (jnp.exp(jnp.sum(lam_ref[0:1, :] * lam_ref[1:2, :], axis=-1, keepdims=True))
           - jnp.exp(jnp.sum(lam_ref[2:3, :] * lam_ref[3:4, :], axis=-1, keepdims=True)) + lam_init)
    o = outs[0] - lam * outs[1]
    o_ref[0] = (_rms(o) * og_ref[...] * (1.0 - lam_init)).astype(o_ref.dtype)


def diff_attention(proj, lam_params, og, *, lam_init, t):
    b, s, _ = proj.shape
    assert s % t == 0 and s <= (1 << (2 * POS_LO_BITS + 1)) and t % LANE == 0
    nk = s // t
    nsub = max(n for n in (16, 8, 4, 3, 2, 1) if nk % n == 0)
    npv = max(n for n in (16, 8, 4, 3, 2, 1) if nk % n == 0)
    vw = 2 * HEAD_DIM
    slopes = jnp.asarray(2.0 ** (-8.0 * np.arange(1, N_HEADS + 1) / N_HEADS), F32)
    pieces = np.zeros((1, LANE), np.float32)
    pieces[0, :6] = LOG2E_PIECES * 2
    cq = jnp.asarray(pieces)
    ck = jnp.asarray(-np.roll(pieces, 6, axis=1))
    const = lambda shape: pl.BlockSpec(shape, lambda bi, h, qi, sl: (0, 0))
    grid_spec = pltpu.PrefetchScalarGridSpec(
        num_scalar_prefetch=1,
        grid=(b, N_HEADS, nk),
        in_specs=[
            pl.BlockSpec((1, t, vw), lambda bi, h, qi, sl: (bi, qi, h)),
            pl.BlockSpec((1, s, vw), lambda bi, h, qi, sl: (bi, 0, N_HEADS + h)),
            pl.BlockSpec((1, s, vw), lambda bi, h, qi, sl: (bi, 0, 2 * N_HEADS + h)),
            const((1, LANE)), const((1, LANE)), const((4, HEAD_DIM)), const((1, vw)),
        ],
        out_specs=pl.BlockSpec((1, t, vw), lambda bi, h, qi, sl: (bi, qi, h)),
        scratch_shapes=[pltpu.VMEM((2, s, vw), BF16),
                        pltpu.VMEM((2, 3, t, vw), BF16),
                        pltpu.VMEM((nk, t, t), F32),
                        pltpu.VMEM((2, t, t), F32),
                        pltpu.VMEM((t, vw), F32)],
    )
    return pl.pallas_call(
        functools.partial(_diff_attn_kernel, lam_init=lam_init, nsub=nsub, npv=npv),
        grid_spec=grid_spec,
        out_shape=jax.ShapeDtypeStruct((b, s, N_HEADS * vw), BF16),
        compiler_params=_cparams(("arbitrary", "arbitrary", "arbitrary")),
        name="diff_attention",
    )(slopes, proj, proj, proj, cq, ck, lam_params, og)


def _row(v):
    return v.astype(F32).reshape(1, -1)


def _lane_bcast(v):
    return jnp.broadcast_to(v.astype(F32)[:, None], (v.shape[0], LANE))


def kernel(x, mem, norm_mix_g, norm_xattn_g, norm_mem_g, norm_ffn_g, w_in_ab, ret_decay_fwd, ret_decay_bwd, ret_out_g, na_q_g, na_k_g, na_rpb, w_out_ab, w_in_c, diff_q_g, diff_k_g, lambda_q1, lambda_k1, lambda_q2, lambda_k2, diff_out_g, w_out_c, w_xq, w_xkv, w_xo, xq_g, xk_g, w_ffn_in, w_ffn_out):
    b, s, d = x.shape
    n_mem = mem.shape[1]
    depth = norm_mix_g.shape[0]
    t = b * s
    scale = HEAD_DIM ** -0.5
    width = N_HEADS * HEAD_DIM
    xw = N_X_HEADS * HEAD_DIM
    tm, tn, tn_in = ROW_TILE, COL_TILE, MIXER_COL_TILE

    xf = x.reshape(t, d)
    memf = mem.reshape(b * n_mem, d)
    ones_cols = lambda n: jnp.ones((1, n), F32)
    w_ffn_out_bf16 = w_ffn_out.astype(BF16)

    for i in range(depth):
        j = i // 2
        if i % 2 == 0:
            gcol = jnp.concatenate([
                ones_cols(4 * width),
                jnp.tile(_row(na_q_g[j]) * scale, (1, N_HEADS)),
                jnp.tile(_row(na_k_g[j]), (1, N_HEADS)),
                ones_cols(width)], axis=1)
            proj = norm_matmul(xf, _row(norm_mix_g[i]), w_in_ab, j, gcol,
                               4 * width // tn_in, 6 * width // tn_in, tm=tm, tn=tn_in)
            proj = proj.reshape(b, s, 7 * width)
            lgf = _lane_bcast(jax.nn.log_sigmoid(ret_decay_fwd[j].astype(F32)))
            lgb = _lane_bcast(jax.nn.log_sigmoid(ret_decay_bwd[j].astype(F32)))
            n_chunks = s // RET_CHUNK
            sb = ret_states(proj, lgb, chunk=RET_CHUNK, per_step=math.gcd(n_chunks, RET_STATE_CHUNKS_PER_STEP))
            ret = ret_out(proj, sb, lgf, lgb, _row(ret_out_g[j]), chunk=RET_CHUNK)
            na = neighborhood_attention(proj, na_rpb[j])
            mixed = [ret.reshape(t, width), na.reshape(t, width)]
            xf = matmul_res(mixed, w_out_ab, j, xf, tm=tm, tn=tn_in, weight_outer=True)
        else:
            cw = 2 * width
            gcol = jnp.concatenate([
                jnp.tile(_row(diff_q_g[j]) * (scale * LOG2E), (1, 2 * N_HEADS)),
                jnp.tile(_row(diff_k_g[j]), (1, 2 * N_HEADS)),
                ones_cols(cw)], axis=1)
            proj = norm_matmul(xf, _row(norm_mix_g[i]), w_in_c, j, gcol, 0, 2 * cw // tn_in, tm=tm, tn=tn_in)
            proj = proj.reshape(b, s, 3 * cw)
            lam_params = jnp.stack([lambda_q1[j], lambda_k1[j], lambda_q2[j], lambda_k2[j]]).astype(F32)
            lam_init = 0.8 - 0.6 * math.exp(-0.3 * i)
            att = diff_attention(proj, lam_params, _row(diff_out_g[j]), lam_init=lam_init, t=ATTN_TILE)
            xf = matmul_res([att.reshape(t, cw)], w_out_c, j, xf, tm=tm, tn=tn_in, weight_outer=True)

        kv_gcol = jnp.concatenate([jnp.tile(_row(xk_g[i]), (1, N_X_HEADS)), ones_cols(xw)], axis=1)
        kv = norm_matmul(memf, _row(norm_mem_g[i]), w_xkv, i, kv_gcol, 0, 1, tm=b * n_mem, tn=xw)
        gq = jnp.tile(_row(xq_g[i]) * scale, (1, N_X_HEADS))
        xf, xn_ffn = xattn(xf, _row(norm_xattn_g[i]), w_xq, gq, kv, w_xo, i, _row(norm_ffn_g[i]),
                           tm=XATTN_ROW_TILE, seq=s)

        hmid = swiglu_in(xn_ffn, w_ffn_in, i, tm=tm, tn=tn)
        xf = matmul_res([hmid], w_ffn_out_bf16, i, xf, tm=tm, tn=tn, weight_outer=True)

    return xf.reshape(b, s, d)
```

```python
import functools
import math

import jax
import jax.numpy as jnp
import numpy as np
from jax import lax
from jax.experimental import pallas as pl
from jax.experimental.pallas import tpu as pltpu

F32 = jnp.float32
BF16 = jnp.bfloat16

HEAD_DIM = 128
N_HEADS = 8
N_X_HEADS = 4
GRID_W = 64
NA_WIN_ROWS = 8
NA_WIN_COLS = 16
NA_QROWS = 4
RET_CHUNK = 256
RET_STATE_CHUNKS_PER_STEP = 4

ROW_TILE = 1024
COL_TILE = 512
MIXER_COL_TILE = 1024
XATTN_ROW_TILE = 512
ATTN_TILE = 512
NORM_ROW_CHUNK = 128
X_TILE_SLABS = 4
RMS_EPS = 1e-6
NEG_BIG = -1e30

LANE = 128
VMEM_LIMIT = 56 * 1024 * 1024


def _cparams(sem):
    return pltpu.CompilerParams(dimension_semantics=sem, vmem_limit_bytes=VMEM_LIMIT)


def _dot_nt(a, b):
    return lax.dot_general(a, b, (((1,), (1,)), ((), ())), preferred_element_type=F32)


def _dot_nn(a, b):
    return lax.dot_general(a, b, (((1,), (0,)), ((), ())), preferred_element_type=F32)


def _dot_tn(a, b):
    return lax.dot_general(a, b, (((0,), (0,)), ((), ())), preferred_element_type=F32)


def _layer_spec(w, layer, block, index_map):
    if w.ndim == 2:
        return pl.BlockSpec(block, index_map)
    return pl.BlockSpec((None,) + tuple(block), lambda *idx: (layer,) + tuple(index_map(*idx)))


def _rms(x):
    return x * lax.rsqrt(jnp.mean(x * x, axis=-1, keepdims=True) + RMS_EPS)


def _norm_rows_to_scratch(x_refs, g_ref, xn_ref, rows):
    slab = x_refs[0].shape[0]
    for q, x_ref in enumerate(x_refs):
        def body(r, c, x_ref=x_ref, q=q):
            src = pl.ds(pl.multiple_of(r * rows, rows), rows)
            dst = pl.ds(pl.multiple_of(q * slab + r * rows, rows), rows)
            xn_ref[dst, :] = (_rms(x_ref[src, :]) * g_ref[...]).astype(BF16)
            return c

        lax.fori_loop(0, slab // rows, body, 0)


def _norm_matmul_kernel(*refs, lo, hi):
    *x_refs, g_ref, w_ref, gc_ref, o_ref, xn_ref = refs
    j = pl.program_id(1)

    @pl.when(j == 0)
    def _():
        _norm_rows_to_scratch(x_refs, g_ref, xn_ref, NORM_ROW_CHUNK)

    grouped = jnp.logical_and(j >= lo, j < hi)

    @pl.when(grouped)
    def _():
        acc = _dot_nn(xn_ref[...], w_ref[...])
        for c in range(acc.shape[1] // HEAD_DIM):
            sl = slice(c * HEAD_DIM, (c + 1) * HEAD_DIM)
            o_ref[:, sl] = (_rms(acc[:, sl]) * gc_ref[:, sl]).astype(o_ref.dtype)

    @pl.when(jnp.logical_not(grouped))
    def _():
        o_ref[...] = _dot_nn(xn_ref[...], w_ref[...]).astype(o_ref.dtype)


def norm_matmul(x, g, w, layer, gcol, lo, hi, *, tm, tn, out_dtype=BF16):
    t, d = x.shape
    n = w.shape[-1]
    ni, nj = t // tm, n // tn
    nslabs = X_TILE_SLABS if nj > X_TILE_SLABS else 1
    slab = tm // nslabs

    def x_spec(q):
        first_next = nj - nslabs + q

        def index(i, j):
            tile = i if nslabs == 1 else jnp.minimum(i + (j >= first_next).astype(jnp.int32), ni - 1)
            return tile * nslabs + q, 0
        return pl.BlockSpec((slab, d), index)

    return pl.pallas_call(
        functools.partial(_norm_matmul_kernel, lo=lo, hi=hi),
        grid=(ni, nj),
        in_specs=[x_spec(q) for q in range(nslabs)] + [
            pl.BlockSpec((1, d), lambda i, j: (0, 0)),
            _layer_spec(w, layer, (d, tn), lambda i, j: (0, j)),
            pl.BlockSpec((1, tn), lambda i, j: (0, j)),
        ],
        out_specs=pl.BlockSpec((tm, tn), lambda i, j: (i, j)),
        out_shape=jax.ShapeDtypeStruct((t, n), out_dtype),
        scratch_shapes=[pltpu.VMEM((tm, d), BF16)],
        compiler_params=_cparams(("arbitrary", "arbitrary")),
        name="norm_matmul",
    )(*([x] * nslabs), g, w, gcol)


def _swiglu_in_kernel(xn_ref, wg_ref, wu_ref, o_ref):
    xn = xn_ref[...]
    gate = _dot_nn(xn, wg_ref[...])
    up = _dot_nn(xn, wu_ref[...])
    o_ref[...] = (gate * jax.nn.sigmoid(gate) * up).astype(o_ref.dtype)


def swiglu_in(xn, w_in, layer, *, tm, tn):
    t, d = xn.shape
    hidden = w_in.shape[-1] // 2
    nh = hidden // tn
    return pl.pallas_call(
        _swiglu_in_kernel,
        grid=(t // tm, nh),
        in_specs=[
            pl.BlockSpec((tm, d), lambda i, j: (i, 0)),
            _layer_spec(w_in, layer, (d, tn), lambda i, j: (0, j)),
            _layer_spec(w_in, layer, (d, tn), lambda i, j: (0, j + nh)),
        ],
        out_specs=pl.BlockSpec((tm, tn), lambda i, j: (i, j)),
        out_shape=jax.ShapeDtypeStruct((t, hidden), BF16),
        compiler_params=_cparams(("parallel", "arbitrary")),
        name="swiglu_in",
    )(xn, w_in, w_in)


def _matmul_res_kernel(*refs):
    *a_refs, w_ref, r_ref, o_ref = refs
    acc = r_ref[...]
    k0 = 0
    for a_ref in a_refs:
        k1 = k0 + a_ref.shape[1]
        acc += _dot_nn(a_ref[...], w_ref[k0:k1, :])
        k0 = k1
    o_ref[...] = acc


def matmul_res(a_parts, w, layer, res, *, tm, tn, weight_outer=False):
    t = res.shape[0]
    k, n = w.shape[-2:]
    assert sum(a.shape[1] for a in a_parts) == k
    if weight_outer:
        grid, ij = (n // tn, t // tm), (lambda j, i: (i, j))
    else:
        grid, ij = (t // tm, n // tn), (lambda i, j: (i, j))
    row = lambda *g: (ij(*g)[0], 0)
    col = lambda *g: (0, ij(*g)[1])
    return pl.pallas_call(
        _matmul_res_kernel,
        grid=grid,
        in_specs=[pl.BlockSpec((tm, a.shape[1]), row) for a in a_parts] + [
            _layer_spec(w, layer, (k, tn), col),
            pl.BlockSpec((tm, tn), ij),
        ],
        out_specs=pl.BlockSpec((tm, tn), ij),
        out_shape=jax.ShapeDtypeStruct((t, n), F32),
        compiler_params=_cparams(("arbitrary", "arbitrary")),
        name="matmul_res",
    )(*a_parts, w, res)


def _xattn_kernel(x_ref, g_ref, wq_ref, gq_ref, kv_ref, wo_ref, gn_ref, o_ref, on_ref):
    x = x_ref[...]
    xn = (_rms(x) * g_ref[...]).astype(BF16)
    q = _dot_nn(xn, wq_ref[...])
    width = N_X_HEADS * HEAD_DIM
    outs = []
    for h in range(N_X_HEADS):
        sl = slice(h * HEAD_DIM, (h + 1) * HEAD_DIM)
        qh = (_rms(q[:, sl]) * gq_ref[:, sl]).astype(BF16)
        s = _dot_nt(qh, kv_ref[:, sl])
        p = jnp.exp(s - jnp.max(s, axis=-1, keepdims=True))
        l = jnp.sum(p, axis=-1, keepdims=True)
        vh = kv_ref[:, width + h * HEAD_DIM: width + (h + 1) * HEAD_DIM]
        oh = jnp.dot(p.astype(BF16), vh, preferred_element_type=F32) / l
        outs.append(oh.astype(BF16))
    o = jnp.concatenate(outs, axis=1)
    y = x + _dot_nn(o, wo_ref[...])
    o_ref[...] = y
    on_ref[...] = (_rms(y) * gn_ref[...]).astype(on_ref.dtype)


def xattn(x, g, wq, gq, kv, wo, layer, g_next, *, tm, seq):
    t, d = x.shape
    n_mem = kv.shape[0] // (t // seq)
    width = wq.shape[-1]
    per_b = seq // tm
    return pl.pallas_call(
        _xattn_kernel,
        grid=(t // tm,),
        in_specs=[
            pl.BlockSpec((tm, d), lambda i: (i, 0)),
            pl.BlockSpec((1, d), lambda i: (0, 0)),
            _layer_spec(wq, layer, (d, width), lambda i: (0, 0)),
            pl.BlockSpec((1, width), lambda i: (0, 0)),
            pl.BlockSpec((n_mem, 2 * width), lambda i: (i // per_b, 0)),
            _layer_spec(wo, layer, (width, d), lambda i: (0, 0)),
            pl.BlockSpec((1, d), lambda i: (0, 0)),
        ],
        out_specs=[pl.BlockSpec((tm, d), lambda i: (i, 0)), pl.BlockSpec((tm, d), lambda i: (i, 0))],
        out_shape=[jax.ShapeDtypeStruct((t, d), F32), jax.ShapeDtypeStruct((t, d), BF16)],
        compiler_params=_cparams(("parallel",)),
        name="xattn",
    )(x, g, wq, gq, kv, wo, g_next)


def _ret_state_kernel(k_ref, v_ref, lgb_ref, sb_ref, state_ref, kd_ref, *, chunk):
    c = pl.program_id(1)
    per_step = k_ref.shape[1] // chunk
    scale = HEAD_DIM ** -0.5

    @pl.when(c == 0)
    def _():
        state_ref[...] = jnp.zeros_like(state_ref)
        pos = lax.broadcasted_iota(jnp.int32, (chunk, HEAD_DIM), 0).astype(F32)
        for h in range(N_HEADS):
            kd_ref[h] = jnp.exp(lgb_ref[h:h + 1, :] * pos) * scale

    for u in reversed(range(per_step)):
        rows = slice(u * chunk, (u + 1) * chunk)
        for h in range(N_HEADS):
            sl = slice(h * HEAD_DIM, (h + 1) * HEAD_DIM)
            st = state_ref[h]
            sb_ref[0, u, h] = st.astype(BF16)
            kd = (k_ref[0, rows, sl].astype(F32) * kd_ref[h]).astype(BF16)
            cdec = jnp.exp(lgb_ref[h:h + 1, :] * float(chunk))
            state_ref[h] = st * cdec + _dot_tn(kd, v_ref[0, rows, sl])


def ret_states(proj, lgb, *, chunk, per_step):
    b, s, _ = proj.shape
    nc = s // chunk
    assert nc % per_step == 0
    ns = nc // per_step
    width = N_HEADS * HEAD_DIM
    return pl.pallas_call(
        functools.partial(_ret_state_kernel, chunk=chunk),
        grid=(b, ns),
        in_specs=[
            pl.BlockSpec((1, per_step * chunk, width), lambda bi, c: (bi, ns - 1 - c, 1)),
            pl.BlockSpec((1, per_step * chunk, width), lambda bi, c: (bi, ns - 1 - c, 2)),
            pl.BlockSpec((N_HEADS, LANE), lambda bi, c: (0, 0)),
        ],
        out_specs=pl.BlockSpec((1, per_step, N_HEADS, HEAD_DIM, HEAD_DIM),
                               lambda bi, c: (bi, ns - 1 - c, 0, 0, 0)),
        out_shape=jax.ShapeDtypeStruct((b, nc, N_HEADS, HEAD_DIM, HEAD_DIM), BF16),
        scratch_shapes=[pltpu.VMEM((N_HEADS, HEAD_DIM, HEAD_DIM), F32),
                        pltpu.VMEM((N_HEADS, chunk, HEAD_DIM), F32)],
        compiler_params=_cparams(("parallel", "arbitrary")),
        name="ret_states",
    )(proj, proj, lgb)


def _ret_out_kernel(q_ref, k_ref, v_ref, rg_ref, sb_ref, lgf_ref, lgb_ref, og_ref, o_ref,
                    state_ref, dmat_ref, qdf_ref, qdb_ref, kdf_ref):
    c = pl.program_id(1)
    chunk = q_ref.shape[1]
    scale = HEAD_DIM ** -0.5

    @pl.when(c == 0)
    def _():
        state_ref[...] = jnp.zeros_like(state_ref)
        pos = lax.broadcasted_iota(jnp.int32, (chunk, HEAD_DIM), 0).astype(F32)
        ri = lax.broadcasted_iota(jnp.int32, (chunk, chunk), 0)
        ci = lax.broadcasted_iota(jnp.int32, (chunk, chunk), 1)
        diff = (ri - ci).astype(F32)
        for h in range(N_HEADS):
            lf = lgf_ref[h:h + 1, :]
            lb = lgb_ref[h:h + 1, :]
            dfwd = jnp.exp(lf[:, :1] * jnp.maximum(diff, 0.0))
            dbwd = jnp.exp(lb[:, :1] * jnp.maximum(-diff, 0.0))
            dmat_ref[h] = jnp.where(diff >= 0.0, dfwd, dbwd) * scale
            qdf_ref[h] = jnp.exp(lf * (pos + 1.0))
            qdb_ref[h] = jnp.exp(lb * (float(chunk) - pos))
            kdf_ref[h] = jnp.exp(lf * (float(chunk) - 1.0 - pos)) * scale

    for h in range(N_HEADS):
        sl = slice(h * HEAD_DIM, (h + 1) * HEAD_DIM)
        qh = q_ref[0, :, sl]
        kh = k_ref[0, :, sl]
        vh = v_ref[0, :, sl]
        qf32 = qh.astype(F32)
        sd = (_dot_nt(qh, kh) * dmat_ref[h]).astype(BF16)
        st = state_ref[h]
        out = jnp.dot(sd, vh, preferred_element_type=F32)
        out += jnp.dot((qf32 * qdf_ref[h]).astype(BF16), st.astype(BF16), preferred_element_type=F32)
        out += jnp.dot((qf32 * qdb_ref[h]).astype(BF16), sb_ref[0, 0, h], preferred_element_type=F32)
        gate = rg_ref[0, :, sl].astype(F32)
        y = _rms(out) * og_ref[...]
        o_ref[0, :, sl] = (y * (gate * jax.nn.sigmoid(gate))).astype(o_ref.dtype)
        kd = (kh.astype(F32) * kdf_ref[h]).astype(BF16)
        cdec = jnp.exp(lgf_ref[h:h + 1, :] * float(chunk))
        state_ref[h] = st * cdec + _dot_tn(kd, vh)


def ret_out(proj, sb, lgf, lgb, og, *, chunk):
    b, s, _ = proj.shape
    nc = s // chunk
    width = N_HEADS * HEAD_DIM
    col = lambda k: pl.BlockSpec((1, chunk, width), lambda bi, c, k=k: (bi, c, k))
    return pl.pallas_call(
        _ret_out_kernel,
        grid=(b, nc),
        in_specs=[
            col(0), col(1), col(2), col(3),
            pl.BlockSpec((1, 1, N_HEADS, HEAD_DIM, HEAD_DIM), lambda bi, c: (bi, c, 0, 0, 0)),
            pl.BlockSpec((N_HEADS, LANE), lambda bi, c: (0, 0)),
            pl.BlockSpec((N_HEADS, LANE), lambda bi, c: (0, 0)),
            pl.BlockSpec((1, HEAD_DIM), lambda bi, c: (0, 0)),
        ],
        out_specs=pl.BlockSpec((1, chunk, width), lambda bi, c: (bi, c, 0)),
        out_shape=jax.ShapeDtypeStruct((b, s, width), BF16),
        scratch_shapes=[pltpu.VMEM((N_HEADS, HEAD_DIM, HEAD_DIM), F32),
                        pltpu.VMEM((N_HEADS, chunk, chunk), F32),
                        pltpu.VMEM((N_HEADS, chunk, HEAD_DIM), F32),
                        pltpu.VMEM((N_HEADS, chunk, HEAD_DIM), F32),
                        pltpu.VMEM((N_HEADS, chunk, HEAD_DIM), F32)],
        compiler_params=_cparams(("parallel", "arbitrary")),
        name="ret_out",
    )(proj, proj, proj, proj, sb, lgf, lgb, og)


N_DR = 2 * NA_WIN_ROWS - 1
N_DC = 2 * NA_WIN_COLS - 1


def _na_kernel(rpb_ref, q_ref, kp_ref, kc_ref, kn_ref, vp_ref, vc_ref, vn_ref, o_ref,
               bcol_ref, slab_ref, *, rows):
    qb = pl.program_id(1)
    nb = rows // NA_QROWS
    neg_tile = jnp.full((GRID_W, GRID_W), NEG_BIG, F32)

    @pl.when(qb == 0)
    def _():
        qc = lax.broadcasted_iota(jnp.int32, (GRID_W, GRID_W), 0)
        kc = lax.broadcasted_iota(jnp.int32, (GRID_W, GRID_W), 1)
        cs = jnp.clip(qc - NA_WIN_COLS // 2, 0, GRID_W - NA_WIN_COLS)
        col_ok = jnp.logical_and(kc >= cs, kc < cs + NA_WIN_COLS)
        dc = kc - qc + (NA_WIN_COLS - 1)
        for h in range(N_HEADS):
            for dr in range(N_DR):
                tile = neg_tile
                for d in range(N_DC):
                    tile = jnp.where(dc == d, rpb_ref[(h * N_DR + dr) * N_DC + d], tile)
                bcol_ref[h, dr] = jnp.where(col_ok, tile, NEG_BIG)

    def assemble(qb_static):
        for h in range(N_HEADS):
            for j in range(NA_QROWS):
                r = NA_QROWS * qb_static + j
                rs = min(max(r - NA_WIN_ROWS // 2, 0), rows - NA_WIN_ROWS)
                for i2 in range(0, 3 * NA_QROWS, 2):
                    halves = []
                    for i in (i2, i2 + 1):
                        kr = NA_QROWS * (qb_static - 1) + i
                        ok = rs <= kr < rs + NA_WIN_ROWS
                        halves.append(bcol_ref[h, kr - r + NA_WIN_ROWS - 1] if ok else neg_tile)
                    slab_ref[h, j * GRID_W:(j + 1) * GRID_W, i2 * GRID_W:(i2 + 2) * GRID_W] = (
                        jnp.concatenate(halves, axis=1))

    for qb_static in (0, 1, nb - 1):
        pl.when(qb == qb_static)(functools.partial(assemble, qb_static))

    for h in range(N_HEADS):
        sl = slice(h * HEAD_DIM, (h + 1) * HEAD_DIM)
        k = jnp.concatenate([kp_ref[0, :, sl], kc_ref[0, :, sl], kn_ref[0, :, sl]], axis=0)
        v = jnp.concatenate([vp_ref[0, :, sl], vc_ref[0, :, sl], vn_ref[0, :, sl]], axis=0)
        s = _dot_nt(q_ref[0, :, sl], k) + slab_ref[h]
        p = jnp.exp(s - jnp.max(s, axis=-1, keepdims=True))
        l = jnp.sum(p, axis=-1, keepdims=True)
        o = jnp.dot(p.astype(BF16), v, preferred_element_type=F32) / l
        o_ref[0, :, sl] = o.astype(o_ref.dtype)


def neighborhood_attention(proj, rpb):
    b, s, _ = proj.shape
    rows = s // GRID_W
    assert NA_QROWS >= NA_WIN_ROWS // 2 and NA_QROWS % 2 == 0 and rows % NA_QROWS == 0
    assert rows >= 3 * NA_QROWS and rows >= NA_WIN_ROWS
    nq = NA_QROWS * GRID_W
    nb = rows // NA_QROWS
    width = N_HEADS * HEAD_DIM

    def blk(group, shift):
        return pl.BlockSpec((1, nq, width),
                            lambda bi, qb, rp: (bi, jnp.clip(qb + shift, 0, nb - 1), group))

    grid_spec = pltpu.PrefetchScalarGridSpec(
        num_scalar_prefetch=1,
        grid=(b, nb),
        in_specs=[blk(4, 0), blk(5, -1), blk(5, 0), blk(5, 1), blk(6, -1), blk(6, 0), blk(6, 1)],
        out_specs=pl.BlockSpec((1, nq, width), lambda bi, qb, rp: (bi, qb, 0)),
        scratch_shapes=[pltpu.VMEM((N_HEADS, N_DR, GRID_W, GRID_W), F32),
                        pltpu.VMEM((N_HEADS, nq, 3 * nq), F32)],
    )
    return pl.pallas_call(
        functools.partial(_na_kernel, rows=rows),
        grid_spec=grid_spec,
        out_shape=jax.ShapeDtypeStruct((b, s, width), BF16),
        compiler_params=_cparams(("arbitrary", "arbitrary")),
        name="neighborhood_attention",
    )(rpb.astype(F32).reshape(-1), proj, proj, proj, proj, proj, proj, proj)


def _bf16_pieces(x, n):
    out, rest = [], float(x)
    for _ in range(n):
        piece = float(np.float32(rest).astype(BF16).astype(np.float32))
        out.append(piece)
        rest -= piece
    return out


LOG2E = math.log2(math.e)
LOG2E_PIECES = _bf16_pieces(LOG2E, 3)
N_ALIBI_COLS = 12
POS_LO_BITS = 7


def _alibi_cols(pos, slope, lane, unit_cols, sign):
    hi = (pos >> POS_LO_BITS).astype(F32) * (slope * float(1 << POS_LO_BITS))
    lo = (pos & ((1 << POS_LO_BITS) - 1)).astype(F32) * slope
    if sign > 0:
        return jnp.where(lane < 3, hi, jnp.where(lane < 6, lo, unit_cols))
    return jnp.where(lane < 6, unit_cols, jnp.where(lane < 9, hi, jnp.where(lane < N_ALIBI_COLS, lo, 0.0)))


def _diff_attn_kernel(slopes_ref, q_ref, k_ref, v_ref, cq_ref, ck_ref, lam_ref, og_ref, o_ref,
                      kaug_ref, qaug_ref, s_ref, bias_ref, acc_ref, *, lam_init, nsub, npv):
    h = pl.program_id(1)
    qi = pl.program_id(2)
    t = q_ref.shape[1]
    nk = k_ref.shape[1] // t
    nslab = t // LANE
    slope = slopes_ref[h]
    lane = lax.broadcasted_iota(jnp.int32, (t, LANE), 1)
    row = lax.broadcasted_iota(jnp.int32, (t, LANE), 0)

    @pl.when(qi == 0)
    def _():
        def body(r, carry):
            rs = pl.multiple_of(r * t, t)
            kx = _alibi_cols(row + rs, slope, lane, ck_ref[...], 1).astype(BF16)
            for c in range(2):
                kaug_ref[c, pl.ds(rs, t), 0:HEAD_DIM] = k_ref[0, pl.ds(rs, t), c * HEAD_DIM:(c + 1) * HEAD_DIM]
                kaug_ref[c, pl.ds(rs, t), HEAD_DIM:2 * HEAD_DIM] = kx
            return carry

        lax.fori_loop(0, nk, body, 0)
        ri = lax.broadcasted_iota(jnp.int32, (t, t), 0)
        ci = lax.broadcasted_iota(jnp.int32, (t, t), 1)
        bias_ref[0] = jnp.zeros((t, t), F32)
        bias_ref[1] = jnp.abs(ri - ci).astype(F32) * (-LOG2E * slope)

    qx = _alibi_cols(row + qi * t, slope, lane, cq_ref[...], -1)
    for c in range(2):
        qc = q_ref[0, :, c * HEAD_DIM:(c + 1) * HEAD_DIM]
        for side, ext in ((0, qx), (1, -qx), (2, jnp.zeros_like(qx))):
            qaug_ref[c, side, :, 0:HEAD_DIM] = qc
            qaug_ref[c, side, :, HEAD_DIM:2 * HEAD_DIM] = ext.astype(BF16)

    width = npv * t
    ngroups = nk // npv
    outs = []
    for c in range(2):
        def scores(g, m):
            for i in range(nsub):
                kb = g * nsub + i
                side = jnp.where(kb < qi, 0, jnp.where(kb > qi, 1, 2))
                ks = pl.multiple_of(kb * t, t)
                s = _dot_nt(qaug_ref[c, side], kaug_ref[c, pl.ds(ks, t), :])
                s = s + bias_ref[(kb == qi).astype(jnp.int32)]
                s_ref[kb] = s
                for u in range(nslab):
                    m = jnp.maximum(m, s[:, u * LANE:(u + 1) * LANE])
            return m

        m = lax.fori_loop(0, nk // nsub, scores, jnp.full((t, LANE), NEG_BIG, F32))
        mb = jnp.broadcast_to(jnp.max(m, axis=-1, keepdims=True), (t, LANE))
        if ngroups > 1:
            acc_ref[...] = jnp.zeros_like(acc_ref)

        def pv(g, lrun):
            ps = []
            for i in range(npv):
                for u in range(nslab):
                    p = jnp.exp2(s_ref[g * npv + i, :, u * LANE:(u + 1) * LANE] - mb)
                    lrun = lrun + p
                    ps.append(p.astype(BF16))
            vs = v_ref[0, pl.ds(pl.multiple_of(g * width, width), width), :]
            pv_part = jnp.dot(jnp.concatenate(ps, axis=1), vs, preferred_element_type=F32)
            acc_ref[...] = pv_part if ngroups == 1 else acc_ref[...] + pv_part
            return lrun

        trips = ngroups + jnp.minimum(qi, 0)
        lrun = lax.fori_loop(0, trips, pv, jnp.zeros((t, LANE), F32))
        outs.append(acc_ref[...] / jnp.sum(lrun, axis=-1, keepdims=True))

    lam = (jnp.exp(jnp.sum(lam_ref[0:1, :] * lam_ref[1:2, :], axis=-1, keepdims=True))
           - jnp.exp(jnp.sum(lam_ref[2:3, :] * lam_ref[3:4, :], axis=-1, keepdims=True)) + lam_init)
    o = outs[0] - lam * outs[1]
    o_ref[0] = (_rms(o) * og_ref[...] * (1.0 - lam_init)).astype(o_ref.dtype)


def diff_attention(proj, lam_params, og, *, lam_init, t):
    b, s, _ = proj.shape
    assert s % t == 0 and s <= (1 << (2 * POS_LO_BITS + 1)) and t % LANE == 0
    nk = s // t
    nsub = max(n for n in (16, 8, 4, 3, 2, 1) if nk % n == 0)
    npv = max(n for n in (16, 8, 4, 3, 2, 1) if nk % n == 0)
    vw = 2 * HEAD_DIM
    slopes = jnp.asarray(2.0 ** (-8.0 * np.arange(1, N_HEADS + 1) / N_HEADS), F32)
    pieces = np.zeros((1, LANE), np.float32)
    pieces[0, :6] = LOG2E_PIECES * 2
    cq = jnp.asarray(pieces)
    ck = jnp.asarray(-np.roll(pieces, 6, axis=1))
    const = lambda shape: pl.BlockSpec(shape, lambda bi, h, qi, sl: (0, 0))
    grid_spec = pltpu.PrefetchScalarGridSpec(
        num_scalar_prefetch=1,
        grid=(b, N_HEADS, nk),
        in_specs=[
            pl.BlockSpec((1, t, vw), lambda bi, h, qi, sl: (bi, qi, h)),
            pl.BlockSpec((1, s, vw), lambda bi, h, qi, sl: (bi, 0, N_HEADS + h)),
            pl.BlockSpec((1, s, vw), lambda bi, h, qi, sl: (bi, 0, 2 * N_HEADS + h)),
            const((1, LANE)), const((1, LANE)), const((4, HEAD_DIM)), const((1, vw)),
        ],
        out_specs=pl.BlockSpec((1, t, vw), lambda bi, h, qi, sl: (bi, qi, h)),
        scratch_shapes=[pltpu.VMEM((2, s, vw), BF16),
                        pltpu.VMEM((2, 3, t, vw), BF16),
                        pltpu.VMEM((nk, t, t), F32),
                        pltpu.VMEM((2, t, t), F32),
                        pltpu.VMEM((t, vw), F32)],
    )
    return pl.pallas_call(
        functools.partial(_diff_attn_kernel, lam_init=lam_init, nsub=nsub, npv=npv),
        grid_spec=grid_spec,
        out_shape=jax.ShapeDtypeStruct((b, s, N_HEADS * vw), BF16),
        compiler_params=_cparams(("arbitrary", "arbitrary", "arbitrary")),
        name="diff_attention",
    )(slopes, proj, proj, proj, cq, ck, lam_params, og)


def _row(v):
    return v.astype(F32).reshape(1, -1)


def _lane_bcast(v):
    return jnp.broadcast_to(v.astype(F32)[:, None], (v.shape[0], LANE))


def kernel(x, mem, norm_mix_g, norm_xattn_g, norm_mem_g, norm_ffn_g, w_in_ab, ret_decay_fwd, ret_decay_bwd, ret_out_g, na_q_g, na_k_g, na_rpb, w_out_ab, w_in_c, diff_q_g, diff_k_g, lambda_q1, lambda_k1, lambda_q2, lambda_k2, diff_out_g, w_out_c, w_xq, w_xkv, w_xo, xq_g, xk_g, w_ffn_in, w_ffn_out):
    b, s, d = x.shape
    n_mem = mem.shape[1]
    depth = norm_mix_g.shape[0]
    t = b * s
    scale = HEAD_DIM ** -0.5
    width = N_HEADS * HEAD_DIM
    xw = N_X_HEADS * HEAD_DIM
    tm, tn, tn_in = ROW_TILE, COL_TILE, MIXER_COL_TILE

    xf = x.reshape(t, d)
    memf = mem.reshape(b * n_mem, d)
    ones_cols = lambda n: jnp.ones((1, n), F32)

    for i in range(depth):
        j = i // 2
        if i % 2 == 0:
            gcol = jnp.concatenate([
                ones_cols(4 * width),
                jnp.tile(_row(na_q_g[j]) * scale, (1, N_HEADS)),
                jnp.tile(_row(na_k_g[j]), (1, N_HEADS)),
                ones_cols(width)], axis=1)
            proj = norm_matmul(xf, _row(norm_mix_g[i]), w_in_ab, j, gcol,
                               4 * width // tn_in, 6 * width // tn_in, tm=tm, tn=tn_in)
            proj = proj.reshape(b, s, 7 * width)
            lgf = _lane_bcast(jax.nn.log_sigmoid(ret_decay_fwd[j].astype(F32)))
            lgb = _lane_bcast(jax.nn.log_sigmoid(ret_decay_bwd[j].astype(F32)))
            n_chunks = s // RET_CHUNK
            sb = ret_states(proj, lgb, chunk=RET_CHUNK, per_step=math.gcd(n_chunks, RET_STATE_CHUNKS_PER_STEP))
            ret = ret_out(proj, sb, lgf, lgb, _row(ret_out_g[j]), chunk=RET_CHUNK)
            na = neighborhood_attention(proj, na_rpb[j])
            mixed = [ret.reshape(t, width), na.reshape(t, width)]
            xf = matmul_res(mixed, w_out_ab, j, xf, tm=tm, tn=tn_in, weight_outer=True)
        else:
            cw = 2 * width
            gcol = jnp.concatenate([
                jnp.tile(_row(diff_q_g[j]) * (scale * LOG2E), (1, 2 * N_HEADS)),
                jnp.tile(_row(diff_k_g[j]), (1, 2 * N_HEADS)),
                ones_cols(cw)], axis=1)
            proj = norm_matmul(xf, _row(norm_mix_g[i]), w_in_c, j, gcol, 0, 2 * cw // tn_in, tm=tm, tn=tn_in)
            proj = proj.reshape(b, s, 3 * cw)
            lam_params = jnp.stack([lambda_q1[j], lambda_k1[j], lambda_q2[j], lambda_k2[j]]).astype(F32)
            lam_init = 0.8 - 0.6 * math.exp(-0.3 * i)
            att = diff_attention(proj, lam_params, _row(diff_out_g[j]), lam_init=lam_init, t=ATTN_TILE)
            xf = matmul_res([att.reshape(t, cw)], w_out_c, j, xf, tm=tm, tn=tn_in, weight_outer=True)

        kv_gcol = jnp.concatenate([jnp.tile(_row(xk_g[i]), (1, N_X_HEADS)), ones_cols(xw)], axis=1)
        kv = norm_matmul(memf, _row(norm_mem_g[i]), w_xkv, i, kv_gcol, 0, 1, tm=b * n_mem, tn=xw)
        gq = jnp.tile(_row(xq_g[i]) * scale, (1, N_X_HEADS))
        xf, xn_ffn = xattn(xf, _row(norm_xattn_g[i]), w_xq, gq, kv, w_xo, i, _row(norm_ffn_g[i]),
                           tm=XATTN_ROW_TILE, seq=s)

        hmid = swiglu_in(xn_ffn, w_ffn_in, i, tm=tm, tn=tn)
        xf = matmul_res([hmid], w_ffn_out, i, xf, tm=tm, tn=tn, weight_outer=True)

    return xf.reshape(b, s, d)
```

```python
import functools
import math

import jax
import jax.numpy as jnp
import numpy as np
from jax import lax
from jax.experimental import pallas as pl
from jax.experimental.pallas import tpu as pltpu

F32 = jnp.float32
BF16 = jnp.bfloat16

HEAD_DIM = 128
N_HEADS = 8
N_X_HEADS = 4
GRID_W = 64
NA_WIN_ROWS = 8
NA_WIN_COLS = 16
NA_QROWS = 4
RET_CHUNK = 256
RET_STATE_CHUNKS_PER_STEP = 4

ROW_TILE = 1024
COL_TILE = 512
MIXER_COL_TILE = 1024
XATTN_ROW_TILE = 512
ATTN_TILE = 512
NORM_ROW_CHUNK = 128
X_TILE_SLABS = 4
RMS_EPS = 1e-6
NEG_BIG = -1e30

LANE = 128
VMEM_LIMIT = 56 * 1024 * 1024


def _cparams(sem):
    return pltpu.CompilerParams(dimension_semantics=sem, vmem_limit_bytes=VMEM_LIMIT)


def _dot_nt(a, b):
    return lax.dot_general(a, b, (((1,), (1,)), ((), ())), preferred_element_type=F32)


def _dot_nn(a, b):
    return lax.dot_general(a, b, (((1,), (0,)), ((), ())), preferred_element_type=F32)


def _dot_tn(a, b):
    return lax.dot_general(a, b, (((0,), (0,)), ((), ())), preferred_element_type=F32)


def _layer_spec(w, layer, block, index_map):
    if w.ndim == 2:
        return pl.BlockSpec(block, index_map)
    return pl.BlockSpec((None,) + tuple(block), lambda *idx: (layer,) + tuple(index_map(*idx)))


def _rms(x):
    return x * lax.rsqrt(jnp.mean(x * x, axis=-1, keepdims=True) + RMS_EPS)


def _norm_rows_to_scratch(x_refs, g_ref, xn_ref, rows):
    slab = x_refs[0].shape[0]
    for q, x_ref in enumerate(x_refs):
        def body(r, c, x_ref=x_ref, q=q):
            src = pl.ds(pl.multiple_of(r * rows, rows), rows)
            dst = pl.ds(pl.multiple_of(q * slab + r * rows, rows), rows)
            xn_ref[dst, :] = (_rms(x_ref[src, :]) * g_ref[...]).astype(BF16)
            return c

        lax.fori_loop(0, slab // rows, body, 0)


def _norm_matmul_kernel(*refs, lo, hi):
    *x_refs, g_ref, w_ref, gc_ref, o_ref, xn_ref = refs
    j = pl.program_id(1)

    @pl.when(j == 0)
    def _():
        _norm_rows_to_scratch(x_refs, g_ref, xn_ref, NORM_ROW_CHUNK)

    grouped = jnp.logical_and(j >= lo, j < hi)

    @pl.when(grouped)
    def _():
        acc = _dot_nn(xn_ref[...], w_ref[...])
        for c in range(acc.shape[1] // HEAD_DIM):
            sl = slice(c * HEAD_DIM, (c + 1) * HEAD_DIM)
            o_ref[:, sl] = (_rms(acc[:, sl]) * gc_ref[:, sl]).astype(o_ref.dtype)

    @pl.when(jnp.logical_not(grouped))
    def _():
        o_ref[...] = _dot_nn(xn_ref[...], w_ref[...]).astype(o_ref.dtype)


def norm_matmul(x, g, w, layer, gcol, lo, hi, *, tm, tn, out_dtype=BF16):
    t, d = x.shape
    n = w.shape[-1]
    ni, nj = t // tm, n // tn
    nslabs = X_TILE_SLABS if nj > X_TILE_SLABS else 1
    slab = tm // nslabs

    def x_spec(q):
        first_next = nj - nslabs + q

        def index(i, j):
            tile = i if nslabs == 1 else jnp.minimum(i + (j >= first_next).astype(jnp.int32), ni - 1)
            return tile * nslabs + q, 0
        return pl.BlockSpec((slab, d), index)

    return pl.pallas_call(
        functools.partial(_norm_matmul_kernel, lo=lo, hi=hi),
        grid=(ni, nj),
        in_specs=[x_spec(q) for q in range(nslabs)] + [
            pl.BlockSpec((1, d), lambda i, j: (0, 0)),
            _layer_spec(w, layer, (d, tn), lambda i, j: (0, j)),
            pl.BlockSpec((1, tn), lambda i, j: (0, j)),
        ],
        out_specs=pl.BlockSpec((tm, tn), lambda i, j: (i, j)),
        out_shape=jax.ShapeDtypeStruct((t, n), out_dtype),
        scratch_shapes=[pltpu.VMEM((tm, d), BF16)],
        compiler_params=_cparams(("arbitrary", "arbitrary")),
        name="norm_matmul",
    )(*([x] * nslabs), g, w, gcol)


def memory_kv(mem2d, g_all, w_all, gcol_all, *, tn):
    rows, d = mem2d.shape
    depth, _, n = w_all.shape
    return pl.pallas_call(
        functools.partial(_norm_matmul_kernel, lo=0, hi=1),
        grid=(depth, n // tn),
        in_specs=[
            pl.BlockSpec((rows, d), lambda l, j: (0, 0)),
            pl.BlockSpec((None, 1, d), lambda l, j: (l, 0, 0)),
            pl.BlockSpec((None, d, tn), lambda l, j: (l, 0, j)),
            pl.BlockSpec((None, 1, tn), lambda l, j: (l, 0, j)),
        ],
        out_specs=pl.BlockSpec((None, rows, tn), lambda l, j: (l, 0, j)),
        out_shape=jax.ShapeDtypeStruct((depth, rows, n), BF16),
        scratch_shapes=[pltpu.VMEM((rows, d), BF16)],
        compiler_params=_cparams(("arbitrary", "arbitrary")),
        name="memory_kv",
    )(mem2d, g_all, w_all, gcol_all)


def _swiglu_in_kernel(xn_ref, wg_ref, wu_ref, o_ref):
    xn = xn_ref[...]
    gate = _dot_nn(xn, wg_ref[...])
    up = _dot_nn(xn, wu_ref[...])
    o_ref[...] = (gate * jax.nn.sigmoid(gate) * up).astype(o_ref.dtype)


def swiglu_in(xn, w_in, layer, *, tm, tn):
    t, d = xn.shape
    hidden = w_in.shape[-1] // 2
    nh = hidden // tn
    return pl.pallas_call(
        _swiglu_in_kernel,
        grid=(t // tm, nh),
        in_specs=[
            pl.BlockSpec((tm, d), lambda i, j: (i, 0)),
            _layer_spec(w_in, layer, (d, tn), lambda i, j: (0, j)),
            _layer_spec(w_in, layer, (d, tn), lambda i, j: (0, j + nh)),
        ],
        out_specs=pl.BlockSpec((tm, tn), lambda i, j: (i, j)),
        out_shape=jax.ShapeDtypeStruct((t, hidden), BF16),
        compiler_params=_cparams(("parallel", "arbitrary")),
        name="swiglu_in",
    )(xn, w_in, w_in)


def _matmul_res_kernel(*refs):
    *a_refs, w_ref, r_ref, o_ref = refs
    acc = r_ref[...]
    k0 = 0
    for a_ref in a_refs:
        k1 = k0 + a_ref.shape[1]
        acc += _dot_nn(a_ref[...], w_ref[k0:k1, :])
        k0 = k1
    o_ref[...] = acc


def matmul_res(a_parts, w, layer, res, *, tm, tn, weight_outer=False):
    t = res.shape[0]
    k, n = w.shape[-2:]
    assert sum(a.shape[1] for a in a_parts) == k
    if weight_outer:
        grid, ij = (n // tn, t // tm), (lambda j, i: (i, j))
    else:
        grid, ij = (t // tm, n // tn), (lambda i, j: (i, j))
    row = lambda *g: (ij(*g)[0], 0)
    col = lambda *g: (0, ij(*g)[1])
    return pl.pallas_call(
        _matmul_res_kernel,
        grid=grid,
        in_specs=[pl.BlockSpec((tm, a.shape[1]), row) for a in a_parts] + [
            _layer_spec(w, layer, (k, tn), col),
            pl.BlockSpec((tm, tn), ij),
        ],
        out_specs=pl.BlockSpec((tm, tn), ij),
        out_shape=jax.ShapeDtypeStruct((t, n), F32),
        compiler_params=_cparams(("arbitrary", "arbitrary")),
        name="matmul_res",
    )(*a_parts, w, res)


def _xattn_kernel(x_ref, g_ref, wq_ref, gq_ref, kv_ref, wo_ref, gn_ref, o_ref, on_ref):
    x = x_ref[...]
    xn = (_rms(x) * g_ref[...]).astype(BF16)
    q = _dot_nn(xn, wq_ref[...])
    width = N_X_HEADS * HEAD_DIM
    outs = []
    for h in range(N_X_HEADS):
        sl = slice(h * HEAD_DIM, (h + 1) * HEAD_DIM)
        qh = (_rms(q[:, sl]) * gq_ref[:, sl]).astype(BF16)
        s = _dot_nt(qh, kv_ref[:, sl])
        p = jnp.exp(s - jnp.max(s, axis=-1, keepdims=True))
        l = jnp.sum(p, axis=-1, keepdims=True)
        vh = kv_ref[:, width + h * HEAD_DIM: width + (h + 1) * HEAD_DIM]
        oh = jnp.dot(p.astype(BF16), vh, preferred_element_type=F32) / l
        outs.append(oh.astype(BF16))
    o = jnp.concatenate(outs, axis=1)
    y = x + _dot_nn(o, wo_ref[...])
    o_ref[...] = y
    on_ref[...] = (_rms(y) * gn_ref[...]).astype(on_ref.dtype)


def xattn(x, g, wq, gq, kv, wo, layer, g_next, *, tm, seq):
    t, d = x.shape
    n_mem = kv.shape[-2] // (t // seq)
    width = wq.shape[-1]
    per_b = seq // tm
    return pl.pallas_call(
        _xattn_kernel,
        grid=(t // tm,),
        in_specs=[
            pl.BlockSpec((tm, d), lambda i: (i, 0)),
            pl.BlockSpec((1, d), lambda i: (0, 0)),
            _layer_spec(wq, layer, (d, width), lambda i: (0, 0)),
            pl.BlockSpec((1, width), lambda i: (0, 0)),
            _layer_spec(kv, layer, (n_mem, 2 * width), lambda i: (i // per_b, 0)),
            _layer_spec(wo, layer, (width, d), lambda i: (0, 0)),
            pl.BlockSpec((1, d), lambda i: (0, 0)),
        ],
        out_specs=[pl.BlockSpec((tm, d), lambda i: (i, 0)), pl.BlockSpec((tm, d), lambda i: (i, 0))],
        out_shape=[jax.ShapeDtypeStruct((t, d), F32), jax.ShapeDtypeStruct((t, d), BF16)],
        compiler_params=_cparams(("parallel",)),
        name="xattn",
    )(x, g, wq, gq, kv, wo, g_next)


def _ret_state_kernel(k_ref, v_ref, lgb_ref, sb_ref, state_ref, kd_ref, *, chunk):
    c = pl.program_id(1)
    per_step = k_ref.shape[1] // chunk
    scale = HEAD_DIM ** -0.5

    @pl.when(c == 0)
    def _():
        state_ref[...] = jnp.zeros_like(state_ref)
        pos = lax.broadcasted_iota(jnp.int32, (chunk, HEAD_DIM), 0).astype(F32)
        for h in range(N_HEADS):
            kd_ref[h] = jnp.exp(lgb_ref[h:h + 1, :] * pos) * scale

    for u in reversed(range(per_step)):
        rows = slice(u * chunk, (u + 1) * chunk)
        for h in range(N_HEADS):
            sl = slice(h * HEAD_DIM, (h + 1) * HEAD_DIM)
            st = state_ref[h]
            sb_ref[0, u, h] = st.astype(BF16)
            kd = (k_ref[0, rows, sl].astype(F32) * kd_ref[h]).astype(BF16)
            cdec = jnp.exp(lgb_ref[h:h + 1, :] * float(chunk))
            state_ref[h] = st * cdec + _dot_tn(kd, v_ref[0, rows, sl])


def ret_states(proj, lgb, *, chunk, per_step):
    b, s, _ = proj.shape
    nc = s // chunk
    assert nc % per_step == 0
    ns = nc // per_step
    width = N_HEADS * HEAD_DIM
    return pl.pallas_call(
        functools.partial(_ret_state_kernel, chunk=chunk),
        grid=(b, ns),
        in_specs=[
            pl.BlockSpec((1, per_step * chunk, width), lambda bi, c: (bi, ns - 1 - c, 1)),
            pl.BlockSpec((1, per_step * chunk, width), lambda bi, c: (bi, ns - 1 - c, 2)),
            pl.BlockSpec((N_HEADS, LANE), lambda bi, c: (0, 0)),
        ],
        out_specs=pl.BlockSpec((1, per_step, N_HEADS, HEAD_DIM, HEAD_DIM),
                               lambda bi, c: (bi, ns - 1 - c, 0, 0, 0)),
        out_shape=jax.ShapeDtypeStruct((b, nc, N_HEADS, HEAD_DIM, HEAD_DIM), BF16),
        scratch_shapes=[pltpu.VMEM((N_HEADS, HEAD_DIM, HEAD_DIM), F32),
                        pltpu.VMEM((N_HEADS, chunk, HEAD_DIM), F32)],
        compiler_params=_cparams(("parallel", "arbitrary")),
        name="ret_states",
    )(proj, proj, lgb)


def _ret_out_kernel(q_ref, k_ref, v_ref, rg_ref, sb_ref, lgf_ref, lgb_ref, og_ref, o_ref,
                    state_ref, dmat_ref, qdf_ref, qdb_ref, kdf_ref):
    c = pl.program_id(1)
    chunk = q_ref.shape[1]
    scale = HEAD_DIM ** -0.5

    @pl.when(c == 0)
    def _():
        state_ref[...] = jnp.zeros_like(state_ref)
        pos = lax.broadcasted_iota(jnp.int32, (chunk, HEAD_DIM), 0).astype(F32)
        ri = lax.broadcasted_iota(jnp.int32, (chunk, chunk), 0)
        ci = lax.broadcasted_iota(jnp.int32, (chunk, chunk), 1)
        diff = (ri - ci).astype(F32)
        for h in range(N_HEADS):
            lf = lgf_ref[h:h + 1, :]
            lb = lgb_ref[h:h + 1, :]
            dfwd = jnp.exp(lf[:, :1] * jnp.maximum(diff, 0.0))
            dbwd = jnp.exp(lb[:, :1] * jnp.maximum(-diff, 0.0))
            dmat_ref[h] = jnp.where(diff >= 0.0, dfwd, dbwd) * scale
            qdf_ref[h] = jnp.exp(lf * (pos + 1.0))
            qdb_ref[h] = jnp.exp(lb * (float(chunk) - pos))
            kdf_ref[h] = jnp.exp(lf * (float(chunk) - 1.0 - pos)) * scale

    for h in range(N_HEADS):
        sl = slice(h * HEAD_DIM, (h + 1) * HEAD_DIM)
        qh = q_ref[0, :, sl]
        kh = k_ref[0, :, sl]
        vh = v_ref[0, :, sl]
        qf32 = qh.astype(F32)
        sd = (_dot_nt(qh, kh) * dmat_ref[h]).astype(BF16)
        st = state_ref[h]
        out = jnp.dot(sd, vh, preferred_element_type=F32)
        out += jnp.dot((qf32 * qdf_ref[h]).astype(BF16), st.astype(BF16), preferred_element_type=F32)
        out += jnp.dot((qf32 * qdb_ref[h]).astype(BF16), sb_ref[0, 0, h], preferred_element_type=F32)
        gate = rg_ref[0, :, sl].astype(F32)
        y = _rms(out) * og_ref[...]
        o_ref[0, :, sl] = (y * (gate * jax.nn.sigmoid(gate))).astype(o_ref.dtype)
        kd = (kh.astype(F32) * kdf_ref[h]).astype(BF16)
        cdec = jnp.exp(lgf_ref[h:h + 1, :] * float(chunk))
        state_ref[h] = st * cdec + _dot_tn(kd, vh)


def ret_out(proj, sb, lgf, lgb, og, *, chunk):
    b, s, _ = proj.shape
    nc = s // chunk
    width = N_HEADS * HEAD_DIM
    col = lambda k: pl.BlockSpec((1, chunk, width), lambda bi, c, k=k: (bi, c, k))
    return pl.pallas_call(
        _ret_out_kernel,
        grid=(b, nc),
        in_specs=[
            col(0), col(1), col(2), col(3),
            pl.BlockSpec((1, 1, N_HEADS, HEAD_DIM, HEAD_DIM), lambda bi, c: (bi, c, 0, 0, 0)),
            pl.BlockSpec((N_HEADS, LANE), lambda bi, c: (0, 0)),
            pl.BlockSpec((N_HEADS, LANE), lambda bi, c: (0, 0)),
            pl.BlockSpec((1, HEAD_DIM), lambda bi, c: (0, 0)),
        ],
        out_specs=pl.BlockSpec((1, chunk, width), lambda bi, c: (bi, c, 0)),
        out_shape=jax.ShapeDtypeStruct((b, s, width), BF16),
        scratch_shapes=[pltpu.VMEM((N_HEADS, HEAD_DIM, HEAD_DIM), F32),
                        pltpu.VMEM((N_HEADS, chunk, chunk), F32),
                        pltpu.VMEM((N_HEADS, chunk, HEAD_DIM), F32),
                        pltpu.VMEM((N_HEADS, chunk, HEAD_DIM), F32),
                        pltpu.VMEM((N_HEADS, chunk, HEAD_DIM), F32)],
        compiler_params=_cparams(("parallel", "arbitrary")),
        name="ret_out",
    )(proj, proj, proj, proj, sb, lgf, lgb, og)


N_DR = 2 * NA_WIN_ROWS - 1
N_DC = 2 * NA_WIN_COLS - 1


def _na_kernel(rpb_ref, q_ref, kp_ref, kc_ref, kn_ref, vp_ref, vc_ref, vn_ref, o_ref,
               bcol_ref, slab_ref, *, rows):
    qb = pl.program_id(1)
    nb = rows // NA_QROWS
    neg_tile = jnp.full((GRID_W, GRID_W), NEG_BIG, F32)

    @pl.when(jnp.logical_and(pl.program_id(0) == 0, qb == 0))
    def _():
        qc = lax.broadcasted_iota(jnp.int32, (GRID_W, GRID_W), 0)
        kc = lax.broadcasted_iota(jnp.int32, (GRID_W, GRID_W), 1)
        cs = jnp.clip(qc - NA_WIN_COLS // 2, 0, GRID_W - NA_WIN_COLS)
        col_ok = jnp.logical_and(kc >= cs, kc < cs + NA_WIN_COLS)
        dc = kc - qc + (NA_WIN_COLS - 1)
        for h in range(N_HEADS):
            for dr in range(N_DR):
                tile = neg_tile
                for d in range(N_DC):
                    tile = jnp.where(dc == d, rpb_ref[(h * N_DR + dr) * N_DC + d], tile)
                bcol_ref[h, dr] = jnp.where(col_ok, tile, NEG_BIG)

    def assemble(qb_static):
        for h in range(N_HEADS):
            for j in range(NA_QROWS):
                r = NA_QROWS * qb_static + j
                rs = min(max(r - NA_WIN_ROWS // 2, 0), rows - NA_WIN_ROWS)
                for i2 in range(0, 3 * NA_QROWS, 2):
                    halves = []
                    for i in (i2, i2 + 1):
                        kr = NA_QROWS * (qb_static - 1) + i
                        ok = rs <= kr < rs + NA_WIN_ROWS
                        halves.append(bcol_ref[h, kr - r + NA_WIN_ROWS - 1] if ok else neg_tile)
                    slab_ref[h, j * GRID_W:(j + 1) * GRID_W, i2 * GRID_W:(i2 + 2) * GRID_W] = (
                        jnp.concatenate(halves, axis=1))

    for qb_static in (0, 1, nb - 1):
        pl.when(qb == qb_static)(functools.partial(assemble, qb_static))

    for h in range(N_HEADS):
        sl = slice(h * HEAD_DIM, (h + 1) * HEAD_DIM)
        k = jnp.concatenate([kp_ref[0, :, sl], kc_ref[0, :, sl], kn_ref[0, :, sl]], axis=0)
        v = jnp.concatenate([vp_ref[0, :, sl], vc_ref[0, :, sl], vn_ref[0, :, sl]], axis=0)
        s = _dot_nt(q_ref[0, :, sl], k) + slab_ref[h]
        p = jnp.exp(s - jnp.max(s, axis=-1, keepdims=True))
        l = jnp.sum(p, axis=-1, keepdims=True)
        o = jnp.dot(p.astype(BF16), v, preferred_element_type=F32) / l
        o_ref[0, :, sl] = o.astype(o_ref.dtype)


def neighborhood_attention(proj, rpb):
    b, s, _ = proj.shape
    rows = s // GRID_W
    assert NA_QROWS >= NA_WIN_ROWS // 2 and NA_QROWS % 2 == 0 and rows % NA_QROWS == 0
    assert rows >= 3 * NA_QROWS and rows >= NA_WIN_ROWS
    nq = NA_QROWS * GRID_W
    nb = rows // NA_QROWS
    width = N_HEADS * HEAD_DIM

    def blk(group, shift):
        return pl.BlockSpec((1, nq, width),
                            lambda bi, qb, rp: (bi, jnp.clip(qb + shift, 0, nb - 1), group))

    grid_spec = pltpu.PrefetchScalarGridSpec(
        num_scalar_prefetch=1,
        grid=(b, nb),
        in_specs=[blk(4, 0), blk(5, -1), blk(5, 0), blk(5, 1), blk(6, -1), blk(6, 0), blk(6, 1)],
        out_specs=pl.BlockSpec((1, nq, width), lambda bi, qb, rp: (bi, qb, 0)),
        scratch_shapes=[pltpu.VMEM((N_HEADS, N_DR, GRID_W, GRID_W), F32),
                        pltpu.VMEM((N_HEADS, nq, 3 * nq), F32)],
    )
    return pl.pallas_call(
        functools.partial(_na_kernel, rows=rows),
        grid_spec=grid_spec,
        out_shape=jax.ShapeDtypeStruct((b, s, width), BF16),
        compiler_params=_cparams(("arbitrary", "arbitrary")),
        name="neighborhood_attention",
    )(rpb.astype(F32).reshape(-1), proj, proj, proj, proj, proj, proj, proj)


def _bf16_pieces(x, n):
    out, rest = [], float(x)
    for _ in range(n):
        piece = float(np.float32(rest).astype(BF16).astype(np.float32))
        out.append(piece)
        rest -= piece
    return out


LOG2E = math.log2(math.e)
LOG2E_PIECES = _bf16_pieces(LOG2E, 3)
N_ALIBI_COLS = 12
POS_LO_BITS = 7


def _alibi_cols(pos, slope, lane, unit_cols, sign):
    hi = (pos >> POS_LO_BITS).astype(F32) * (slope * float(1 << POS_LO_BITS))
    lo = (pos & ((1 << POS_LO_BITS) - 1)).astype(F32) * slope
    if sign > 0:
        return jnp.where(lane < 3, hi, jnp.where(lane < 6, lo, unit_cols))
    return jnp.where(lane < 6, unit_cols, jnp.where(lane < 9, hi, jnp.where(lane < N_ALIBI_COLS, lo, 0.0)))


def _diff_attn_kernel(slopes_ref, q_ref, k_ref, v_ref, cq_ref, ck_ref, lam_ref, og_ref, o_ref,
                      kaug_ref, qaug_ref, s_ref, bias_ref, acc_ref, *, lam_init, nsub, npv):
    h = pl.program_id(1)
    qi = pl.program_id(2)
    t = q_ref.shape[1]
    nk = k_ref.shape[1] // t
    nslab = t // LANE
    slope = slopes_ref[h]
    lane = lax.broadcasted_iota(jnp.int32, (t, LANE), 1)
    row = lax.broadcasted_iota(jnp.int32, (t, LANE), 0)

    @pl.when(qi == 0)
    def _():
        def body(r, carry):
            rs = pl.multiple_of(r * t, t)
            kx = _alibi_cols(row + rs, slope, lane, ck_ref[...], 1).astype(BF16)
            for c in range(2):
                kaug_ref[c, pl.ds(rs, t), 0:HEAD_DIM] = k_ref[0, pl.ds(rs, t), c * HEAD_DIM:(c + 1) * HEAD_DIM]
                kaug_ref[c, pl.ds(rs, t), HEAD_DIM:2 * HEAD_DIM] = kx
            return carry

        lax.fori_loop(0, nk, body, 0)
        ri = lax.broadcasted_iota(jnp.int32, (t, t), 0)
        ci = lax.broadcasted_iota(jnp.int32, (t, t), 1)
        bias_ref[0] = jnp.zeros((t, t), F32)
        bias_ref[1] = jnp.abs(ri - ci).astype(F32) * (-LOG2E * slope)

    qx = _alibi_cols(row + qi * t, slope, lane, cq_ref[...], -1)
    for c in range(2):
        qc = q_ref[0, :, c * HEAD_DIM:(c + 1) * HEAD_DIM]
        for side, ext in ((0, qx), (1, -qx), (2, jnp.zeros_like(qx))):
            qaug_ref[c, side, :, 0:HEAD_DIM] = qc
            qaug_ref[c, side, :, HEAD_DIM:2 * HEAD_DIM] = ext.astype(BF16)

    width = npv * t
    ngroups = nk // npv
    outs = []
    for c in range(2):
        def scores(g, m):
            for i in range(nsub):
                kb = g * nsub + i
                side = jnp.where(kb < qi, 0, jnp.where(kb > qi, 1, 2))
                ks = pl.multiple_of(kb * t, t)
                s = _dot_nt(qaug_ref[c, side], kaug_ref[c, pl.ds(ks, t), :])
                s = s + bias_ref[(kb == qi).astype(jnp.int32)]
                s_ref[kb] = s
                for u in range(nslab):
                    m = jnp.maximum(m, s[:, u * LANE:(u + 1) * LANE])
            return m

        m = lax.fori_loop(0, nk // nsub, scores, jnp.full((t, LANE), NEG_BIG, F32))
        mb = jnp.broadcast_to(jnp.max(m, axis=-1, keepdims=True), (t, LANE))
        if ngroups > 1:
            acc_ref[...] = jnp.zeros_like(acc_ref)

        def pv(g, lrun):
            ps = []
            for i in range(npv):
                for u in range(nslab):
                    p = jnp.exp2(s_ref[g * npv + i, :, u * LANE:(u + 1) * LANE] - mb)
                    lrun = lrun + p
                    ps.append(p.astype(BF16))
            vs = v_ref[0, pl.ds(pl.multiple_of(g * width, width), width), :]
            pv_part = jnp.dot(jnp.concatenate(ps, axis=1), vs, preferred_element_type=F32)
            acc_ref[...] = pv_part if ngroups == 1 else acc_ref[...] + pv_part
            return lrun

        trips = ngroups + jnp.minimum(qi, 0)
        lrun = lax.fori_loop(0, trips, pv, jnp.zeros((t, LANE), F32))
        outs.append(acc_ref[...] / jnp.sum(lrun, axis=-1, keepdims=True))

    lam = (jnp.exp(jnp.sum(lam_ref[0:1, :] * lam_ref[1:2, :], axis=-1, keepdims=True))
           - jnp.exp(jnp.sum(lam_ref[2:3, :] * lam_ref[3:4, :], axis=-1, keepdims=True)) + lam_init)
    o = outs[0] - lam * outs[1]
    o_ref[0] = (_rms(o) * og_ref[...] * (1.0 - lam_init)).astype(o_ref.dtype)


def diff_attention(proj, lam_params, og, *, lam_init, t):
    b, s, _ = proj.shape
    assert s % t == 0 and s <= (1 << (2 * POS_LO_BITS + 1)) and t % LANE == 0
    nk = s // t
    nsub = max(n for n in (16, 8, 4, 3, 2, 1) if nk % n == 0)
    npv = max(n for n in (16, 8, 4, 3, 2, 1) if nk % n == 0)
    vw = 2 * HEAD_DIM
    slopes = jnp.asarray(2.0 ** (-8.0 * np.arange(1, N_HEADS + 1) / N_HEADS), F32)
    pieces = np.zeros((1, LANE), np.float32)
    pieces[0, :6] = LOG2E_PIECES * 2
    cq = jnp.asarray(pieces)
    ck = jnp.asarray(-np.roll(pieces, 6, axis=1))
    const = lambda shape: pl.BlockSpec(shape, lambda bi, h, qi, sl: (0, 0))
    grid_spec = pltpu.PrefetchScalarGridSpec(
        num_scalar_prefetch=1,
        grid=(b, N_HEADS, nk),
        in_specs=[
            pl.BlockSpec((1, t, vw), lambda bi, h, qi, sl: (bi, qi, h)),
            pl.BlockSpec((1, s, vw), lambda bi, h, qi, sl: (bi, 0, N_HEADS + h)),
            pl.BlockSpec((1, s, vw), lambda bi, h, qi, sl: (bi, 0, 2 * N_HEADS + h)),
            const((1, LANE)), const((1, LANE)), const((4, HEAD_DIM)), const((1, vw)),
        ],
        out_specs=pl.BlockSpec((1, t, vw), lambda bi, h, qi, sl: (bi, qi, h)),
        scratch_shapes=[pltpu.VMEM((2, s, vw), BF16),
                        pltpu.VMEM((2, 3, t, vw), BF16),
                        pltpu.VMEM((nk, t, t), F32),
                        pltpu.VMEM((2, t, t), F32),
                        pltpu.VMEM((t, vw), F32)],
    )
    return pl.pallas_call(
        functools.partial(_diff_attn_kernel, lam_init=lam_init, nsub=nsub, npv=npv),
        grid_spec=grid_spec,
        out_shape=jax.ShapeDtypeStruct((b, s, N_HEADS * vw), BF16),
        compiler_params=_cparams(("arbitrary", "arbitrary", "arbitrary")),
        name="diff_attention",
    )(slopes, proj, proj, proj, cq, ck, lam_params, og)


def _row(v):
    return v.astype(F32).reshape(1, -1)


def _lane_bcast(v):
    return jnp.broadcast_to(v.astype(F32)[:, None], (v.shape[0], LANE))


def kernel(x, mem, norm_mix_g, norm_xattn_g, norm_mem_g, norm_ffn_g, w_in_ab, ret_decay_fwd, ret_decay_bwd, ret_out_g, na_q_g, na_k_g, na_rpb, w_out_ab, w_in_c, diff_q_g, diff_k_g, lambda_q1, lambda_k1, lambda_q2, lambda_k2, diff_out_g, w_out_c, w_xq, w_xkv, w_xo, xq_g, xk_g, w_ffn_in, w_ffn_out):
    b, s, d = x.shape
    n_mem = mem.shape[1]
    depth = norm_mix_g.shape[0]
    t = b * s
    scale = HEAD_DIM ** -0.5
    width = N_HEADS * HEAD_DIM
    xw = N_X_HEADS * HEAD_DIM
    tm, tn, tn_in = ROW_TILE, COL_TILE, MIXER_COL_TILE

    xf = x.reshape(t, d)
    memf = mem.reshape(b * n_mem, d)
    ones_cols = lambda n: jnp.ones((1, n), F32)
    kv_gcol = jnp.concatenate([jnp.tile(xk_g.astype(F32), (1, N_X_HEADS)), jnp.ones((depth, xw), F32)], axis=1)
    kv_all = memory_kv(memf, norm_mem_g.astype(F32)[:, None, :], w_xkv, kv_gcol[:, None, :], tn=xw)

    for i in range(depth):
        j = i // 2
        if i % 2 == 0:
            gcol = jnp.concatenate([
                ones_cols(4 * width),
                jnp.tile(_row(na_q_g[j]) * scale, (1, N_HEADS)),
                jnp.tile(_row(na_k_g[j]), (1, N_HEADS)),
                ones_cols(width)], axis=1)
            proj = norm_matmul(xf, _row(norm_mix_g[i]), w_in_ab, j, gcol,
                               4 * width // tn_in, 6 * width // tn_in, tm=tm, tn=tn_in)
            proj = proj.reshape(b, s, 7 * width)
            lgf = _lane_bcast(jax.nn.log_sigmoid(ret_decay_fwd[j].astype(F32)))
            lgb = _lane_bcast(jax.nn.log_sigmoid(ret_decay_bwd[j].astype(F32)))
            n_chunks = s // RET_CHUNK
            sb = ret_states(proj, lgb, chunk=RET_CHUNK, per_step=math.gcd(n_chunks, RET_STATE_CHUNKS_PER_STEP))
            ret = ret_out(proj, sb, lgf, lgb, _row(ret_out_g[j]), chunk=RET_CHUNK)
            na = neighborhood_attention(proj, na_rpb[j])
            mixed = [ret.reshape(t, width), na.reshape(t, width)]
            xf = matmul_res(mixed, w_out_ab, j, xf, tm=tm, tn=tn_in, weight_outer=True)
        else:
            cw = 2 * width
            gcol = jnp.concatenate([
                jnp.tile(_row(diff_q_g[j]) * (scale * LOG2E), (1, 2 * N_HEADS)),
                jnp.tile(_row(diff_k_g[j]), (1, 2 * N_HEADS)),
                ones_cols(cw)], axis=1)
            proj = norm_matmul(xf, _row(norm_mix_g[i]), w_in_c, j, gcol, 0, 2 * cw // tn_in, tm=tm, tn=tn_in)
            proj = proj.reshape(b, s, 3 * cw)
            lam_params = jnp.stack([lambda_q1[j], lambda_k1[j], lambda_q2[j], lambda_k2[j]]).astype(F32)
            lam_init = 0.8 - 0.6 * math.exp(-0.3 * i)
            att = diff_attention(proj, lam_params, _row(diff_out_g[j]), lam_init=lam_init, t=ATTN_TILE)
            xf = matmul_res([att.reshape(t, cw)], w_out_c, j, xf, tm=tm, tn=tn_in, weight_outer=True)

        gq = jnp.tile(_row(xq_g[i]) * scale, (1, N_X_HEADS))
        xf, xn_ffn = xattn(xf, _row(norm_xattn_g[i]), w_xq, gq, kv_all, w_xo, i, _row(norm_ffn_g[i]),
                           tm=XATTN_ROW_TILE, seq=s)

        hmid = swiglu_in(xn_ffn, w_ffn_in, i, tm=tm, tn=tn)
        xf = matmul_res([hmid], w_ffn_out, i, xf, tm=tm, tn=tn, weight_outer=True)

    return xf.reshape(b, s, d)
```

```python
import functools
import math

import jax
import jax.numpy as jnp
import numpy as np
from jax import lax
from jax.experimental import pallas as pl
from jax.experimental.pallas import tpu as pltpu

F32 = jnp.float32
BF16 = jnp.bfloat16

HEAD_DIM = 128
N_HEADS = 8
N_X_HEADS = 4
GRID_W = 64
NA_WIN_ROWS = 8
NA_WIN_COLS = 16
NA_QROWS = 4
RET_CHUNK = 256
RET_STATE_CHUNKS_PER_STEP = 4
RET_OUT_CHUNKS_PER_STEP = 2

ROW_TILE = 1024
COL_TILE = 512
MIXER_COL_TILE = 1024
XATTN_ROW_TILE = 512
ATTN_TILE = 512
NORM_ROW_CHUNK = 128
X_TILE_SLABS = 4
RMS_EPS = 1e-6
NEG_BIG = -1e30

LANE = 128
VMEM_LIMIT = 56 * 1024 * 1024


def _cparams(sem):
    return pltpu.CompilerParams(dimension_semantics=sem, vmem_limit_bytes=VMEM_LIMIT)


def _dot_nt(a, b):
    return lax.dot_general(a, b, (((1,), (1,)), ((), ())), preferred_element_type=F32)


def _dot_nn(a, b):
    return lax.dot_general(a, b, (((1,), (0,)), ((), ())), preferred_element_type=F32)


def _dot_tn(a, b):
    return lax.dot_general(a, b, (((0,), (0,)), ((), ())), preferred_element_type=F32)


def _layer_spec(w, layer, block, index_map):
    if w.ndim == 2:
        return pl.BlockSpec(block, index_map)
    return pl.BlockSpec((None,) + tuple(block), lambda *idx: (layer,) + tuple(index_map(*idx)))


def _rms(x):
    return x * lax.rsqrt(jnp.mean(x * x, axis=-1, keepdims=True) + RMS_EPS)


def _norm_rows_to_scratch(x_refs, g_ref, xn_ref, rows):
    slab = x_refs[0].shape[0]
    for q, x_ref in enumerate(x_refs):
        def body(r, c, x_ref=x_ref, q=q):
            src = pl.ds(pl.multiple_of(r * rows, rows), rows)
            dst = pl.ds(pl.multiple_of(q * slab + r * rows, rows), rows)
            xn_ref[dst, :] = (_rms(x_ref[src, :]) * g_ref[...]).astype(BF16)
            return c

        lax.fori_loop(0, slab // rows, body, 0)


def _norm_matmul_kernel(*refs, lo, hi):
    *x_refs, g_ref, w_ref, gc_ref, o_ref, xn_ref = refs
    j = pl.program_id(1)

    @pl.when(j == 0)
    def _():
        _norm_rows_to_scratch(x_refs, g_ref, xn_ref, NORM_ROW_CHUNK)

    grouped = jnp.logical_and(j >= lo, j < hi)

    @pl.when(grouped)
    def _():
        acc = _dot_nn(xn_ref[...], w_ref[...])
        for c in range(acc.shape[1] // HEAD_DIM):
            sl = slice(c * HEAD_DIM, (c + 1) * HEAD_DIM)
            o_ref[:, sl] = (_rms(acc[:, sl]) * gc_ref[:, sl]).astype(o_ref.dtype)

    @pl.when(jnp.logical_not(grouped))
    def _():
        o_ref[...] = _dot_nn(xn_ref[...], w_ref[...]).astype(o_ref.dtype)


def norm_matmul(x, g, w, layer, gcol, lo, hi, *, tm, tn, out_dtype=BF16):
    t, d = x.shape
    n = w.shape[-1]
    ni, nj = t // tm, n // tn
    nslabs = X_TILE_SLABS if nj > X_TILE_SLABS else 1
    slab = tm // nslabs

    def x_spec(q):
        first_next = nj - nslabs + q

        def index(i, j):
            tile = i if nslabs == 1 else jnp.minimum(i + (j >= first_next).astype(jnp.int32), ni - 1)
            return tile * nslabs + q, 0
        return pl.BlockSpec((slab, d), index)

    return pl.pallas_call(
        functools.partial(_norm_matmul_kernel, lo=lo, hi=hi),
        grid=(ni, nj),
        in_specs=[x_spec(q) for q in range(nslabs)] + [
            pl.BlockSpec((1, d), lambda i, j: (0, 0)),
            _layer_spec(w, layer, (d, tn), lambda i, j: (0, j)),
            pl.BlockSpec((1, tn), lambda i, j: (0, j)),
        ],
        out_specs=pl.BlockSpec((tm, tn), lambda i, j: (i, j)),
        out_shape=jax.ShapeDtypeStruct((t, n), out_dtype),
        scratch_shapes=[pltpu.VMEM((tm, d), BF16)],
        compiler_params=_cparams(("arbitrary", "arbitrary")),
        name="norm_matmul",
    )(*([x] * nslabs), g, w, gcol)


def memory_kv(mem2d, g_all, w_all, gcol_all, *, tn):
    rows, d = mem2d.shape
    depth, _, n = w_all.shape
    return pl.pallas_call(
        functools.partial(_norm_matmul_kernel, lo=0, hi=1),
        grid=(depth, n // tn),
        in_specs=[
            pl.BlockSpec((rows, d), lambda l, j: (0, 0)),
            pl.BlockSpec((None, 1, d), lambda l, j: (l, 0, 0)),
            pl.BlockSpec((None, d, tn), lambda l, j: (l, 0, j)),
            pl.BlockSpec((None, 1, tn), lambda l, j: (l, 0, j)),
        ],
        out_specs=pl.BlockSpec((None, rows, tn), lambda l, j: (l, 0, j)),
        out_shape=jax.ShapeDtypeStruct((depth, rows, n), BF16),
        scratch_shapes=[pltpu.VMEM((rows, d), BF16)],
        compiler_params=_cparams(("arbitrary", "arbitrary")),
        name="memory_kv",
    )(mem2d, g_all, w_all, gcol_all)


def _swiglu_in_kernel(xn_ref, wg_ref, wu_ref, o_ref):
    xn = xn_ref[...]
    gate = _dot_nn(xn, wg_ref[...])
    up = _dot_nn(xn, wu_ref[...])
    o_ref[...] = (gate * jax.nn.sigmoid(gate) * up).astype(o_ref.dtype)


def swiglu_in(xn, w_in, layer, *, tm, tn):
    t, d = xn.shape
    hidden = w_in.shape[-1] // 2
    nh = hidden // tn
    return pl.pallas_call(
        _swiglu_in_kernel,
        grid=(t // tm, nh),
        in_specs=[
            pl.BlockSpec((tm, d), lambda i, j: (i, 0)),
            _layer_spec(w_in, layer, (d, tn), lambda i, j: (0, j)),
            _layer_spec(w_in, layer, (d, tn), lambda i, j: (0, j + nh)),
        ],
        out_specs=pl.BlockSpec((tm, tn), lambda i, j: (i, j)),
        out_shape=jax.ShapeDtypeStruct((t, hidden), BF16),
        compiler_params=_cparams(("parallel", "arbitrary")),
        name="swiglu_in",
    )(xn, w_in, w_in)


def _matmul_res_kernel(*refs):
    *a_refs, w_ref, r_ref, o_ref = refs
    acc = r_ref[...]
    k0 = 0
    for a_ref in a_refs:
        k1 = k0 + a_ref.shape[1]
        acc += _dot_nn(a_ref[...], w_ref[k0:k1, :])
        k0 = k1
    o_ref[...] = acc


def matmul_res(a_parts, w, layer, res, *, tm, tn, weight_outer=False):
    t = res.shape[0]
    k, n = w.shape[-2:]
    assert sum(a.shape[1] for a in a_parts) == k
    if weight_outer:
        grid, ij = (n // tn, t // tm), (lambda j, i: (i, j))
    else:
        grid, ij = (t // tm, n // tn), (lambda i, j: (i, j))
    row = lambda *g: (ij(*g)[0], 0)
    col = lambda *g: (0, ij(*g)[1])
    return pl.pallas_call(
        _matmul_res_kernel,
        grid=grid,
        in_specs=[pl.BlockSpec((tm, a.shape[1]), row) for a in a_parts] + [
            _layer_spec(w, layer, (k, tn), col),
            pl.BlockSpec((tm, tn), ij),
        ],
        out_specs=pl.BlockSpec((tm, tn), ij),
        out_shape=jax.ShapeDtypeStruct((t, n), F32),
        compiler_params=_cparams(("arbitrary", "arbitrary")),
        name="matmul_res",
    )(*a_parts, w, res)


def _xattn_kernel(x_ref, g_ref, wq_ref, gq_ref, kv_ref, wo_ref, gn_ref, o_ref, on_ref):
    x = x_ref[...]
    xn = (_rms(x) * g_ref[...]).astype(BF16)
    q = _dot_nn(xn, wq_ref[...])
    width = N_X_HEADS * HEAD_DIM
    outs = []
    for h in range(N_X_HEADS):
        sl = slice(h * HEAD_DIM, (h + 1) * HEAD_DIM)
        qh = (_rms(q[:, sl]) * gq_ref[:, sl]).astype(BF16)
        s = _dot_nt(qh, kv_ref[:, sl])
        p = jnp.exp(s - jnp.max(s, axis=-1, keepdims=True))
        l = jnp.sum(p, axis=-1, keepdims=True)
        vh = kv_ref[:, width + h * HEAD_DIM: width + (h + 1) * HEAD_DIM]
        oh = jnp.dot(p.astype(BF16), vh, preferred_element_type=F32) / l
        outs.append(oh.astype(BF16))
    o = jnp.concatenate(outs, axis=1)
    y = x + _dot_nn(o, wo_ref[...])
    o_ref[...] = y
    on_ref[...] = (_rms(y) * gn_ref[...]).astype(on_ref.dtype)


def xattn(x, g, wq, gq, kv, wo, layer, g_next, *, tm, seq):
    t, d = x.shape
    n_mem = kv.shape[-2] // (t // seq)
    width = wq.shape[-1]
    per_b = seq // tm
    return pl.pallas_call(
        _xattn_kernel,
        grid=(t // tm,),
        in_specs=[
            pl.BlockSpec((tm, d), lambda i: (i, 0)),
            pl.BlockSpec((1, d), lambda i: (0, 0)),
            _layer_spec(wq, layer, (d, width), lambda i: (0, 0)),
            pl.BlockSpec((1, width), lambda i: (0, 0)),
            _layer_spec(kv, layer, (n_mem, 2 * width), lambda i: (i // per_b, 0)),
            _layer_spec(wo, layer, (width, d), lambda i: (0, 0)),
            pl.BlockSpec((1, d), lambda i: (0, 0)),
        ],
        out_specs=[pl.BlockSpec((tm, d), lambda i: (i, 0)), pl.BlockSpec((tm, d), lambda i: (i, 0))],
        out_shape=[jax.ShapeDtypeStruct((t, d), F32), jax.ShapeDtypeStruct((t, d), BF16)],
        compiler_params=_cparams(("parallel",)),
        name="xattn",
    )(x, g, wq, gq, kv, wo, g_next)


def _ret_state_kernel(k_ref, v_ref, lgb_ref, sb_ref, state_ref, kd_ref, *, chunk):
    c = pl.program_id(1)
    per_step = k_ref.shape[1] // chunk
    scale = HEAD_DIM ** -0.5

    @pl.when(c == 0)
    def _():
        state_ref[...] = jnp.zeros_like(state_ref)
        pos = lax.broadcasted_iota(jnp.int32, (chunk, HEAD_DIM), 0).astype(F32)
        for h in range(N_HEADS):
            kd_ref[h] = jnp.exp(lgb_ref[h:h + 1, :] * pos) * scale

    for u in reversed(range(per_step)):
        rows = slice(u * chunk, (u + 1) * chunk)
        for h in range(N_HEADS):
            sl = slice(h * HEAD_DIM, (h + 1) * HEAD_DIM)
            st = state_ref[h]
            sb_ref[0, u, h] = st.astype(BF16)
            kd = (k_ref[0, rows, sl].astype(F32) * kd_ref[h]).astype(BF16)
            cdec = jnp.exp(lgb_ref[h:h + 1, :] * float(chunk))
            state_ref[h] = st * cdec + _dot_tn(kd, v_ref[0, rows, sl])


def ret_states(proj, lgb, *, chunk, per_step):
    b, s, _ = proj.shape
    nc = s // chunk
    assert nc % per_step == 0
    ns = nc // per_step
    width = N_HEADS * HEAD_DIM
    return pl.pallas_call(
        functools.partial(_ret_state_kernel, chunk=chunk),
        grid=(b, ns),
        in_specs=[
            pl.BlockSpec((1, per_step * chunk, width), lambda bi, c: (bi, ns - 1 - c, 1)),
            pl.BlockSpec((1, per_step * chunk, width), lambda bi, c: (bi, ns - 1 - c, 2)),
            pl.BlockSpec((N_HEADS, LANE), lambda bi, c: (0, 0)),
        ],
        out_specs=pl.BlockSpec((1, per_step, N_HEADS, HEAD_DIM, HEAD_DIM),
                               lambda bi, c: (bi, ns - 1 - c, 0, 0, 0)),
        out_shape=jax.ShapeDtypeStruct((b, nc, N_HEADS, HEAD_DIM, HEAD_DIM), BF16),
        scratch_shapes=[pltpu.VMEM((N_HEADS, HEAD_DIM, HEAD_DIM), F32),
                        pltpu.VMEM((N_HEADS, chunk, HEAD_DIM), F32)],
        compiler_params=_cparams(("parallel", "arbitrary")),
        name="ret_states",
    )(proj, proj, lgb)


def _ret_out_kernel(q_ref, k_ref, v_ref, rg_ref, sb_ref, lgf_ref, lgb_ref, og_ref, o_ref,
                    state_ref, dmat_ref, qdf_ref, qdb_ref, kdf_ref, *, chunk):
    c = pl.program_id(1)
    per_step = q_ref.shape[1] // chunk
    scale = HEAD_DIM ** -0.5

    @pl.when(c == 0)
    def _():
        state_ref[...] = jnp.zeros_like(state_ref)
        pos = lax.broadcasted_iota(jnp.int32, (chunk, HEAD_DIM), 0).astype(F32)
        ri = lax.broadcasted_iota(jnp.int32, (chunk, chunk), 0)
        ci = lax.broadcasted_iota(jnp.int32, (chunk, chunk), 1)
        diff = (ri - ci).astype(F32)
        for h in range(N_HEADS):
            lf = lgf_ref[h:h + 1, :]
            lb = lgb_ref[h:h + 1, :]
            dfwd = jnp.exp(lf[:, :1] * jnp.maximum(diff, 0.0))
            dbwd = jnp.exp(lb[:, :1] * jnp.maximum(-diff, 0.0))
            dmat_ref[h] = jnp.where(diff >= 0.0, dfwd, dbwd) * scale
            qdf_ref[h] = jnp.exp(lf * (pos + 1.0))
            qdb_ref[h] = jnp.exp(lb * (float(chunk) - pos))
            kdf_ref[h] = jnp.exp(lf * (float(chunk) - 1.0 - pos)) * scale

    for u in range(per_step):
        rows = slice(u * chunk, (u + 1) * chunk)
        for h in range(N_HEADS):
            sl = slice(h * HEAD_DIM, (h + 1) * HEAD_DIM)
            qh = q_ref[0, rows, sl]
            kh = k_ref[0, rows, sl]
            vh = v_ref[0, rows, sl]
            qf32 = qh.astype(F32)
            sd = (_dot_nt(qh, kh) * dmat_ref[h]).astype(BF16)
            st = state_ref[h]
            out = jnp.dot(sd, vh, preferred_element_type=F32)
            out += jnp.dot((qf32 * qdf_ref[h]).astype(BF16), st.astype(BF16), preferred_element_type=F32)
            out += jnp.dot((qf32 * qdb_ref[h]).astype(BF16), sb_ref[0, u, h], preferred_element_type=F32)
            gate = rg_ref[0, rows, sl].astype(F32)
            y = _rms(out) * og_ref[...]
            o_ref[0, rows, sl] = (y * (gate * jax.nn.sigmoid(gate))).astype(o_ref.dtype)
            kd = (kh.astype(F32) * kdf_ref[h]).astype(BF16)
            cdec = jnp.exp(lgf_ref[h:h + 1, :] * float(chunk))
            state_ref[h] = st * cdec + _dot_tn(kd, vh)


def ret_out(proj, sb, lgf, lgb, og, *, chunk, per_step):
    b, s, _ = proj.shape
    nc = s // chunk
    assert nc % per_step == 0
    width = N_HEADS * HEAD_DIM
    col = lambda k: pl.BlockSpec((1, per_step * chunk, width), lambda bi, c, k=k: (bi, c, k))
    return pl.pallas_call(
        functools.partial(_ret_out_kernel, chunk=chunk),
        grid=(b, nc // per_step),
        in_specs=[
            col(0), col(1), col(2), col(3),
            pl.BlockSpec((1, per_step, N_HEADS, HEAD_DIM, HEAD_DIM), lambda bi, c: (bi, c, 0, 0, 0)),
            pl.BlockSpec((N_HEADS, LANE), lambda bi, c: (0, 0)),
            pl.BlockSpec((N_HEADS, LANE), lambda bi, c: (0, 0)),
            pl.BlockSpec((1, HEAD_DIM), lambda bi, c: (0, 0)),
        ],
        out_specs=pl.BlockSpec((1, per_step * chunk, width), lambda bi, c: (bi, c, 0)),
        out_shape=jax.ShapeDtypeStruct((b, s, width), BF16),
        scratch_shapes=[pltpu.VMEM((N_HEADS, HEAD_DIM, HEAD_DIM), F32),
                        pltpu.VMEM((N_HEADS, chunk, chunk), F32),
                        pltpu.VMEM((N_HEADS, chunk, HEAD_DIM), F32),
                        pltpu.VMEM((N_HEADS, chunk, HEAD_DIM), F32),
                        pltpu.VMEM((N_HEADS, chunk, HEAD_DIM), F32)],
        compiler_params=_cparams(("parallel", "arbitrary")),
        name="ret_out",
    )(proj, proj, proj, proj, sb, lgf, lgb, og)


N_DR = 2 * NA_WIN_ROWS - 1
N_DC = 2 * NA_WIN_COLS - 1


def _na_kernel(rpb_ref, q_ref, kp_ref, kc_ref, kn_ref, vp_ref, vc_ref, vn_ref, o_ref,
               bcol_ref, slab_ref, *, rows):
    qb = pl.program_id(1)
    nb = rows // NA_QROWS
    neg_tile = jnp.full((GRID_W, GRID_W), NEG_BIG, F32)

    @pl.when(jnp.logical_and(pl.program_id(0) == 0, qb == 0))
    def _():
        qc = lax.broadcasted_iota(jnp.int32, (GRID_W, GRID_W), 0)
        kc = lax.broadcasted_iota(jnp.int32, (GRID_W, GRID_W), 1)
        cs = jnp.clip(qc - NA_WIN_COLS // 2, 0, GRID_W - NA_WIN_COLS)
        col_ok = jnp.logical_and(kc >= cs, kc < cs + NA_WIN_COLS)
        dc = kc - qc + (NA_WIN_COLS - 1)
        for h in range(N_HEADS):
            for dr in range(N_DR):
                tile = neg_tile
                for d in range(N_DC):
                    tile = jnp.where(dc == d, rpb_ref[(h * N_DR + dr) * N_DC + d], tile)
                bcol_ref[h, dr] = jnp.where(col_ok, tile, NEG_BIG)

    def assemble(qb_static):
        for h in range(N_HEADS):
            for j in range(NA_QROWS):
                r = NA_QROWS * qb_static + j
                rs = min(max(r - NA_WIN_ROWS // 2, 0), rows - NA_WIN_ROWS)
                for i2 in range(0, 3 * NA_QROWS, 2):
                    halves = []
                    for i in (i2, i2 + 1):
                        kr = NA_QROWS * (qb_static - 1) + i
                        ok = rs <= kr < rs + NA_WIN_ROWS
                        halves.append(bcol_ref[h, kr - r + NA_WIN_ROWS - 1] if ok else neg_tile)
                    slab_ref[h, j * GRID_W:(j + 1) * GRID_W, i2 * GRID_W:(i2 + 2) * GRID_W] = (
                        jnp.concatenate(halves, axis=1))

    for qb_static in (0, 1, nb - 1):
        pl.when(qb == qb_static)(functools.partial(assemble, qb_static))

    for h in range(N_HEADS):
        sl = slice(h * HEAD_DIM, (h + 1) * HEAD_DIM)
        k = jnp.concatenate([kp_ref[0, :, sl], kc_ref[0, :, sl], kn_ref[0, :, sl]], axis=0)
        v = jnp.concatenate([vp_ref[0, :, sl], vc_ref[0, :, sl], vn_ref[0, :, sl]], axis=0)
        s = _dot_nt(q_ref[0, :, sl], k) + slab_ref[h]
        p = jnp.exp(s - jnp.max(s, axis=-1, keepdims=True))
        l = jnp.sum(p, axis=-1, keepdims=True)
        o = jnp.dot(p.astype(BF16), v, preferred_element_type=F32) / l
        o_ref[0, :, sl] = o.astype(o_ref.dtype)


def neighborhood_attention(proj, rpb):
    b, s, _ = proj.shape
    rows = s // GRID_W
    assert NA_QROWS >= NA_WIN_ROWS // 2 and NA_QROWS % 2 == 0 and rows % NA_QROWS == 0
    assert rows >= 3 * NA_QROWS and rows >= NA_WIN_ROWS
    nq = NA_QROWS * GRID_W
    nb = rows // NA_QROWS
    width = N_HEADS * HEAD_DIM

    def blk(group, shift):
        return pl.BlockSpec((1, nq, width),
                            lambda bi, qb, rp: (bi, jnp.clip(qb + shift, 0, nb - 1), group))

    grid_spec = pltpu.PrefetchScalarGridSpec(
        num_scalar_prefetch=1,
        grid=(b, nb),
        in_specs=[blk(4, 0), blk(5, -1), blk(5, 0), blk(5, 1), blk(6, -1), blk(6, 0), blk(6, 1)],
        out_specs=pl.BlockSpec((1, nq, width), lambda bi, qb, rp: (bi, qb, 0)),
        scratch_shapes=[pltpu.VMEM((N_HEADS, N_DR, GRID_W, GRID_W), F32),
                        pltpu.VMEM((N_HEADS, nq, 3 * nq), F32)],
    )
    return pl.pallas_call(
        functools.partial(_na_kernel, rows=rows),
        grid_spec=grid_spec,
        out_shape=jax.ShapeDtypeStruct((b, s, width), BF16),
        compiler_params=_cparams(("arbitrary", "arbitrary")),
        name="neighborhood_attention",
    )(rpb.astype(F32).reshape(-1), proj, proj, proj, proj, proj, proj, proj)


def _bf16_pieces(x, n):
    out, rest = [], float(x)
    for _ in range(n):
        piece = float(np.float32(rest).astype(BF16).astype(np.float32))
        out.append(piece)
        rest -= piece
    return out


LOG2E = math.log2(math.e)
LOG2E_PIECES = _bf16_pieces(LOG2E, 3)
N_ALIBI_COLS = 12
POS_LO_BITS = 7


def _alibi_cols(pos, slope, lane, unit_cols, sign):
    hi = (pos >> POS_LO_BITS).astype(F32) * (slope * float(1 << POS_LO_BITS))
    lo = (pos & ((1 << POS_LO_BITS) - 1)).astype(F32) * slope
    if sign > 0:
        return jnp.where(lane < 3, hi, jnp.where(lane < 6, lo, unit_cols))
    return jnp.where(lane < 6, unit_cols, jnp.where(lane < 9, hi, jnp.where(lane < N_ALIBI_COLS, lo, 0.0)))


def _diff_attn_kernel(slopes_ref, q_ref, k_ref, v_ref, cq_ref, ck_ref, lam_ref, og_ref, o_ref,
                      kaug_ref, qaug_ref, s_ref, bias_ref, acc_ref, *, lam_init, nsub, npv):
    h = pl.program_id(1)
    qi = pl.program_id(2)
    t = q_ref.shape[1]
    nk = k_ref.shape[1] // t
    nslab = t // LANE
    slope = slopes_ref[h]
    lane = lax.broadcasted_iota(jnp.int32, (t, LANE), 1)
    row = lax.broadcasted_iota(jnp.int32, (t, LANE), 0)

    @pl.when(qi == 0)
    def _():
        def body(r, carry):
            rs = pl.multiple_of(r * t, t)
            kx = _alibi_cols(row + rs, slope, lane, ck_ref[...], 1).astype(BF16)
            for c in range(2):
                kaug_ref[c, pl.ds(rs, t), 0:HEAD_DIM] = k_ref[0, pl.ds(rs, t), c * HEAD_DIM:(c + 1) * HEAD_DIM]
                kaug_ref[c, pl.ds(rs, t), HEAD_DIM:2 * HEAD_DIM] = kx
            return carry

        lax.fori_loop(0, nk, body, 0)
        ri = lax.broadcasted_iota(jnp.int32, (t, t), 0)
        ci = lax.broadcasted_iota(jnp.int32, (t, t), 1)
        bias_ref[0] = jnp.zeros((t, t), F32)
        bias_ref[1] = jnp.abs(ri - ci).astype(F32) * (-LOG2E * slope)

    qx = _alibi_cols(row + qi * t, slope, lane, cq_ref[...], -1)
    for c in range(2):
        qc = q_ref[0, :, c * HEAD_DIM:(c + 1) * HEAD_DIM]
        for side, ext in ((0, qx), (1, -qx), (2, jnp.zeros_like(qx))):
            qaug_ref[c, side, :, 0:HEAD_DIM] = qc
            qaug_ref[c, side, :, HEAD_DIM:2 * HEAD_DIM] = ext.astype(BF16)

    width = npv * t
    ngroups = nk // npv
    outs = []
    for c in range(2):
        def scores(g, m):
            for i in range(nsub):
                kb = g * nsub + i
                side = jnp.where(kb < qi, 0, jnp.where(kb > qi, 1, 2))
                ks = pl.multiple_of(kb * t, t)
                s = _dot_nt(qaug_ref[c, side], kaug_ref[c, pl.ds(ks, t), :])
                s = s + bias_ref[(kb == qi).astype(jnp.int32)]
                s_ref[kb] = s
                for u in range(nslab):
                    m = jnp.maximum(m, s[:, u * LANE:(u + 1) * LANE])
            return m

        m = lax.fori_loop(0, nk // nsub, scores, jnp.full((t, LANE), NEG_BIG, F32))
        mb = jnp.broadcast_to(jnp.max(m, axis=-1, keepdims=True), (t, LANE))
        if ngroups > 1:
            acc_ref[...] = jnp.zeros_like(acc_ref)

        def pv(g, lrun):
            ps = []
            for i in range(npv):
                for u in range(nslab):
                    p = jnp.exp2(s_ref[g * npv + i, :, u * LANE:(u + 1) * LANE] - mb)
                    lrun = lrun + p
                    ps.append(p.astype(BF16))
            vs = v_ref[0, pl.ds(pl.multiple_of(g * width, width), width), :]
            pv_part = jnp.dot(jnp.concatenate(ps, axis=1), vs, preferred_element_type=F32)
            acc_ref[...] = pv_part if ngroups == 1 else acc_ref[...] + pv_part
            return lrun

        trips = ngroups + jnp.minimum(qi, 0)
        lrun = lax.fori_loop(0, trips, pv, jnp.zeros((t, LANE), F32))
        outs.append(acc_ref[...] / jnp.sum(lrun, axis=-1, keepdims=True))

    lam = (jnp.exp(jnp.sum(lam_ref[0:1, :] * lam_ref[1:2, :], axis=-1, keepdims=True))
           - jnp.exp(jnp.sum(lam_ref[2:3, :] * lam_ref[3:4, :], axis=-1, keepdims=True)) + lam_init)
    o = outs[0] - lam * outs[1]
    o_ref[0] = (_rms(o) * og_ref[...] * (1.0 - lam_init)).astype(o_ref.dtype)


def diff_attention(proj, lam_params, og, *, lam_init, t):
    b, s, _ = proj.shape
    assert s % t == 0 and s <= (1 << (2 * POS_LO_BITS + 1)) and t % LANE == 0
    nk = s // t
    nsub = max(n for n in (16, 8, 4, 3, 2, 1) if nk % n == 0)
    npv = max(n for n in (16, 8, 4, 3, 2, 1) if nk % n == 0)
    vw = 2 * HEAD_DIM
    slopes = jnp.asarray(2.0 ** (-8.0 * np.arange(1, N_HEADS + 1) / N_HEADS), F32)
    pieces = np.zeros((1, LANE), np.float32)
    pieces[0, :6] = LOG2E_PIECES * 2
    cq = jnp.asarray(pieces)
    ck = jnp.asarray(-np.roll(pieces, 6, axis=1))
    const = lambda shape: pl.BlockSpec(shape, lambda bi, h, qi, sl: (0, 0))
    grid_spec = pltpu.PrefetchScalarGridSpec(
        num_scalar_prefetch=1,
        grid=(b, N_HEADS, nk),
        in_specs=[
            pl.BlockSpec((1, t, vw), lambda bi, h, qi, sl: (bi, qi, h)),
            pl.BlockSpec((1, s, vw), lambda bi, h, qi, sl: (bi, 0, N_HEADS + h)),
            pl.BlockSpec((1, s, vw), lambda bi, h, qi, sl: (bi, 0, 2 * N_HEADS + h)),
            const((1, LANE)), const((1, LANE)), const((4, HEAD_DIM)), const((1, vw)),
        ],
        out_specs=pl.BlockSpec((1, t, vw), lambda bi, h, qi, sl: (bi, qi, h)),
        scratch_shapes=[pltpu.VMEM((2, s, vw), BF16),
                        pltpu.VMEM((2, 3, t, vw), BF16),
                        pltpu.VMEM((nk, t, t), F32),
                        pltpu.VMEM((2, t, t), F32),
                        pltpu.VMEM((t, vw), F32)],
    )
    return pl.pallas_call(
        functools.partial(_diff_attn_kernel, lam_init=lam_init, nsub=nsub, npv=npv),
        grid_spec=grid_spec,
        out_shape=jax.ShapeDtypeStruct((b, s, N_HEADS * vw), BF16),
        compiler_params=_cparams(("arbitrary", "arbitrary", "arbitrary")),
        name="diff_attention",
    )(slopes, proj, proj, proj, cq, ck, lam_params, og)


def _row(v):
    return v.astype(F32).reshape(1, -1)


def _lane_bcast(v):
    return jnp.broadcast_to(v.astype(F32)[:, None], (v.shape[0], LANE))


def kernel(x, mem, norm_mix_g, norm_xattn_g, norm_mem_g, norm_ffn_g, w_in_ab, ret_decay_fwd, ret_decay_bwd, ret_out_g, na_q_g, na_k_g, na_rpb, w_out_ab, w_in_c, diff_q_g, diff_k_g, lambda_q1, lambda_k1, lambda_q2, lambda_k2, diff_out_g, w_out_c, w_xq, w_xkv, w_xo, xq_g, xk_g, w_ffn_in, w_ffn_out):
    b, s, d = x.shape
    n_mem = mem.shape[1]
    depth = norm_mix_g.shape[0]
    t = b * s
    scale = HEAD_DIM ** -0.5
    width = N_HEADS * HEAD_DIM
    xw = N_X_HEADS * HEAD_DIM
    tm, tn, tn_in = ROW_TILE, COL_TILE, MIXER_COL_TILE

    xf = x.reshape(t, d)
    memf = mem.reshape(b * n_mem, d)
    ones_cols = lambda n: jnp.ones((1, n), F32)
    kv_gcol = jnp.concatenate([jnp.tile(xk_g.astype(F32), (1, N_X_HEADS)), jnp.ones((depth, xw), F32)], axis=1)
    kv_all = memory_kv(memf, norm_mem_g.astype(F32)[:, None, :], w_xkv, kv_gcol[:, None, :], tn=xw)

    for i in range(depth):
        j = i // 2
        if i % 2 == 0:
            gcol = jnp.concatenate([
                ones_cols(4 * width),
                jnp.tile(_row(na_q_g[j]) * scale, (1, N_HEADS)),
                jnp.tile(_row(na_k_g[j]), (1, N_HEADS)),
                ones_cols(width)], axis=1)
            proj = norm_matmul(xf, _row(norm_mix_g[i]), w_in_ab, j, gcol,
                               4 * width // tn_in, 6 * width // tn_in, tm=tm, tn=tn_in)
            proj = proj.reshape(b, s, 7 * width)
            lgf = _lane_bcast(jax.nn.log_sigmoid(ret_decay_fwd[j].astype(F32)))
            lgb = _lane_bcast(jax.nn.log_sigmoid(ret_decay_bwd[j].astype(F32)))
            n_chunks = s // RET_CHUNK
            sb = ret_states(proj, lgb, chunk=RET_CHUNK, per_step=math.gcd(n_chunks, RET_STATE_CHUNKS_PER_STEP))
            ret = ret_out(proj, sb, lgf, lgb, _row(ret_out_g[j]), chunk=RET_CHUNK,
                          per_step=math.gcd(n_chunks, RET_OUT_CHUNKS_PER_STEP))
            na = neighborhood_attention(proj, na_rpb[j])
            mixed = [ret.reshape(t, width), na.reshape(t, width)]
            xf = matmul_res(mixed, w_out_ab, j, xf, tm=tm, tn=tn_in, weight_outer=True)
        else:
            cw = 2 * width
            gcol = jnp.concatenate([
                jnp.tile(_row(diff_q_g[j]) * (scale * LOG2E), (1, 2 * N_HEADS)),
                jnp.tile(_row(diff_k_g[j]), (1, 2 * N_HEADS)),
                ones_cols(cw)], axis=1)
            proj = norm_matmul(xf, _row(norm_mix_g[i]), w_in_c, j, gcol, 0, 2 * cw // tn_in, tm=tm, tn=tn_in)
            proj = proj.reshape(b, s, 3 * cw)
            lam_params = jnp.stack([lambda_q1[j], lambda_k1[j], lambda_q2[j], lambda_k2[j]]).astype(F32)
            lam_init = 0.8 - 0.6 * math.exp(-0.3 * i)
            att = diff_attention(proj, lam_params, _row(diff_out_g[j]), lam_init=lam_init, t=ATTN_TILE)
            xf = matmul_res([att.reshape(t, cw)], w_out_c, j, xf, tm=tm, tn=tn_in, weight_outer=True)

        gq = jnp.tile(_row(xq_g[i]) * scale, (1, N_X_HEADS))
        xf, xn_ffn = xattn(xf, _row(norm_xattn_g[i]), w_xq, gq, kv_all, w_xo, i, _row(norm_ffn_g[i]),
                           tm=XATTN_ROW_TILE, seq=s)

        hmid = swiglu_in(xn_ffn, w_ffn_in, i, tm=tm, tn=tn)
        xf = matmul_res([hmid], w_ffn_out, i, xf, tm=tm, tn=tn, weight_outer=True)

    return xf.reshape(b, s, d)
```

```python
import functools
import math

import jax
import jax.numpy as jnp
import numpy as np
from jax import lax
from jax.experimental import pallas as pl
from jax.experimental.pallas import tpu as pltpu

F32 = jnp.float32
BF16 = jnp.bfloat16

HEAD_DIM = 128
N_HEADS = 8
N_X_HEADS = 4
GRID_W = 64
NA_WIN_ROWS = 8
NA_WIN_COLS = 16
NA_QROWS = 4
RET_CHUNK = 256
RET_STATE_CHUNKS_PER_STEP = 4
RET_OUT_CHUNKS_PER_STEP = 4

ROW_TILE = 1024
COL_TILE = 512
MIXER_COL_TILE = 1024
XATTN_ROW_TILE = 512
ATTN_TILE = 512
NORM_ROW_CHUNK = 128
X_TILE_SLABS = 4
RMS_EPS = 1e-6
NEG_BIG = -1e30

LANE = 128
VMEM_LIMIT = 56 * 1024 * 1024


def _cparams(sem):
    return pltpu.CompilerParams(dimension_semantics=sem, vmem_limit_bytes=VMEM_LIMIT)


def _dot_nt(a, b):
    return lax.dot_general(a, b, (((1,), (1,)), ((), ())), preferred_element_type=F32)


def _dot_nn(a, b):
    return lax.dot_general(a, b, (((1,), (0,)), ((), ())), preferred_element_type=F32)


def _dot_tn(a, b):
    return lax.dot_general(a, b, (((0,), (0,)), ((), ())), preferred_element_type=F32)


def _layer_spec(w, layer, block, index_map):
    if w.ndim == 2:
        return pl.BlockSpec(block, index_map)
    return pl.BlockSpec((None,) + tuple(block), lambda *idx: (layer,) + tuple(index_map(*idx)))


def _rms(x):
    return x * lax.rsqrt(jnp.mean(x * x, axis=-1, keepdims=True) + RMS_EPS)


def _norm_rows_to_scratch(x_refs, g_ref, xn_ref, rows):
    slab = x_refs[0].shape[0]
    for q, x_ref in enumerate(x_refs):
        def body(r, c, x_ref=x_ref, q=q):
            src = pl.ds(pl.multiple_of(r * rows, rows), rows)
            dst = pl.ds(pl.multiple_of(q * slab + r * rows, rows), rows)
            xn_ref[dst, :] = (_rms(x_ref[src, :]) * g_ref[...]).astype(BF16)
            return c

        lax.fori_loop(0, slab // rows, body, 0)


def _norm_matmul_kernel(*refs, lo, hi):
    *x_refs, g_ref, w_ref, gc_ref, o_ref, xn_ref = refs
    j = pl.program_id(1)

    @pl.when(j == 0)
    def _():
        _norm_rows_to_scratch(x_refs, g_ref, xn_ref, NORM_ROW_CHUNK)

    grouped = jnp.logical_and(j >= lo, j < hi)

    @pl.when(grouped)
    def _():
        acc = _dot_nn(xn_ref[...], w_ref[...])
        for c in range(acc.shape[1] // HEAD_DIM):
            sl = slice(c * HEAD_DIM, (c + 1) * HEAD_DIM)
            o_ref[:, sl] = (_rms(acc[:, sl]) * gc_ref[:, sl]).astype(o_ref.dtype)

    @pl.when(jnp.logical_not(grouped))
    def _():
        o_ref[...] = _dot_nn(xn_ref[...], w_ref[...]).astype(o_ref.dtype)


def norm_matmul(x, g, w, layer, gcol, lo, hi, *, tm, tn, out_dtype=BF16):
    t, d = x.shape
    n = w.shape[-1]
    ni, nj = t // tm, n // tn
    nslabs = X_TILE_SLABS if nj > X_TILE_SLABS else 1
    slab = tm // nslabs

    def x_spec(q):
        first_next = nj - nslabs + q

        def index(i, j):
            tile = i if nslabs == 1 else jnp.minimum(i + (j >= first_next).astype(jnp.int32), ni - 1)
            return tile * nslabs + q, 0
        return pl.BlockSpec((slab, d), index)

    return pl.pallas_call(
        functools.partial(_norm_matmul_kernel, lo=lo, hi=hi),
        grid=(ni, nj),
        in_specs=[x_spec(q) for q in range(nslabs)] + [
            pl.BlockSpec((1, d), lambda i, j: (0, 0)),
            _layer_spec(w, layer, (d, tn), lambda i, j: (0, j)),
            pl.BlockSpec((1, tn), lambda i, j: (0, j)),
        ],
        out_specs=pl.BlockSpec((tm, tn), lambda i, j: (i, j)),
        out_shape=jax.ShapeDtypeStruct((t, n), out_dtype),
        scratch_shapes=[pltpu.VMEM((tm, d), BF16)],
        compiler_params=_cparams(("arbitrary", "arbitrary")),
        name="norm_matmul",
    )(*([x] * nslabs), g, w, gcol)


def memory_kv(mem2d, g_all, w_all, gcol_all, *, tn):
    rows, d = mem2d.shape
    depth, _, n = w_all.shape
    return pl.pallas_call(
        functools.partial(_norm_matmul_kernel, lo=0, hi=1),
        grid=(depth, n // tn),
        in_specs=[
            pl.BlockSpec((rows, d), lambda l, j: (0, 0)),
            pl.BlockSpec((None, 1, d), lambda l, j: (l, 0, 0)),
            pl.BlockSpec((None, d, tn), lambda l, j: (l, 0, j)),
            pl.BlockSpec((None, 1, tn), lambda l, j: (l, 0, j)),
        ],
        out_specs=pl.BlockSpec((None, rows, tn), lambda l, j: (l, 0, j)),
        out_shape=jax.ShapeDtypeStruct((depth, rows, n), BF16),
        scratch_shapes=[pltpu.VMEM((rows, d), BF16)],
        compiler_params=_cparams(("arbitrary", "arbitrary")),
        name="memory_kv",
    )(mem2d, g_all, w_all, gcol_all)


def _swiglu_in_kernel(xn_ref, wg_ref, wu_ref, o_ref):
    xn = xn_ref[...]
    gate = _dot_nn(xn, wg_ref[...])
    up = _dot_nn(xn, wu_ref[...])
    o_ref[...] = (gate * jax.nn.sigmoid(gate) * up).astype(o_ref.dtype)


def swiglu_in(xn, w_in, layer, *, tm, tn):
    t, d = xn.shape
    hidden = w_in.shape[-1] // 2
    nh = hidden // tn
    return pl.pallas_call(
        _swiglu_in_kernel,
        grid=(t // tm, nh),
        in_specs=[
            pl.BlockSpec((tm, d), lambda i, j: (i, 0)),
            _layer_spec(w_in, layer, (d, tn), lambda i, j: (0, j)),
            _layer_spec(w_in, layer, (d, tn), lambda i, j: (0, j + nh)),
        ],
        out_specs=pl.BlockSpec((tm, tn), lambda i, j: (i, j)),
        out_shape=jax.ShapeDtypeStruct((t, hidden), BF16),
        compiler_params=_cparams(("parallel", "arbitrary")),
        name="swiglu_in",
    )(xn, w_in, w_in)


def _matmul_res_kernel(*refs):
    *a_refs, w_ref, r_ref, o_ref = refs
    acc = r_ref[...]
    k0 = 0
    for a_ref in a_refs:
        k1 = k0 + a_ref.shape[1]
        acc += _dot_nn(a_ref[...], w_ref[k0:k1, :])
        k0 = k1
    o_ref[...] = acc


def matmul_res(a_parts, w, layer, res, *, tm, tn, weight_outer=False):
    t = res.shape[0]
    k, n = w.shape[-2:]
    assert sum(a.shape[1] for a in a_parts) == k
    if weight_outer:
        grid, ij = (n // tn, t // tm), (lambda j, i: (i, j))
    else:
        grid, ij = (t // tm, n // tn), (lambda i, j: (i, j))
    row = lambda *g: (ij(*g)[0], 0)
    col = lambda *g: (0, ij(*g)[1])
    return pl.pallas_call(
        _matmul_res_kernel,
        grid=grid,
        in_specs=[pl.BlockSpec((tm, a.shape[1]), row) for a in a_parts] + [
            _layer_spec(w, layer, (k, tn), col),
            pl.BlockSpec((tm, tn), ij),
        ],
        out_specs=pl.BlockSpec((tm, tn), ij),
        out_shape=jax.ShapeDtypeStruct((t, n), F32),
        compiler_params=_cparams(("arbitrary", "arbitrary")),
        name="matmul_res",
    )(*a_parts, w, res)


def _xattn_kernel(x_ref, g_ref, wq_ref, gq_ref, kv_ref, wo_ref, gn_ref, o_ref, on_ref):
    x = x_ref[...]
    xn = (_rms(x) * g_ref[...]).astype(BF16)
    q = _dot_nn(xn, wq_ref[...])
    width = N_X_HEADS * HEAD_DIM
    outs = []
    for h in range(N_X_HEADS):
        sl = slice(h * HEAD_DIM, (h + 1) * HEAD_DIM)
        qh = (_rms(q[:, sl]) * gq_ref[:, sl]).astype(BF16)
        s = _dot_nt(qh, kv_ref[:, sl])
        p = jnp.exp(s - jnp.max(s, axis=-1, keepdims=True))
        l = jnp.sum(p, axis=-1, keepdims=True)
        vh = kv_ref[:, width + h * HEAD_DIM: width + (h + 1) * HEAD_DIM]
        oh = jnp.dot(p.astype(BF16), vh, preferred_element_type=F32) / l
        outs.append(oh.astype(BF16))
    o = jnp.concatenate(outs, axis=1)
    y = x + _dot_nn(o, wo_ref[...])
    o_ref[...] = y
    on_ref[...] = (_rms(y) * gn_ref[...]).astype(on_ref.dtype)


def xattn(x, g, wq, gq, kv, wo, layer, g_next, *, tm, seq):
    t, d = x.shape
    n_mem = kv.shape[-2] // (t // seq)
    width = wq.shape[-1]
    per_b = seq // tm
    return pl.pallas_call(
        _xattn_kernel,
        grid=(t // tm,),
        in_specs=[
            pl.BlockSpec((tm, d), lambda i: (i, 0)),
            pl.BlockSpec((1, d), lambda i: (0, 0)),
            _layer_spec(wq, layer, (d, width), lambda i: (0, 0)),
            pl.BlockSpec((1, width), lambda i: (0, 0)),
            _layer_spec(kv, layer, (n_mem, 2 * width), lambda i: (i // per_b, 0)),
            _layer_spec(wo, layer, (width, d), lambda i: (0, 0)),
            pl.BlockSpec((1, d), lambda i: (0, 0)),
        ],
        out_specs=[pl.BlockSpec((tm, d), lambda i: (i, 0)), pl.BlockSpec((tm, d), lambda i: (i, 0))],
        out_shape=[jax.ShapeDtypeStruct((t, d), F32), jax.ShapeDtypeStruct((t, d), BF16)],
        compiler_params=_cparams(("parallel",)),
        name="xattn",
    )(x, g, wq, gq, kv, wo, g_next)


def _ret_state_kernel(k_ref, v_ref, lgb_ref, sb_ref, state_ref, kd_ref, *, chunk):
    c = pl.program_id(1)
    per_step = k_ref.shape[1] // chunk
    scale = HEAD_DIM ** -0.5

    @pl.when(c == 0)
    def _():
        state_ref[...] = jnp.zeros_like(state_ref)
        pos = lax.broadcasted_iota(jnp.int32, (chunk, HEAD_DIM), 0).astype(F32)
        for h in range(N_HEADS):
            kd_ref[h] = jnp.exp(lgb_ref[h:h + 1, :] * pos) * scale

    for u in reversed(range(per_step)):
        rows = slice(u * chunk, (u + 1) * chunk)
        for h in range(N_HEADS):
            sl = slice(h * HEAD_DIM, (h + 1) * HEAD_DIM)
            st = state_ref[h]
            sb_ref[0, u, h] = st.astype(BF16)
            kd = (k_ref[0, rows, sl].astype(F32) * kd_ref[h]).astype(BF16)
            cdec = jnp.exp(lgb_ref[h:h + 1, :] * float(chunk))
            state_ref[h] = st * cdec + _dot_tn(kd, v_ref[0, rows, sl])


def ret_states(proj, lgb, *, chunk, per_step):
    b, s, _ = proj.shape
    nc = s // chunk
    assert nc % per_step == 0
    ns = nc // per_step
    width = N_HEADS * HEAD_DIM
    return pl.pallas_call(
        functools.partial(_ret_state_kernel, chunk=chunk),
        grid=(b, ns),
        in_specs=[
            pl.BlockSpec((1, per_step * chunk, width), lambda bi, c: (bi, ns - 1 - c, 1)),
            pl.BlockSpec((1, per_step * chunk, width), lambda bi, c: (bi, ns - 1 - c, 2)),
            pl.BlockSpec((N_HEADS, LANE), lambda bi, c: (0, 0)),
        ],
        out_specs=pl.BlockSpec((1, per_step, N_HEADS, HEAD_DIM, HEAD_DIM),
                               lambda bi, c: (bi, ns - 1 - c, 0, 0, 0)),
        out_shape=jax.ShapeDtypeStruct((b, nc, N_HEADS, HEAD_DIM, HEAD_DIM), BF16),
        scratch_shapes=[pltpu.VMEM((N_HEADS, HEAD_DIM, HEAD_DIM), F32),
                        pltpu.VMEM((N_HEADS, chunk, HEAD_DIM), F32)],
        compiler_params=_cparams(("parallel", "arbitrary")),
        name="ret_states",
    )(proj, proj, lgb)


def _ret_out_kernel(q_ref, k_ref, v_ref, rg_ref, sb_ref, lgf_ref, lgb_ref, og_ref, o_ref,
                    state_ref, dmat_ref, qdf_ref, qdb_ref, kdf_ref, *, chunk):
    c = pl.program_id(1)
    per_step = q_ref.shape[1] // chunk
    scale = HEAD_DIM ** -0.5

    @pl.when(c == 0)
    def _():
        state_ref[...] = jnp.zeros_like(state_ref)
        pos = lax.broadcasted_iota(jnp.int32, (chunk, HEAD_DIM), 0).astype(F32)
        ri = lax.broadcasted_iota(jnp.int32, (chunk, chunk), 0)
        ci = lax.broadcasted_iota(jnp.int32, (chunk, chunk), 1)
        diff = (ri - ci).astype(F32)
        for h in range(N_HEADS):
            lf = lgf_ref[h:h + 1, :]
            lb = lgb_ref[h:h + 1, :]
            dfwd = jnp.exp(lf[:, :1] * jnp.maximum(diff, 0.0))
            dbwd = jnp.exp(lb[:, :1] * jnp.maximum(-diff, 0.0))
            dmat_ref[h] = jnp.where(diff >= 0.0, dfwd, dbwd) * scale
            qdf_ref[h] = jnp.exp(lf * (pos + 1.0))
            qdb_ref[h] = jnp.exp(lb * (float(chunk) - pos))
            kdf_ref[h] = jnp.exp(lf * (float(chunk) - 1.0 - pos)) * scale

    for u in range(per_step):
        rows = slice(u * chunk, (u + 1) * chunk)
        for h in range(N_HEADS):
            sl = slice(h * HEAD_DIM, (h + 1) * HEAD_DIM)
            qh = q_ref[0, rows, sl]
            kh = k_ref[0, rows, sl]
            vh = v_ref[0, rows, sl]
            qf32 = qh.astype(F32)
            sd = (_dot_nt(qh, kh) * dmat_ref[h]).astype(BF16)
            st = state_ref[h]
            out = jnp.dot(sd, vh, preferred_element_type=F32)
            out += jnp.dot((qf32 * qdf_ref[h]).astype(BF16), st.astype(BF16), preferred_element_type=F32)
            out += jnp.dot((qf32 * qdb_ref[h]).astype(BF16), sb_ref[0, u, h], preferred_element_type=F32)
            gate = rg_ref[0, rows, sl].astype(F32)
            y = _rms(out) * og_ref[...]
            o_ref[0, rows, sl] = (y * (gate * jax.nn.sigmoid(gate))).astype(o_ref.dtype)
            kd = (kh.astype(F32) * kdf_ref[h]).astype(BF16)
            cdec = jnp.exp(lgf_ref[h:h + 1, :] * float(chunk))
            state_ref[h] = st * cdec + _dot_tn(kd, vh)


def ret_out(proj, sb, lgf, lgb, og, *, chunk, per_step):
    b, s, _ = proj.shape
    nc = s // chunk
    assert nc % per_step == 0
    width = N_HEADS * HEAD_DIM
    col = lambda k: pl.BlockSpec((1, per_step * chunk, width), lambda bi, c, k=k: (bi, c, k))
    return pl.pallas_call(
        functools.partial(_ret_out_kernel, chunk=chunk),
        grid=(b, nc // per_step),
        in_specs=[
            col(0), col(1), col(2), col(3),
            pl.BlockSpec((1, per_step, N_HEADS, HEAD_DIM, HEAD_DIM), lambda bi, c: (bi, c, 0, 0, 0)),
            pl.BlockSpec((N_HEADS, LANE), lambda bi, c: (0, 0)),
            pl.BlockSpec((N_HEADS, LANE), lambda bi, c: (0, 0)),
            pl.BlockSpec((1, HEAD_DIM), lambda bi, c: (0, 0)),
        ],
        out_specs=pl.BlockSpec((1, per_step * chunk, width), lambda bi, c: (bi, c, 0)),
        out_shape=jax.ShapeDtypeStruct((b, s, width), BF16),
        scratch_shapes=[pltpu.VMEM((N_HEADS, HEAD_DIM, HEAD_DIM), F32),
                        pltpu.VMEM((N_HEADS, chunk, chunk), F32),
                        pltpu.VMEM((N_HEADS, chunk, HEAD_DIM), F32),
                        pltpu.VMEM((N_HEADS, chunk, HEAD_DIM), F32),
                        pltpu.VMEM((N_HEADS, chunk, HEAD_DIM), F32)],
        compiler_params=_cparams(("parallel", "arbitrary")),
        name="ret_out",
    )(proj, proj, proj, proj, sb, lgf, lgb, og)


N_DR = 2 * NA_WIN_ROWS - 1
N_DC = 2 * NA_WIN_COLS - 1


def _na_kernel(rpb_ref, q_ref, kp_ref, kc_ref, kn_ref, vp_ref, vc_ref, vn_ref, o_ref,
               bcol_ref, slab_ref, *, rows):
    qb = pl.program_id(1)
    nb = rows // NA_QROWS
    neg_tile = jnp.full((GRID_W, GRID_W), NEG_BIG, F32)

    @pl.when(jnp.logical_and(pl.program_id(0) == 0, qb == 0))
    def _():
        qc = lax.broadcasted_iota(jnp.int32, (GRID_W, GRID_W), 0)
        kc = lax.broadcasted_iota(jnp.int32, (GRID_W, GRID_W), 1)
        cs = jnp.clip(qc - NA_WIN_COLS // 2, 0, GRID_W - NA_WIN_COLS)
        col_ok = jnp.logical_and(kc >= cs, kc < cs + NA_WIN_COLS)
        dc = kc - qc + (NA_WIN_COLS - 1)
        for h in range(N_HEADS):
            for dr in range(N_DR):
                tile = neg_tile
                for d in range(N_DC):
                    tile = jnp.where(dc == d, rpb_ref[(h * N_DR + dr) * N_DC + d], tile)
                bcol_ref[h, dr] = jnp.where(col_ok, tile, NEG_BIG)

    def assemble(qb_static):
        for h in range(N_HEADS):
            for j in range(NA_QROWS):
                r = NA_QROWS * qb_static + j
                rs = min(max(r - NA_WIN_ROWS // 2, 0), rows - NA_WIN_ROWS)
                for i2 in range(0, 3 * NA_QROWS, 2):
                    halves = []
                    for i in (i2, i2 + 1):
                        kr = NA_QROWS * (qb_static - 1) + i
                        ok = rs <= kr < rs + NA_WIN_ROWS
                        halves.append(bcol_ref[h, kr - r + NA_WIN_ROWS - 1] if ok else neg_tile)
                    slab_ref[h, j * GRID_W:(j + 1) * GRID_W, i2 * GRID_W:(i2 + 2) * GRID_W] = (
                        jnp.concatenate(halves, axis=1))

    for qb_static in (0, 1, nb - 1):
        pl.when(qb == qb_static)(functools.partial(assemble, qb_static))

    for h in range(N_HEADS):
        sl = slice(h * HEAD_DIM, (h + 1) * HEAD_DIM)
        k = jnp.concatenate([kp_ref[0, :, sl], kc_ref[0, :, sl], kn_ref[0, :, sl]], axis=0)
        v = jnp.concatenate([vp_ref[0, :, sl], vc_ref[0, :, sl], vn_ref[0, :, sl]], axis=0)
        s = _dot_nt(q_ref[0, :, sl], k) + slab_ref[h]
        p = jnp.exp(s - jnp.max(s, axis=-1, keepdims=True))
        l = jnp.sum(p, axis=-1, keepdims=True)
        o = jnp.dot(p.astype(BF16), v, preferred_element_type=F32) / l
        o_ref[0, :, sl] = o.astype(o_ref.dtype)


def neighborhood_attention(proj, rpb):
    b, s, _ = proj.shape
    rows = s // GRID_W
    assert NA_QROWS >= NA_WIN_ROWS // 2 and NA_QROWS % 2 == 0 and rows % NA_QROWS == 0
    assert rows >= 3 * NA_QROWS and rows >= NA_WIN_ROWS
    nq = NA_QROWS * GRID_W
    nb = rows // NA_QROWS
    width = N_HEADS * HEAD_DIM

    def blk(group, shift):
        return pl.BlockSpec((1, nq, width),
                            lambda bi, qb, rp: (bi, jnp.clip(qb + shift, 0, nb - 1), group))

    grid_spec = pltpu.PrefetchScalarGridSpec(
        num_scalar_prefetch=1,
        grid=(b, nb),
        in_specs=[blk(4, 0), blk(5, -1), blk(5, 0), blk(5, 1), blk(6, -1), blk(6, 0), blk(6, 1)],
        out_specs=pl.BlockSpec((1, nq, width), lambda bi, qb, rp: (bi, qb, 0)),
        scratch_shapes=[pltpu.VMEM((N_HEADS, N_DR, GRID_W, GRID_W), F32),
                        pltpu.VMEM((N_HEADS, nq, 3 * nq), F32)],
    )
    return pl.pallas_call(
        functools.partial(_na_kernel, rows=rows),
        grid_spec=grid_spec,
        out_shape=jax.ShapeDtypeStruct((b, s, width), BF16),
        compiler_params=_cparams(("arbitrary", "arbitrary")),
        name="neighborhood_attention",
    )(rpb.astype(F32).reshape(-1), proj, proj, proj, proj, proj, proj, proj)


def _bf16_pieces(x, n):
    out, rest = [], float(x)
    for _ in range(n):
        piece = float(np.float32(rest).astype(BF16).astype(np.float32))
        out.append(piece)
        rest -= piece
    return out


LOG2E = math.log2(math.e)
LOG2E_PIECES = _bf16_pieces(LOG2E, 3)
N_ALIBI_COLS = 12
POS_LO_BITS = 7


def _alibi_cols(pos, slope, lane, unit_cols, sign):
    hi = (pos >> POS_LO_BITS).astype(F32) * (slope * float(1 << POS_LO_BITS))
    lo = (pos & ((1 << POS_LO_BITS) - 1)).astype(F32) * slope
    if sign > 0:
        return jnp.where(lane < 3, hi, jnp.where(lane < 6, lo, unit_cols))
    return jnp.where(lane < 6, unit_cols, jnp.where(lane < 9, hi, jnp.where(lane < N_ALIBI_COLS, lo, 0.0)))


def _diff_attn_kernel(slopes_ref, q_ref, k_ref, v_ref, cq_ref, ck_ref, lam_ref, og_ref, o_ref,
                      kaug_ref, qaug_ref, s_ref, bias_ref, acc_ref, *, lam_init, nsub, npv):
    h = pl.program_id(1)
    qi = pl.program_id(2)
    t = q_ref.shape[1]
    nk = k_ref.shape[1] // t
    nslab = t // LANE
    slope = slopes_ref[h]
    lane = lax.broadcasted_iota(jnp.int32, (t, LANE), 1)
    row = lax.broadcasted_iota(jnp.int32, (t, LANE), 0)

    @pl.when(qi == 0)
    def _():
        def body(r, carry):
            rs = pl.multiple_of(r * t, t)
            kx = _alibi_cols(row + rs, slope, lane, ck_ref[...], 1).astype(BF16)
            for c in range(2):
                kaug_ref[c, pl.ds(rs, t), 0:HEAD_DIM] = k_ref[0, pl.ds(rs, t), c * HEAD_DIM:(c + 1) * HEAD_DIM]
                kaug_ref[c, pl.ds(rs, t), HEAD_DIM:2 * HEAD_DIM] = kx
            return carry

        lax.fori_loop(0, nk, body, 0)
        ri = lax.broadcasted_iota(jnp.int32, (t, t), 0)
        ci = lax.broadcasted_iota(jnp.int32, (t, t), 1)
        bias_ref[0] = jnp.zeros((t, t), F32)
        bias_ref[1] = jnp.abs(ri - ci).astype(F32) * (-LOG2E * slope)

    qx = _alibi_cols(row + qi * t, slope, lane, cq_ref[...], -1)
    for c in range(2):
        qc = q_ref[0, :, c * HEAD_DIM:(c + 1) * HEAD_DIM]
        for side, ext in ((0, qx), (1, -qx), (2, jnp.zeros_like(qx))):
            qaug_ref[c, side, :, 0:HEAD_DIM] = qc
            qaug_ref[c, side, :, HEAD_DIM:2 * HEAD_DIM] = ext.astype(BF16)

    width = npv * t
    ngroups = nk // npv
    outs = []
    for c in range(2):
        def scores(g, m):
            for i in range(nsub):
                kb = g * nsub + i
                side = jnp.where(kb < qi, 0, jnp.where(kb > qi, 1, 2))
                ks = pl.multiple_of(kb * t, t)
                s = _dot_nt(qaug_ref[c, side], kaug_ref[c, pl.ds(ks, t), :])
                s = s + bias_ref[(kb == qi).astype(jnp.int32)]
                s_ref[kb] = s
                for u in range(nslab):
                    m = jnp.maximum(m, s[:, u * LANE:(u + 1) * LANE])
            return m

        m = lax.fori_loop(0, nk // nsub, scores, jnp.full((t, LANE), NEG_BIG, F32))
        mb = jnp.broadcast_to(jnp.max(m, axis=-1, keepdims=True), (t, LANE))
        if ngroups > 1:
            acc_ref[...] = jnp.zeros_like(acc_ref)

        def pv(g, lrun):
            ps = []
            for i in range(npv):
                for u in range(nslab):
                    p = jnp.exp2(s_ref[g * npv + i, :, u * LANE:(u + 1) * LANE] - mb)
                    lrun = lrun + p
                    ps.append(p.astype(BF16))
            vs = v_ref[0, pl.ds(pl.multiple_of(g * width, width), width), :]
            pv_part = jnp.dot(jnp.concatenate(ps, axis=1), vs, preferred_element_type=F32)
            acc_ref[...] = pv_part if ngroups == 1 else acc_ref[...] + pv_part
            return lrun

        trips = ngroups + jnp.minimum(qi, 0)
        lrun = lax.fori_loop(0, trips, pv, jnp.zeros((t, LANE), F32))
        outs.append(acc_ref[...] / jnp.sum(lrun, axis=-1, keepdims=True))

    lam = (jnp.exp(jnp.sum(lam_ref[0:1, :] * lam_ref[1:2, :], axis=-1, keepdims=True))
           - jnp.exp(jnp.sum(lam_ref[2:3, :] * lam_ref[3:4, :], axis=-1, keepdims=True)) + lam_init)
    o = outs[0] - lam * outs[1]
    o_ref[0] = (_rms(o) * og_ref[...] * (1.0 - lam_init)).astype(o_ref.dtype)


def diff_attention(proj, lam_params, og, *, lam_init, t):
    b, s, _ = proj.shape
    assert s % t == 0 and s <= (1 << (2 * POS_LO_BITS + 1)) and t % LANE == 0
    nk = s // t
    nsub = max(n for n in (16, 8, 4, 3, 2, 1) if nk % n == 0)
    npv = max(n for n in (16, 8, 4, 3, 2, 1) if nk % n == 0)
    vw = 2 * HEAD_DIM
    slopes = jnp.asarray(2.0 ** (-8.0 * np.arange(1, N_HEADS + 1) / N_HEADS), F32)
    pieces = np.zeros((1, LANE), np.float32)
    pieces[0, :6] = LOG2E_PIECES * 2
    cq = jnp.asarray(pieces)
    ck = jnp.asarray(-np.roll(pieces, 6, axis=1))
    const = lambda shape: pl.BlockSpec(shape, lambda bi, h, qi, sl: (0, 0))
    grid_spec = pltpu.PrefetchScalarGridSpec(
        num_scalar_prefetch=1,
        grid=(b, N_HEADS, nk),
        in_specs=[
            pl.BlockSpec((1, t, vw), lambda bi, h, qi, sl: (bi, qi, h)),
            pl.BlockSpec((1, s, vw), lambda bi, h, qi, sl: (bi, 0, N_HEADS + h)),
            pl.BlockSpec((1, s, vw), lambda bi, h, qi, sl: (bi, 0, 2 * N_HEADS + h)),
            const((1, LANE)), const((1, LANE)), const((4, HEAD_DIM)), const((1, vw)),
        ],
        out_specs=pl.BlockSpec((1, t, vw), lambda bi, h, qi, sl: (bi, qi, h)),
        scratch_shapes=[pltpu.VMEM((2, s, vw), BF16),
                        pltpu.VMEM((2, 3, t, vw), BF16),
                        pltpu.VMEM((nk, t, t), F32),
                        pltpu.VMEM((2, t, t), F32),
                        pltpu.VMEM((t, vw), F32)],
    )
    return pl.pallas_call(
        functools.partial(_diff_attn_kernel, lam_init=lam_init, nsub=nsub, npv=npv),
        grid_spec=grid_spec,
        out_shape=jax.ShapeDtypeStruct((b, s, N_HEADS * vw), BF16),
        compiler_params=_cparams(("arbitrary", "arbitrary", "arbitrary")),
        name="diff_attention",
    )(slopes, proj, proj, proj, cq, ck, lam_params, og)


def _row(v):
    return v.astype(F32).reshape(1, -1)


def _lane_bcast(v):
    return jnp.broadcast_to(v.astype(F32)[:, None], (v.shape[0], LANE))


def kernel(x, mem, norm_mix_g, norm_xattn_g, norm_mem_g, norm_ffn_g, w_in_ab, ret_decay_fwd, ret_decay_bwd, ret_out_g, na_q_g, na_k_g, na_rpb, w_out_ab, w_in_c, diff_q_g, diff_k_g, lambda_q1, lambda_k1, lambda_q2, lambda_k2, diff_out_g, w_out_c, w_xq, w_xkv, w_xo, xq_g, xk_g, w_ffn_in, w_ffn_out):
    b, s, d = x.shape
    n_mem = mem.shape[1]
    depth = norm_mix_g.shape[0]
    t = b * s
    scale = HEAD_DIM ** -0.5
    width = N_HEADS * HEAD_DIM
    xw = N_X_HEADS * HEAD_DIM
    tm, tn, tn_in = ROW_TILE, COL_TILE, MIXER_COL_TILE

    xf = x.reshape(t, d)
    memf = mem.reshape(b * n_mem, d)
    ones_cols = lambda n: jnp.ones((1, n), F32)
    kv_gcol = jnp.concatenate([jnp.tile(xk_g.astype(F32), (1, N_X_HEADS)), jnp.ones((depth, xw), F32)], axis=1)
    kv_all = memory_kv(memf, norm_mem_g.astype(F32)[:, None, :], w_xkv, kv_gcol[:, None, :], tn=xw)

    for i in range(depth):
        j = i // 2
        if i % 2 == 0:
            gcol = jnp.concatenate([
                ones_cols(4 * width),
                jnp.tile(_row(na_q_g[j]) * scale, (1, N_HEADS)),
                jnp.tile(_row(na_k_g[j]), (1, N_HEADS)),
                ones_cols(width)], axis=1)
            proj = norm_matmul(xf, _row(norm_mix_g[i]), w_in_ab, j, gcol,
                               4 * width // tn_in, 6 * width // tn_in, tm=tm, tn=tn_in)
            proj = proj.reshape(b, s, 7 * width)
            lgf = _lane_bcast(jax.nn.log_sigmoid(ret_decay_fwd[j].astype(F32)))
            lgb = _lane_bcast(jax.nn.log_sigmoid(ret_decay_bwd[j].astype(F32)))
            n_chunks = s // RET_CHUNK
            sb = ret_states(proj, lgb, chunk=RET_CHUNK, per_step=math.gcd(n_chunks, RET_STATE_CHUNKS_PER_STEP))
            ret = ret_out(proj, sb, lgf, lgb, _row(ret_out_g[j]), chunk=RET_CHUNK,
                          per_step=math.gcd(n_chunks, RET_OUT_CHUNKS_PER_STEP))
            na = neighborhood_attention(proj, na_rpb[j])
            mixed = [ret.reshape(t, width), na.reshape(t, width)]
            xf = matmul_res(mixed, w_out_ab, j, xf, tm=tm, tn=tn_in, weight_outer=True)
        else:
            cw = 2 * width
            gcol = jnp.concatenate([
                jnp.tile(_row(diff_q_g[j]) * (scale * LOG2E), (1, 2 * N_HEADS)),
                jnp.tile(_row(diff_k_g[j]), (1, 2 * N_HEADS)),
                ones_cols(cw)], axis=1)
            proj = norm_matmul(xf, _row(norm_mix_g[i]), w_in_c, j, gcol, 0, 2 * cw // tn_in, tm=tm, tn=tn_in)
            proj = proj.reshape(b, s, 3 * cw)
            lam_params = jnp.stack([lambda_q1[j], lambda_k1[j], lambda_q2[j], lambda_k2[j]]).astype(F32)
            lam_init = 0.8 - 0.6 * math.exp(-0.3 * i)
            att = diff_attention(proj, lam_params, _row(diff_out_g[j]), lam_init=lam_init, t=ATTN_TILE)
            xf = matmul_res([att.reshape(t, cw)], w_out_c, j, xf, tm=tm, tn=tn_in, weight_outer=True)

        gq = jnp.tile(_row(xq_g[i]) * scale, (1, N_X_HEADS))
        xf, xn_ffn = xattn(xf, _row(norm_xattn_g[i]), w_xq, gq, kv_all, w_xo, i, _row(norm_ffn_g[i]),
                           tm=XATTN_ROW_TILE, seq=s)

        hmid = swiglu_in(xn_ffn, w_ffn_in, i, tm=tm, tn=tn)
        xf = matmul_res([hmid], w_ffn_out, i, xf, tm=tm, tn=tn, weight_outer=True)

    return xf.reshape(b, s, d)
```
